```python
import math
import jax, jax.numpy as jnp
from jax import lax
import numpy as np

D_MODEL = 2048
BATCH = 1
SEQ = 8192
DEPTH = 2

D_MIX = D_MODEL
BR = D_MIX // 4
CONV_A_WIDTH = 3
ATT_HEADS = 8
ATT_HEAD_DIM = BR // ATT_HEADS
DILATIONS = ((128, 1), (512, 4), (2048, 16))
BLK = 128
REL_BUCKETS = 32
REL_MAX_DIST = 2048
LRU_HEADS = 8
LRU_HEAD_DIM = BR // LRU_HEADS
CONV_C_WIDTH = 4
LRU_C = 8.0
S5_CH = 16
S5_GROUPS = BR // S5_CH
S5_STATE = 64
N_IN = 4 * BR + 4 * BR + 2 * BR + 2 * BR
ALPHA = (2 * DEPTH) ** 0.25
BETA = (8 * DEPTH) ** -0.25
LN_EPS = 1e-5

kernel_name = "hybrid_parallel_conv_dilattn_rglru_s5"


def layer_norm(x, g, b):
    xf = x.astype(jnp.float32)
    mu = jnp.mean(xf, axis=-1, keepdims=True)
    var = jnp.mean(jnp.square(xf - mu), axis=-1, keepdims=True)
    return ((xf - mu) * lax.rsqrt(var + LN_EPS)).astype(x.dtype) * g + b


def causal_dwconv(x, w):
    K = w.shape[0]
    S = x.shape[1]
    xp = jnp.pad(x, ((0, 0), (K - 1, 0), (0, 0)))
    y = xp[:, :S] * w[0]
    for j in range(1, K):
        y = y + xp[:, j:j + S] * w[j]
    return y


def t5_bucket(dist):
    max_exact = REL_BUCKETS // 2
    nf = jnp.maximum(dist, 1).astype(jnp.float32)
    large = max_exact + (jnp.log(nf / max_exact) / math.log(REL_MAX_DIST / max_exact)
                         * (REL_BUCKETS - max_exact)).astype(jnp.int32)
    large = jnp.minimum(large, REL_BUCKETS - 1)
    return jnp.where(dist < max_exact, dist, large)


def dilated_group(q, k, v, rel_bias, window, dil):
    Bsz, S, H, hd = q.shape
    span = window // dil
    assert span <= BLK
    unit = dil * BLK
    S_pad = -(-S // unit) * unit
    nb = S_pad // unit
    pad = ((0, 0), (0, S_pad - S), (0, 0), (0, 0))

    def split(t):
        return jnp.pad(t, pad).reshape(Bsz, nb, BLK, dil, H, hd)

    def with_prev(t):
        prev = jnp.pad(t, ((0, 0), (1, 0), (0, 0), (0, 0), (0, 0), (0, 0)))[:, :-1]
        return jnp.concatenate([prev, t], axis=2)

    qb = split(q)
    kk = with_prev(split(k))
    vv = with_prev(split(v))
    s = jnp.einsum('bnirhd,bnjrhd->bnrhij', qb, kk).astype(jnp.float32)

    i = jnp.arange(BLK)[:, None]
    j = jnp.arange(2 * BLK)[None, :]
    delta = i + BLK - j
    bucket = t5_bucket(jnp.clip(delta, 0, span) * dil)
    bias = jnp.transpose(rel_bias[bucket], (2, 0, 1)).astype(jnp.float32)
    valid = (delta >= 0) & (delta <= span)
    has_prev = (jnp.arange(nb)[:, None, None] > 0) | (j >= BLK)[None]
    mask = valid[None] & has_prev

    s = jnp.where(mask[None, :, None, None], s + bias, -1e30)
    m = jnp.max(s, axis=-1, keepdims=True)
    p = jnp.exp(s - m)
    l = jnp.sum(p, axis=-1, keepdims=True)
    o = jnp.einsum('bnrhij,bnjrhd->bnirhd', (p / l).astype(v.dtype), vv)
    lse = (m + jnp.log(l))[..., 0]
    lse = jnp.transpose(lse, (0, 1, 4, 2, 3)).reshape(Bsz, S_pad, H)[:, :S]
    o = o.reshape(Bsz, S_pad, H, hd)[:, :S]
    return o, lse


def dilated_attention(q, k, v, rel_bias):
    outs, lses = [], []
    for window, dil in DILATIONS:
        o, lse = dilated_group(q, k, v, rel_bias, window, dil)
        outs.append(o)
        lses.append(lse)
    w = jax.nn.softmax(jnp.stack(lses, axis=0), axis=0)
    return jnp.einsum('gbsh,gbshd->bshd', w.astype(q.dtype), jnp.stack(outs, axis=0))


def linear_combine(e1, e2):
    a1, b1 = e1
    a2, b2 = e2
    return a1 * a2, a2 * b1 + b2


def complex_combine(e1, e2):
    ar1, ai1, br1, bi1 = e1
    ar2, ai2, br2, bi2 = e2
    return (ar2 * ar1 - ai2 * ai1,
            ar2 * ai1 + ai2 * ar1,
            ar2 * br1 - ai2 * bi1 + br2,
            ar2 * bi1 + ai2 * br1 + bi2)


def rglru_branch(xb, conv_w, conv_b, wa, ba, wx, bx, lam):
    Bsz, S, _ = xb.shape
    xc = causal_dwconv(xb, conv_w) + conv_b
    xh = xc.reshape(Bsz, S, LRU_HEADS, LRU_HEAD_DIM)
    r = jax.nn.sigmoid(jnp.einsum('bshi,hij->bshj', xh, wa).reshape(Bsz, S, BR) + ba)
    ig = jax.nn.sigmoid(jnp.einsum('bshi,hij->bshj', xh, wx).reshape(Bsz, S, BR) + bx)
    log_a = -LRU_C * r * jax.nn.softplus(-lam)
    a = jnp.exp(log_a)
    mult = jnp.sqrt(-jnp.expm1(2.0 * log_a))
    _, h = lax.associative_scan(linear_combine, (a, mult * ig * xc), axis=1)
    return h


def s5_branch(u, lam_re, lam_im, log_dt, b_re, b_im, c_re, c_im, d_skip, w_glu, b_glu):
    Bsz, S, _ = u.shape
    ug = u.reshape(Bsz, S, S5_GROUPS, S5_CH)
    dt = jnp.exp(log_dt)[:, None]
    mag = jnp.exp(lam_re * dt)
    ab_re = mag * jnp.cos(lam_im * dt)
    ab_im = mag * jnp.sin(lam_im * dt)
    den = lam_re * lam_re + lam_im * lam_im
    f_re = ((ab_re - 1.0) * lam_re + ab_im * lam_im) / den
    f_im = (ab_im * lam_re - (ab_re - 1.0) * lam_im) / den
    bb_re = f_re[..., None] * b_re - f_im[..., None] * b_im
    bb_im = f_re[..., None] * b_im + f_im[..., None] * b_re
    bu_re = jnp.einsum('bsgc,gpc->bsgp', ug, bb_re)
    bu_im = jnp.einsum('bsgc,gpc->bsgp', ug, bb_im)
    a_re = jnp.broadcast_to(ab_re, bu_re.shape)
    a_im = jnp.broadcast_to(ab_im, bu_im.shape)
    _, _, x_re, x_im = lax.associative_scan(complex_combine, (a_re, a_im, bu_re, bu_im), axis=1)
    y = jnp.einsum('gcp,bsgp->bsgc', c_re, x_re) - jnp.einsum('gcp,bsgp->bsgc', c_im, x_im)
    y = y.reshape(Bsz, S, BR) + d_skip * u
    y = jax.nn.gelu(y)
    return y * jax.nn.sigmoid(y @ w_glu + b_glu)


def setup_inputs(seed: int = 0) -> dict:
    key = jax.random.key(seed)
    ks = jax.random.split(key, 32)

    def nrm(k, shape, scale):
        return scale * jax.random.normal(k, shape, jnp.float32)

    HD = LRU_HEAD_DIM
    a_c = jax.random.uniform(ks[13], (DEPTH, BR), jnp.float32, minval=0.9, maxval=0.999)
    a0 = a_c ** (1.0 / LRU_C)
    lru_lambda = jnp.log(a0) - jnp.log1p(-a0)
    n_idx = jnp.arange(S5_STATE, dtype=jnp.float32)
    return {
        "x": nrm(ks[0], (BATCH, SEQ, D_MODEL), 1.0),
        "c": nrm(ks[1], (BATCH, D_MODEL), 1.0),
        "rel_bias": nrm(ks[2], (REL_BUCKETS, ATT_HEADS), 0.5),
        "w_ada": nrm(ks[3], (DEPTH, D_MODEL, 3 * D_MODEL), 0.1 * D_MODEL ** -0.5),
        "b_ada": nrm(ks[4], (DEPTH, 3 * D_MODEL), 0.01),
        "w_in": nrm(ks[5], (DEPTH, D_MODEL, N_IN), D_MODEL ** -0.5),
        "conv_a": nrm(ks[6], (DEPTH, CONV_A_WIDTH, BR), CONV_A_WIDTH ** -0.5),
        "conv_c": nrm(ks[7], (DEPTH, CONV_C_WIDTH, BR), CONV_C_WIDTH ** -0.5),
        "conv_c_b": nrm(ks[8], (DEPTH, BR), 0.01),
        "lru_wa": nrm(ks[9], (DEPTH, LRU_HEADS, HD, HD), HD ** -0.5),
        "lru_ba": nrm(ks[10], (DEPTH, BR), 0.01),
        "lru_wx": nrm(ks[11], (DEPTH, LRU_HEADS, HD, HD), HD ** -0.5),
        "lru_bx": nrm(ks[12], (DEPTH, BR), 0.01),
        "lru_lambda": lru_lambda,
        "s5_lam_re": -0.5 + nrm(ks[14], (DEPTH, S5_GROUPS, S5_STATE), 0.01),
        "s5_lam_im": jnp.pi * n_idx + nrm(ks[15], (DEPTH, S5_GROUPS, S5_STATE), 0.01),
        "s5_log_dt": jax.random.uniform(ks[16], (DEPTH, S5_GROUPS), jnp.float32,
                                        minval=math.log(1e-3), maxval=math.log(1e-1)),
        "s5_b_re": nrm(ks[17], (DEPTH, S5_GROUPS, S5_STATE, S5_CH), (2 * S5_CH) ** -0.5),
        "s5_b_im": nrm(ks[18], (DEPTH, S5_GROUPS, S5_STATE, S5_CH), (2 * S5_CH) ** -0.5),
        "s5_c_re": nrm(ks[19], (DEPTH, S5_GROUPS, S5_CH, S5_STATE), (2 * S5_STATE) ** -0.5),
        "s5_c_im": nrm(ks[20], (DEPTH, S5_GROUPS, S5_CH, S5_STATE), (2 * S5_STATE) ** -0.5),
        "s5_d": nrm(ks[21], (DEPTH, BR), 1.0),
        "s5_w_glu": nrm(ks[22], (DEPTH, BR, BR), BR ** -0.5),
        "s5_b_glu": nrm(ks[23], (DEPTH, BR), 0.01),
        "w_out": nrm(ks[24], (DEPTH, D_MIX, D_MODEL), BETA * D_MIX ** -0.5),
        "ln_g": 1.0 + nrm(ks[25], (DEPTH, D_MODEL), 0.01),
        "ln_b": nrm(ks[26], (DEPTH, D_MODEL), 0.01),
    }


def reference(x, c, rel_bias, w_ada, b_ada, w_in, conv_a, conv_c, conv_c_b, lru_wa, lru_ba,
              lru_wx, lru_bx, lru_lambda, s5_lam_re, s5_lam_im, s5_log_dt, s5_b_re, s5_b_im,
              s5_c_re, s5_c_im, s5_d, s5_w_glu, s5_b_glu, w_out, ln_g, ln_b):
    Bsz, S, _ = x.shape
    cond = jax.nn.silu(c)
    for l in range(DEPTH):
        ada = cond @ w_ada[l] + b_ada[l]
        shift, scale, gate = jnp.split(ada, 3, axis=-1)
        h = x * (1.0 + scale[:, None]) + shift[:, None]
        proj = h @ w_in[l]
        pa, pb, pc, pd = jnp.split(proj, [4 * BR, 8 * BR, 10 * BR], axis=-1)

        a_b, a_c, a_x, a_g = jnp.split(pa, 4, axis=-1)
        y_a = a_b * causal_dwconv(a_c * a_x, conv_a[l]) * jax.nn.silu(a_g)

        q, k, v, b_g = jnp.split(pb, 4, axis=-1)
        q = q.reshape(Bsz, S, ATT_HEADS, ATT_HEAD_DIM) * (ATT_HEAD_DIM ** -0.5)
        k = k.reshape(Bsz, S, ATT_HEADS, ATT_HEAD_DIM)
        v = v.reshape(Bsz, S, ATT_HEADS, ATT_HEAD_DIM)
        y_b = dilated_attention(q, k, v, rel_bias).reshape(Bsz, S, BR) * jax.nn.silu(b_g)

        c_x, c_g = jnp.split(pc, 2, axis=-1)
        y_c = rglru_branch(c_x, conv_c[l], conv_c_b[l], lru_wa[l], lru_ba[l], lru_wx[l],
                           lru_bx[l], lru_lambda[l]) * jax.nn.silu(c_g)

        d_u, d_g = jnp.split(pd, 2, axis=-1)
        y_d = s5_branch(d_u, s5_lam_re[l], s5_lam_im[l], s5_log_dt[l], s5_b_re[l], s5_b_im[l],
                        s5_c_re[l], s5_c_im[l], s5_d[l], s5_w_glu[l], s5_b_glu[l]) * jax.nn.silu(d_g)

        y = jnp.concatenate([y_a, y_b, y_c, y_d], axis=-1) @ w_out[l]
        x = layer_norm(ALPHA * x + (1.0 + gate[:, None]) * y, ln_g[l], ln_b[l])
    return x
```

```python
import functools
import math

import numpy as np
import jax
import jax.numpy as jnp
from jax import lax
from jax.experimental import pallas as pl
from jax.experimental.pallas import tpu as pltpu

BR = 512
N_IN = 12 * BR
ATT_HEADS = 8
ATT_HEAD_DIM = 64
BLK = 128
SPAN = 128
DILATIONS = (1, 4, 16)
REL_BUCKETS = 32
REL_MAX_DIST = 2048
LRU_HEADS = 8
LRU_C = 8.0
S5_CH = 16
S5_GROUPS = 32
S5_STATE = 64
DEPTH = 2
ALPHA = (2 * DEPTH) ** 0.25
LN_EPS = 1e-5

SUBLANES = 8
LANES = 128
VMEM_LIMIT = 48 * 1024 * 1024

COL_A_B, COL_A_C, COL_A_X, COL_A_G = 0, 1, 2, 3
COL_Q, COL_K, COL_V, COL_B_G = 4, 5, 6, 7
COL_C_X, COL_C_G = 8, 9
COL_D_U, COL_D_G = 10, 11
N_COLBLK = 12


def _silu(x):
    return x * jax.nn.sigmoid(x)


def _log1p(x):
    w = 1.0 + x
    return jnp.where(w == 1.0, x, x * jnp.log(w) / (w - 1.0))


def _expm1(x):
    e = jnp.exp(x)
    return jnp.where(e == 1.0, x, (e - 1.0) * x / jnp.log(e))


def _params(*sem):
    return pltpu.CompilerParams(dimension_semantics=sem, vmem_limit_bytes=VMEM_LIMIT)


def _shift_rows(x, d, prev):
    assert 0 < d < SUBLANES and prev.shape[0] == SUBLANES
    rolled = pltpu.roll(x, d, 0)
    rolled_prev = pltpu.roll(prev, d, 0)
    row = lax.broadcasted_iota(jnp.int32, prev.shape, 0)
    top = jnp.where(row < d, rolled_prev, rolled[:SUBLANES])
    return jnp.concatenate([top, rolled[SUBLANES:]], axis=0)


def _shift_rows_fill(x, d, fill):
    n = x.shape[0]
    if d % SUBLANES == 0:
        head = jnp.full((d,) + x.shape[1:], fill, x.dtype)
        return jnp.concatenate([head, x[:n - d]], axis=0)
    rolled = pltpu.roll(x, d, 0)
    row = lax.broadcasted_iota(jnp.int32, x.shape, 0)
    return jnp.where(row < d, jnp.asarray(fill, x.dtype), rolled)


def _ada_kernel(c_ref, w_ref, b_ref, o_ref):
    cond = _silu(c_ref[...])
    o_ref[...] = jnp.sum(cond * w_ref[...], axis=0, keepdims=True) + b_ref[...]


def _ada(c_col, w_ada, b_ada, layer):
    d = c_col.shape[0]
    n = w_ada.shape[2]
    tn = 512
    return pl.pallas_call(
        _ada_kernel,
        grid=(n // tn,),
        in_specs=[
            pl.BlockSpec((d, 1), lambda j: (0, 0)),
            pl.BlockSpec((None, d, tn), lambda j: (layer, 0, j)),
            pl.BlockSpec((None, 1, tn), lambda j: (layer, 0, j)),
        ],
        out_specs=pl.BlockSpec((1, tn), lambda j: (0, j)),
        out_shape=jax.ShapeDtypeStruct((1, n), jnp.float32),
        compiler_params=_params("arbitrary"),
        name="ada",
    )(c_col, w_ada, b_ada)


def _proj_kernel(x_ref, shift_ref, scale_ref, w_ref, o_ref, h_ref):
    @pl.when(pl.program_id(1) == 0)
    def _():
        h = x_ref[...] * (1.0 + scale_ref[...]) + shift_ref[...]
        h_ref[...] = h.astype(jnp.bfloat16)

    o_ref[...] = jnp.dot(h_ref[...], w_ref[...], preferred_element_type=jnp.float32)


def _proj(x, ada, w_in_bf16, layer):
    s, d = x.shape
    n = w_in_bf16.shape[2]
    tm, tn = 1024, 512
    return pl.pallas_call(
        _proj_kernel,
        grid=(s // tm, n // tn),
        in_specs=[
            pl.BlockSpec((tm, d), lambda i, j: (i, 0)),
            pl.BlockSpec((1, d), lambda i, j: (0, 0)),
            pl.BlockSpec((1, d), lambda i, j: (0, 1)),
            pl.BlockSpec((None, d, tn), lambda i, j: (layer, 0, j)),
        ],
        out_specs=pl.BlockSpec((tm, tn), lambda i, j: (i, j)),
        out_shape=jax.ShapeDtypeStruct((s, n), jnp.float32),
        scratch_shapes=[pltpu.VMEM((tm, d), jnp.bfloat16)],
        compiler_params=_params("arbitrary", "arbitrary"),
        name="proj",
    )(x, ada, ada, w_in_bf16)


def _conv_a_kernel(ab_ref, ac_ref, ax_ref, ag_ref, w_ref, y_ref, prev_ref):
    @pl.when(pl.program_id(0) == 0)
    def _():
        prev_ref[...] = jnp.zeros_like(prev_ref)

    u = ac_ref[...] * ax_ref[...]
    prev = prev_ref[...]
    tm = u.shape[0]
    conv = _shift_rows(u, 2, prev) * w_ref[0:1, :]
    conv = conv + _shift_rows(u, 1, prev) * w_ref[1:2, :]
    conv = conv + u * w_ref[2:3, :]
    prev_ref[...] = u[tm - SUBLANES:]
    y_ref[...] = (ab_ref[...] * conv * _silu(ag_ref[...])).astype(y_ref.dtype)


def _conv_a(proj, conv_a, layer):
    s = proj.shape[0]
    tm = 512
    col = lambda c: pl.BlockSpec((tm, BR), lambda i, c=c: (i, c))
    return pl.pallas_call(
        _conv_a_kernel,
        grid=(s // tm,),
        in_specs=[col(COL_A_B), col(COL_A_C), col(COL_A_X), col(COL_A_G),
                  pl.BlockSpec((None,) + conv_a.shape[1:], lambda i: (layer, 0, 0))],
        out_specs=pl.BlockSpec((tm, BR), lambda i: (i, 0)),
        out_shape=jax.ShapeDtypeStruct((s, BR), jnp.bfloat16),
        scratch_shapes=[pltpu.VMEM((SUBLANES, BR), jnp.float32)],
        compiler_params=_params("arbitrary"),
        name="conv_a",
    )(proj, proj, proj, proj, conv_a)


def _t5_bucket_table(dil):
    i = jnp.arange(BLK)[:, None]
    j = jnp.arange(2 * BLK)[None, :]
    dist = jnp.clip(i + BLK - j, 0, SPAN) * dil
    max_exact = REL_BUCKETS // 2
    nf = jnp.maximum(dist, 1).astype(jnp.float32)
    large = max_exact + (jnp.log(nf / max_exact) / math.log(REL_MAX_DIST / max_exact)
                         * (REL_BUCKETS - max_exact)).astype(jnp.int32)
    large = jnp.minimum(large, REL_BUCKETS - 1)
    return jnp.where(dist < max_exact, dist, large).astype(jnp.int32)


def _attn_kernel(rb_ref, bucket_ref, q_ref, kp_ref, kc_ref, vp_ref, vc_ref, o_ref, lse_ref, bias_ref):
    r = pl.program_id(0)
    n = pl.program_id(1)

    @pl.when((r == 0) & (n == 0))
    def _():
        bucket = bucket_ref[...]
        hits = [bucket == b for b in range(REL_BUCKETS)]
        for h in range(ATT_HEADS):
            acc = jnp.zeros(bucket.shape, jnp.float32)
            for b in range(REL_BUCKETS):
                acc = jnp.where(hits[b], rb_ref[b, h], acc)
            bias_ref[h] = acc

    i = lax.broadcasted_iota(jnp.int32, (BLK, 2 * BLK), 0)
    j = lax.broadcasted_iota(jnp.int32, (BLK, 2 * BLK), 1)
    valid = (j >= i) & (j <= i + BLK) & ((j >= BLK) | (n > 0))
    lane = lax.broadcasted_iota(jnp.int32, (BLK, LANES), 1)
    low = lane < ATT_HEAD_DIM

    q = (q_ref[...] * (ATT_HEAD_DIM ** -0.5)).astype(jnp.bfloat16)
    k = jnp.concatenate([kp_ref[...], kc_ref[...]], axis=0).astype(jnp.bfloat16)
    v = jnp.concatenate([vp_ref[...], vc_ref[...]], axis=0).astype(jnp.bfloat16)
    zero = jnp.zeros((), jnp.bfloat16)

    for pair in range(ATT_HEADS // 2):
        sl = slice(pair * LANES, (pair + 1) * LANES)
        qp, kp, vp = q[:, sl], k[:, sl], v[:, sl]
        outs, lses = [], []
        for half in range(2):
            h = 2 * pair + half
            qh = jnp.where(low if half == 0 else ~low, qp, zero)
            sc = lax.dot_general(qh, kp, (((1,), (1,)), ((), ())),
                                 preferred_element_type=jnp.float32)
            sc = jnp.where(valid, sc + bias_ref[h], -1e30)
            m = jnp.max(sc, axis=-1, keepdims=True)
            p = jnp.exp(sc - m)
            l = jnp.sum(p, axis=-1, keepdims=True)
            pv = jnp.dot(p.astype(jnp.bfloat16), vp, preferred_element_type=jnp.float32)
            outs.append(pv / l)
            lses.append(m + jnp.log(l))
        o_ref[:, sl] = jnp.where(low, outs[0], outs[1])
        lse_ref[:, sl] = jnp.where(low, lses[0], lses[1])


def _attn_group(proj, rel_bias, dil):
    s = proj.shape[0]
    rows = s // dil
    nb = rows // BLK
    view = proj.reshape(rows, dil * N_IN)
    cur = lambda c: pl.BlockSpec((BLK, BR), lambda r, n, c=c: (n, r * N_COLBLK + c))
    prev = lambda c: pl.BlockSpec((BLK, BR), lambda r, n, c=c: (jnp.maximum(n - 1, 0), r * N_COLBLK + c))
    out = pl.BlockSpec((BLK, BR), lambda r, n: (n, r))
    o, lse = pl.pallas_call(
        _attn_kernel,
        grid=(dil, nb),
        in_specs=[
            pl.BlockSpec(memory_space=pltpu.SMEM),
            pl.BlockSpec((BLK, 2 * BLK), lambda r, n: (0, 0)),
            cur(COL_Q), prev(COL_K), cur(COL_K), prev(COL_V), cur(COL_V),
        ],
        out_specs=[out, out],
        out_shape=[jax.ShapeDtypeStruct((rows, dil * BR), jnp.float32)] * 2,
        scratch_shapes=[pltpu.VMEM((ATT_HEADS, BLK, 2 * BLK), jnp.float32)],
        compiler_params=_params("arbitrary", "arbitrary"),
        name=f"attn_d{dil}",
    )(rel_bias, _t5_bucket_table(dil), view, view, view, view, view)
    return o.reshape(s, BR), lse.reshape(s, BR)


def _attn_mix_kernel(o1, l1, o2, l2, o3, l3, g_ref, y_ref):
    a1, a2, a3 = l1[...], l2[...], l3[...]
    m = jnp.maximum(jnp.maximum(a1, a2), a3)
    e1, e2, e3 = jnp.exp(a1 - m), jnp.exp(a2 - m), jnp.exp(a3 - m)
    mix = (e1 * o1[...] + e2 * o2[...] + e3 * o3[...]) / (e1 + e2 + e3)
    y_ref[...] = (mix * _silu(g_ref[...])).astype(y_ref.dtype)


def _attn_mix(groups, proj):
    s = proj.shape[0]
    tm = 512
    full = pl.BlockSpec((tm, BR), lambda i: (i, 0))
    args = [a for pair in groups for a in pair]
    return pl.pallas_call(
        _attn_mix_kernel,
        grid=(s // tm,),
        in_specs=[full] * 6 + [pl.BlockSpec((tm, BR), lambda i: (i, COL_B_G))],
        out_specs=full,
        out_shape=jax.ShapeDtypeStruct((s, BR), jnp.bfloat16),
        compiler_params=_params("arbitrary"),
        name="attn_mix",
    )(*args, proj)


def _lru_kernel(cx_ref, cg_ref, cw_ref, cb_ref, w_ref, b_ref, lam_ref, y_ref, prev_ref, h_ref):
    @pl.when(pl.program_id(0) == 0)
    def _():
        prev_ref[...] = jnp.zeros_like(prev_ref)
        h_ref[...] = jnp.zeros_like(h_ref)

    cx = cx_ref[...]
    tm = cx.shape[0]
    prev = prev_ref[...]
    xc = _shift_rows(cx, 3, prev) * cw_ref[0:1, :]
    xc = xc + _shift_rows(cx, 2, prev) * cw_ref[1:2, :]
    xc = xc + _shift_rows(cx, 1, prev) * cw_ref[2:3, :]
    xc = xc + cx * cw_ref[3:4, :]
    xc = xc + cb_ref[...]
    prev_ref[...] = cx[tm - SUBLANES:]

    z = jnp.dot(xc.astype(jnp.bfloat16), w_ref[...], preferred_element_type=jnp.float32) + b_ref[...]
    r = jax.nn.sigmoid(z[:, :BR])
    ig = jax.nn.sigmoid(z[:, BR:])
    neg_lam = -lam_ref[...]
    softplus = jnp.maximum(neg_lam, 0.0) + _log1p(jnp.exp(-jnp.abs(neg_lam)))
    log_a = -LRU_C * r * softplus
    a = jnp.exp(log_a)
    b = jnp.sqrt(-_expm1(2.0 * log_a)) * ig * xc

    d = 1
    while d < tm:
        b = b + a * _shift_rows_fill(b, d, 0.0)
        a = a * _shift_rows_fill(a, d, 1.0)
        d *= 2
    h = b + a * h_ref[...]
    h_ref[...] = h[tm - 1:tm]
    y_ref[...] = (h * _silu(cg_ref[...])).astype(y_ref.dtype)


def _block_diag(w):
    hh, n, m = w.shape
    eye = jnp.eye(hh, dtype=w.dtype)
    return jnp.einsum('hij,hg->higj', w, eye).reshape(hh * n, hh * m)


def _lru(proj, conv_c, conv_c_b, w_cat, b_cat, lru_lambda, layer):
    s = proj.shape[0]
    tm = 512
    vec = lambda a: pl.BlockSpec((None,) + a.shape[1:], lambda i, nd=a.ndim: (layer,) + (0,) * (nd - 1))
    conv_c_b = conv_c_b.reshape(DEPTH, 1, BR)
    lam = lru_lambda.reshape(DEPTH, 1, BR)
    return pl.pallas_call(
        _lru_kernel,
        grid=(s // tm,),
        in_specs=[pl.BlockSpec((tm, BR), lambda i: (i, COL_C_X)),
                  pl.BlockSpec((tm, BR), lambda i: (i, COL_C_G)),
                  vec(conv_c), vec(conv_c_b), vec(w_cat), vec(b_cat), vec(lam)],
        out_specs=pl.BlockSpec((tm, BR), lambda i: (i, 0)),
        out_shape=jax.ShapeDtypeStruct((s, BR), jnp.bfloat16),
        scratch_shapes=[pltpu.VMEM((SUBLANES, BR), jnp.float32), pltpu.VMEM((1, BR), jnp.float32)],
        compiler_params=_params("arbitrary"),
        name="rglru",
    )(proj, proj, conv_c, conv_c_b, w_cat, b_cat, lam)


S5_COLS = 256
N_STATE = S5_GROUPS * S5_STATE
N_COLCH = N_STATE // S5_COLS


def _s5_kernel(u_ref, g_ref, bmat_ref, apr_ref, api_ref, cmat_ref, d_ref, wglu_ref, bglu_ref,
               y_ref, xr_ref, xi_ref, sr_ref, si_ref):
    @pl.when(pl.program_id(0) == 0)
    def _():
        sr_ref[...] = jnp.zeros_like(sr_ref)
        si_ref[...] = jnp.zeros_like(si_ref)

    u = u_ref[...]
    tm = u.shape[0]
    nsteps = int(math.log2(tm))
    ub = u.astype(jnp.bfloat16)
    for j in range(N_COLCH):
        lo = j * S5_COLS
        xr_ref[j] = jnp.dot(ub, bmat_ref[:, lo:lo + S5_COLS], preferred_element_type=jnp.float32)
        xi_ref[j] = jnp.dot(ub, bmat_ref[:, N_STATE + lo:N_STATE + lo + S5_COLS],
                            preferred_element_type=jnp.float32)

    def scan_cols(c, carry):
        xr = xr_ref[c]
        xi = xi_ref[c]
        for k in range(nsteps):
            d = 1 << k
            ar = apr_ref[c, d - 1:d, :]
            ai = api_ref[c, d - 1:d, :]
            sr = _shift_rows_fill(xr, d, 0.0)
            si = _shift_rows_fill(xi, d, 0.0)
            xr, xi = xr + (ar * sr - ai * si), xi + (ar * si + ai * sr)
        ar = apr_ref[c]
        ai = api_ref[c]
        cr = sr_ref[c]
        ci = si_ref[c]
        xr = xr + (ar * cr - ai * ci)
        xi = xi + (ar * ci + ai * cr)
        xr_ref[c] = xr
        xi_ref[c] = xi
        sr_ref[c] = xr[tm - 1:tm]
        si_ref[c] = xi[tm - 1:tm]
        return carry

    lax.fori_loop(0, N_COLCH, scan_cols, 0)

    y = d_ref[...] * u
    for j in range(N_COLCH):
        lo = j * S5_COLS
        y = y + jnp.dot(xr_ref[j].astype(jnp.bfloat16), cmat_ref[lo:lo + S5_COLS, :],
                        preferred_element_type=jnp.float32)
        y = y + jnp.dot(xi_ref[j].astype(jnp.bfloat16), cmat_ref[N_STATE + lo:N_STATE + lo + S5_COLS, :],
                        preferred_element_type=jnp.float32)
    y = 0.5 * y * (1.0 + jnp.tanh(math.sqrt(2.0 / math.pi) * (y + 0.044715 * (y * y * y))))
    gate = jnp.dot(y.astype(jnp.bfloat16), wglu_ref[...], preferred_element_type=jnp.float32) + bglu_ref[...]
    y = y * jax.nn.sigmoid(gate)
    y_ref[...] = (y * _silu(g_ref[...])).astype(y_ref.dtype)


def _s5_discretize(lam_re, lam_im, log_dt, b_re, b_im, c_re, c_im, tm):
    dt = jnp.exp(log_dt)[:, None]
    mag = jnp.exp(lam_re * dt)
    ab_re = mag * jnp.cos(lam_im * dt)
    ab_im = mag * jnp.sin(lam_im * dt)
    den = lam_re * lam_re + lam_im * lam_im
    f_re = ((ab_re - 1.0) * lam_re + ab_im * lam_im) / den
    f_im = (ab_im * lam_re - (ab_re - 1.0) * lam_im) / den
    bb_re = f_re[..., None] * b_re - f_im[..., None] * b_im
    bb_im = f_re[..., None] * b_im + f_im[..., None] * b_re
    bmat = jnp.concatenate([_block_diag(jnp.swapaxes(bb_re, 1, 2)),
                            _block_diag(jnp.swapaxes(bb_im, 1, 2))], axis=1)
    cmat = jnp.concatenate([_block_diag(jnp.swapaxes(c_re, 1, 2)),
                            -_block_diag(jnp.swapaxes(c_im, 1, 2))], axis=0)
    pr, pi = ab_re.reshape(1, N_STATE), ab_im.reshape(1, N_STATE)
    while pr.shape[0] < tm:
        lr, li = pr[-1:], pi[-1:]
        pr, pi = (jnp.concatenate([pr, pr * lr - pi * li], axis=0),
                  jnp.concatenate([pi, pr * li + pi * lr], axis=0))
    by_cols = lambda a: a.reshape(tm, N_COLCH, S5_COLS).transpose(1, 0, 2)
    return bmat.astype(jnp.bfloat16), cmat.astype(jnp.bfloat16), by_cols(pr), by_cols(pi)


def _s5(proj, bmat, apr, api, cmat, s5_d, w_glu_bf16, b_glu, layer, tm):
    s = proj.shape[0]
    const = lambda a: pl.BlockSpec(a.shape, lambda i, nd=a.ndim: (0,) * nd)
    vec = lambda a: pl.BlockSpec((None,) + a.shape[1:], lambda i, nd=a.ndim: (layer,) + (0,) * (nd - 1))
    s5_d = s5_d.reshape(DEPTH, 1, BR)
    b_glu = b_glu.reshape(DEPTH, 1, BR)
    return pl.pallas_call(
        _s5_kernel,
        grid=(s // tm,),
        in_specs=[pl.BlockSpec((tm, BR), lambda i: (i, COL_D_U)),
                  pl.BlockSpec((tm, BR), lambda i: (i, COL_D_G)),
                  const(bmat), const(apr), const(api), const(cmat),
                  vec(s5_d), vec(w_glu_bf16), vec(b_glu)],
        out_specs=pl.BlockSpec((tm, BR), lambda i: (i, 0)),
        out_shape=jax.ShapeDtypeStruct((s, BR), jnp.bfloat16),
        scratch_shapes=[pltpu.VMEM((N_COLCH, tm, S5_COLS), jnp.float32),
                        pltpu.VMEM((N_COLCH, tm, S5_COLS), jnp.float32),
                        pltpu.VMEM((N_COLCH, 1, S5_COLS), jnp.float32),
                        pltpu.VMEM((N_COLCH, 1, S5_COLS), jnp.float32)],
        compiler_params=_params("arbitrary"),
        name="s5",
    )(proj, proj, bmat, apr, api, cmat, s5_d, w_glu_bf16, b_glu)


def _out_kernel(x_ref, ya_ref, yb_ref, yc_ref, yd_ref, w_ref, gate_ref, g_ref, b_ref, o_ref):
    y = jnp.dot(ya_ref[...], w_ref[0 * BR:1 * BR, :], preferred_element_type=jnp.float32)
    y = y + jnp.dot(yb_ref[...], w_ref[1 * BR:2 * BR, :], preferred_element_type=jnp.float32)
    y = y + jnp.dot(yc_ref[...], w_ref[2 * BR:3 * BR, :], preferred_element_type=jnp.float32)
    y = y + jnp.dot(yd_ref[...], w_ref[3 * BR:4 * BR, :], preferred_element_type=jnp.float32)
    z = ALPHA * x_ref[...] + (1.0 + gate_ref[...]) * y
    mu = jnp.mean(z, axis=-1, keepdims=True)
    zc = z - mu
    var = jnp.mean(zc * zc, axis=-1, keepdims=True)
    o_ref[...] = zc * lax.rsqrt(var + LN_EPS) * g_ref[...] + b_ref[...]


def _out(x, ya, yb, yc, yd, w_out_bf16, ada, ln_g, ln_b, layer):
    s, d = x.shape
    tm = 256
    branch = pl.BlockSpec((tm, BR), lambda i: (i, 0))
    vec = pl.BlockSpec((None, 1, d), lambda i: (layer, 0, 0))
    return pl.pallas_call(
        _out_kernel,
        grid=(s // tm,),
        in_specs=[pl.BlockSpec((tm, d), lambda i: (i, 0)), branch, branch, branch, branch,
                  pl.BlockSpec((None, 4 * BR, d), lambda i: (layer, 0, 0)),
                  pl.BlockSpec((1, d), lambda i: (0, 2)),
                  vec, vec],
        out_specs=pl.BlockSpec((tm, d), lambda i: (i, 0)),
        out_shape=jax.ShapeDtypeStruct((s, d), jnp.float32),
        compiler_params=_params("arbitrary"),
        name="out_proj_ln",
    )(x, ya, yb, yc, yd, w_out_bf16, ada, ln_g.reshape(DEPTH, 1, d), ln_b.reshape(DEPTH, 1, d))


def kernel(x, c, rel_bias, w_ada, b_ada, w_in, conv_a, conv_c, conv_c_b, lru_wa, lru_ba, lru_wx, lru_bx, lru_lambda, s5_lam_re, s5_lam_im, s5_log_dt, s5_b_re, s5_b_im, s5_c_re, s5_c_im, s5_d, s5_w_glu, s5_b_glu, w_out, ln_g, ln_b):
    bsz, s, d = x.shape
    assert bsz == 1 and w_in.shape == (DEPTH, d, N_IN)
    s5_tm = 256
    xs = x.reshape(s, d)
    c_col = c.reshape(d, 1)
    b_ada3 = b_ada.reshape(DEPTH, 1, 3 * d)
    w_in_bf16 = w_in.astype(jnp.bfloat16)
    w_out_bf16 = w_out.astype(jnp.bfloat16)
    w_glu_bf16 = s5_w_glu.astype(jnp.bfloat16)
    lru_w = jnp.stack([jnp.concatenate([_block_diag(lru_wa[l]), _block_diag(lru_wx[l])], axis=1)
                       for l in range(DEPTH)]).astype(jnp.bfloat16)
    lru_b = jnp.concatenate([lru_ba, lru_bx], axis=1).reshape(DEPTH, 1, 2 * BR)

    for l in range(DEPTH):
        ada = _ada(c_col, w_ada, b_ada3, l)
        proj = _proj(xs, ada, w_in_bf16, l)
        ya = _conv_a(proj, conv_a, l)
        yb = _attn_mix([_attn_group(proj, rel_bias, dil) for dil in DILATIONS], proj)
        yc = _lru(proj, conv_c, conv_c_b, lru_w, lru_b, lru_lambda, l)
        bmat, cmat, apr, api = _s5_discretize(
            s5_lam_re[l], s5_lam_im[l], s5_log_dt[l], s5_b_re[l], s5_b_im[l], s5_c_re[l], s5_c_im[l], s5_tm)
        yd = _s5(proj, bmat, apr, api, cmat, s5_d, w_glu_bf16, s5_b_glu, l, s5_tm)
        xs = _out(xs, ya, yb, yc, yd, w_out_bf16, ada, ln_g, ln_b, l)
    return xs.reshape(bsz, s, d)
```

```python
import functools
import math

import jax
import jax.numpy as jnp
from jax import lax
from jax.experimental import pallas as pl
from jax.experimental.pallas import tpu as pltpu

BR = 512
N_IN = 12 * BR
ATT_HEADS = 8
ATT_HEAD_DIM = 64
BLK = 128
SPAN = 128
DILATIONS = (1, 4, 16)
REL_BUCKETS = 32
REL_MAX_DIST = 2048
LRU_HEADS = 8
LRU_C = 8.0
S5_CH = 16
S5_GROUPS = 32
S5_STATE = 64
DEPTH = 2
ALPHA = (2 * DEPTH) ** 0.25
LN_EPS = 1e-5

SUBLANES = 8
LANES = 128
VMEM_LIMIT = 48 * 1024 * 1024

COL_A_B, COL_A_C, COL_A_X, COL_A_G = 0, 1, 2, 3
COL_Q, COL_K, COL_V, COL_B_G = 4, 5, 6, 7
COL_C_X, COL_C_G = 8, 9
COL_D_U, COL_D_G = 10, 11


def _silu(x):
    return x * jax.nn.sigmoid(x)


def _log1p(x):
    w = 1.0 + x
    return jnp.where(w == 1.0, x, x * jnp.log(w) / (w - 1.0))


def _expm1(x):
    e = jnp.exp(x)
    return jnp.where(e == 1.0, x, (e - 1.0) * x / jnp.log(e))


def _params(*sem):
    return pltpu.CompilerParams(dimension_semantics=sem, vmem_limit_bytes=VMEM_LIMIT)


def _shift_rows(x, d, prev):
    assert 0 < d < SUBLANES and prev.shape[0] == SUBLANES
    rolled = pltpu.roll(x, d, 0)
    rolled_prev = pltpu.roll(prev, d, 0)
    row = lax.broadcasted_iota(jnp.int32, prev.shape, 0)
    top = jnp.where(row < d, rolled_prev, rolled[:SUBLANES])
    return jnp.concatenate([top, rolled[SUBLANES:]], axis=0)


def _shift_rows_fill(x, d, fill):
    n = x.shape[0]
    if d % SUBLANES == 0:
        head = jnp.full((d,) + x.shape[1:], fill, x.dtype)
        return jnp.concatenate([head, x[:n - d]], axis=0)
    rolled = pltpu.roll(x, d, 0)
    row = lax.broadcasted_iota(jnp.int32, x.shape, 0)
    return jnp.where(row < d, jnp.asarray(fill, x.dtype), rolled)


def _ada_kernel(c_ref, w_ref, b_ref, o_ref):
    cond = _silu(c_ref[...])
    o_ref[...] = jnp.sum(cond * w_ref[...], axis=0, keepdims=True) + b_ref[...]


def _ada(c_col, w_ada, b_ada, layer):
    d = c_col.shape[0]
    n = w_ada.shape[2]
    tn = 512
    return pl.pallas_call(
        _ada_kernel,
        grid=(n // tn,),
        in_specs=[
            pl.BlockSpec((d, 1), lambda j: (0, 0)),
            pl.BlockSpec((None, d, tn), lambda j: (layer, 0, j)),
            pl.BlockSpec((None, 1, tn), lambda j: (layer, 0, j)),
        ],
        out_specs=pl.BlockSpec((1, tn), lambda j: (0, j)),
        out_shape=jax.ShapeDtypeStruct((1, n), jnp.float32),
        compiler_params=_params("arbitrary"),
        name="ada",
    )(c_col, w_ada, b_ada)


def _proj_kernel(x_ref, shift_ref, scale_ref, w_ref, o_ref, h_ref):
    @pl.when(pl.program_id(1) == 0)
    def _():
        h = x_ref[...] * (1.0 + scale_ref[...]) + shift_ref[...]
        h_ref[...] = h.astype(jnp.bfloat16)

    o_ref[...] = jnp.dot(h_ref[...], w_ref[...], preferred_element_type=jnp.float32)


def _proj(x, ada, w_in_bf16, layer):
    s, d = x.shape
    n = w_in_bf16.shape[2]
    tm, tn = 1024, 512
    return pl.pallas_call(
        _proj_kernel,
        grid=(s // tm, n // tn),
        in_specs=[
            pl.BlockSpec((tm, d), lambda i, j: (i, 0)),
            pl.BlockSpec((1, d), lambda i, j: (0, 0)),
            pl.BlockSpec((1, d), lambda i, j: (0, 1)),
            pl.BlockSpec((None, d, tn), lambda i, j: (layer, 0, j)),
        ],
        out_specs=pl.BlockSpec((tm, tn), lambda i, j: (i, j)),
        out_shape=jax.ShapeDtypeStruct((s, n), jnp.float32),
        scratch_shapes=[pltpu.VMEM((tm, d), jnp.bfloat16)],
        compiler_params=_params("arbitrary", "arbitrary"),
        name="proj",
    )(x, ada, ada, w_in_bf16)


def _conv_a_kernel(ab_ref, ac_ref, ax_ref, ag_ref, w_ref, y_ref, prev_ref):
    @pl.when(pl.program_id(0) == 0)
    def _():
        prev_ref[...] = jnp.zeros_like(prev_ref)

    u = ac_ref[...] * ax_ref[...]
    prev = prev_ref[...]
    tm = u.shape[0]
    conv = _shift_rows(u, 2, prev) * w_ref[0:1, :]
    conv = conv + _shift_rows(u, 1, prev) * w_ref[1:2, :]
    conv = conv + u * w_ref[2:3, :]
    prev_ref[...] = u[tm - SUBLANES:]
    y_ref[...] = (ab_ref[...] * conv * _silu(ag_ref[...])).astype(y_ref.dtype)


def _conv_a(proj, conv_a, layer):
    s = proj.shape[0]
    tm = 512
    col = lambda c: pl.BlockSpec((tm, BR), lambda i, c=c: (i, c))
    return pl.pallas_call(
        _conv_a_kernel,
        grid=(s // tm,),
        in_specs=[col(COL_A_B), col(COL_A_C), col(COL_A_X), col(COL_A_G),
                  pl.BlockSpec((None,) + conv_a.shape[1:], lambda i: (layer, 0, 0))],
        out_specs=pl.BlockSpec((tm, BR), lambda i: (i, 0)),
        out_shape=jax.ShapeDtypeStruct((s, BR), jnp.bfloat16),
        scratch_shapes=[pltpu.VMEM((SUBLANES, BR), jnp.float32)],
        compiler_params=_params("arbitrary"),
        name="conv_a",
    )(proj, proj, proj, proj, conv_a)


ATT_TILE = max(DILATIONS) * BLK
ATT_BLOCKS = ATT_TILE // BLK
N_PAIRS = ATT_HEADS // 2
MIX_ROWS = 256


def _t5_bucket_tables():
    i = jnp.arange(BLK)[:, None]
    j = jnp.arange(2 * BLK)[None, :]
    max_exact = REL_BUCKETS // 2
    tables = []
    for dil in DILATIONS:
        dist = jnp.clip(i + BLK - j, 0, SPAN) * dil
        nf = jnp.maximum(dist, 1).astype(jnp.float32)
        large = max_exact + (jnp.log(nf / max_exact) / math.log(REL_MAX_DIST / max_exact)
                             * (REL_BUCKETS - max_exact)).astype(jnp.int32)
        large = jnp.minimum(large, REL_BUCKETS - 1)
        tables.append(jnp.where(dist < max_exact, dist, large).astype(jnp.int32))
    return jnp.stack(tables)


def _rows(start, size, stride):
    return pl.ds(start, size) if stride == 1 else pl.ds(start, size, stride=stride)


def _attn_kernel(rb_ref, bucket_ref, q_ref, k_ref, v_ref, g_ref, y_ref,
                 k2_ref, v2_ref, o_ref, lse_ref, bias_ref, mask_ref):
    pair = pl.program_id(0)
    t = pl.program_id(1)
    lane = lax.broadcasted_iota(jnp.int32, (BLK, LANES), 1)
    low = lane < ATT_HEAD_DIM

    @pl.when(t == 0)
    def _():
        k2_ref[0:ATT_TILE, :] = jnp.zeros((ATT_TILE, LANES), jnp.float32)
        v2_ref[0:ATT_TILE, :] = jnp.zeros((ATT_TILE, LANES), jnp.float32)
        for g in range(len(DILATIONS)):
            bucket = bucket_ref[g]
            hits = [bucket == b for b in range(REL_BUCKETS)]
            for half in range(2):
                acc = jnp.zeros(bucket.shape, jnp.float32)
                for b in range(REL_BUCKETS):
                    acc = jnp.where(hits[b], rb_ref[b, 2 * pair + half], acc)
                bias_ref[g, half * BLK:(half + 1) * BLK, :] = acc
        i = lax.broadcasted_iota(jnp.int32, (2 * BLK, 2 * BLK), 0) & (BLK - 1)
        j = lax.broadcasted_iota(jnp.int32, (2 * BLK, 2 * BLK), 1)
        valid = (j >= i) & (j <= i + SPAN)
        mask_ref[0] = jnp.where(valid & (j >= BLK), 1.0, 0.0)
        mask_ref[1] = jnp.where(valid, 1.0, 0.0)

    k2_ref[ATT_TILE:2 * ATT_TILE, :] = k_ref[...]
    v2_ref[ATT_TILE:2 * ATT_TILE, :] = v_ref[...]

    for g, dil in enumerate(DILATIONS):
        per_residue = ATT_BLOCKS // dil

        def block(b, carry, g=g, dil=dil, per_residue=per_residue):
            r = b // per_residue
            n = b % per_residue
            q0 = n * (BLK * dil) + r
            k0 = ATT_TILE - BLK * dil + q0
            q = (q_ref[_rows(q0, BLK, dil), :] * (ATT_HEAD_DIM ** -0.5)).astype(jnp.bfloat16)
            k = k2_ref[_rows(k0, 2 * BLK, dil), :].astype(jnp.bfloat16)
            v = v2_ref[_rows(k0, 2 * BLK, dil), :].astype(jnp.bfloat16)
            zero = jnp.zeros_like(q)
            q2 = jnp.concatenate([jnp.where(low, q, zero), jnp.where(low, zero, q)], axis=0)
            sc = lax.dot_general(q2, k, (((1,), (1,)), ((), ())), preferred_element_type=jnp.float32)
            has_prev = jnp.where((t > 0) | (n > 0), 1, 0)
            sc = jnp.where(mask_ref[has_prev] > 0.5, sc + bias_ref[g], -1e30)
            m = jnp.max(sc, axis=-1, keepdims=True)
            p = jnp.exp(sc - m)
            l = jnp.sum(p, axis=-1, keepdims=True)
            pv = jnp.dot(p.astype(jnp.bfloat16), v, preferred_element_type=jnp.float32)
            o = jnp.where(low, pv[:BLK], pv[BLK:])
            l2 = jnp.where(low, l[:BLK], l[BLK:])
            m2 = jnp.where(low, m[:BLK], m[BLK:])
            o_ref[g, _rows(q0, BLK, dil), :] = o / l2
            lse_ref[g, _rows(q0, BLK, dil), :] = m2 + jnp.log(l2)
            return carry

        lax.fori_loop(0, ATT_BLOCKS, block, 0)

    def mix(c, carry):
        rows = pl.ds(pl.multiple_of(c * MIX_ROWS, MIX_ROWS), MIX_ROWS)
        a = [lse_ref[g, rows, :] for g in range(len(DILATIONS))]
        m = functools.reduce(jnp.maximum, a)
        e = [jnp.exp(x - m) for x in a]
        num = sum(e[g] * o_ref[g, rows, :] for g in range(len(DILATIONS)))
        y_ref[rows, :] = (num / sum(e) * _silu(g_ref[rows, :])).astype(y_ref.dtype)
        return carry

    lax.fori_loop(0, ATT_TILE // MIX_ROWS, mix, 0)
    k2_ref[0:ATT_TILE, :] = k_ref[...]
    v2_ref[0:ATT_TILE, :] = v_ref[...]


def _attn(proj, rel_bias):
    s = proj.shape[0]
    assert s % ATT_TILE == 0
    lanes_per_col = BR // LANES
    col = lambda c: pl.BlockSpec((ATT_TILE, LANES), lambda p, t, c=c: (t, c * lanes_per_col + p))
    n_pat = len(DILATIONS)
    return pl.pallas_call(
        _attn_kernel,
        grid=(N_PAIRS, s // ATT_TILE),
        in_specs=[
            pl.BlockSpec(memory_space=pltpu.SMEM),
            pl.BlockSpec((n_pat, BLK, 2 * BLK), lambda p, t: (0, 0, 0)),
            col(COL_Q), col(COL_K), col(COL_V), col(COL_B_G),
        ],
        out_specs=pl.BlockSpec((ATT_TILE, LANES), lambda p, t: (t, p)),
        out_shape=jax.ShapeDtypeStruct((s, BR), jnp.bfloat16),
        scratch_shapes=[pltpu.VMEM((2 * ATT_TILE, LANES), jnp.float32),
                        pltpu.VMEM((2 * ATT_TILE, LANES), jnp.float32),
                        pltpu.VMEM((n_pat, ATT_TILE, LANES), jnp.float32),
                        pltpu.VMEM((n_pat, ATT_TILE, LANES), jnp.float32),
                        pltpu.VMEM((n_pat, 2 * BLK, 2 * BLK), jnp.float32),
                        pltpu.VMEM((2, 2 * BLK, 2 * BLK), jnp.float32)],
        compiler_params=_params("arbitrary", "arbitrary"),
        name="dilated_attn",
    )(rel_bias, _t5_bucket_tables(), proj, proj, proj, proj)


def _lru_kernel(cx_ref, cg_ref, cw_ref, cb_ref, w_ref, b_ref, lam_ref, y_ref, prev_ref, h_ref):
    @pl.when(pl.program_id(0) == 0)
    def _():
        prev_ref[...] = jnp.zeros_like(prev_ref)
        h_ref[...] = jnp.zeros_like(h_ref)

    cx = cx_ref[...]
    tm = cx.shape[0]
    prev = prev_ref[...]
    xc = _shift_rows(cx, 3, prev) * cw_ref[0:1, :]
    xc = xc + _shift_rows(cx, 2, prev) * cw_ref[1:2, :]
    xc = xc + _shift_rows(cx, 1, prev) * cw_ref[2:3, :]
    xc = xc + cx * cw_ref[3:4, :]
    xc = xc + cb_ref[...]
    prev_ref[...] = cx[tm - SUBLANES:]

    z = jnp.dot(xc.astype(jnp.bfloat16), w_ref[...], preferred_element_type=jnp.float32) + b_ref[...]
    r = jax.nn.sigmoid(z[:, :BR])
    ig = jax.nn.sigmoid(z[:, BR:])
    neg_lam = -lam_ref[...]
    softplus = jnp.maximum(neg_lam, 0.0) + _log1p(jnp.exp(-jnp.abs(neg_lam)))
    log_a = -LRU_C * r * softplus
    a = jnp.exp(log_a)
    b = jnp.sqrt(-_expm1(2.0 * log_a)) * ig * xc

    d = 1
    while d < tm:
        b = b + a * _shift_rows_fill(b, d, 0.0)
        a = a * _shift_rows_fill(a, d, 1.0)
        d *= 2
    h = b + a * h_ref[...]
    h_ref[...] = h[tm - 1:tm]
    y_ref[...] = (h * _silu(cg_ref[...])).astype(y_ref.dtype)


def _block_diag(w):
    hh, n, m = w.shape
    eye = jnp.eye(hh, dtype=w.dtype)
    return jnp.einsum('hij,hg->higj', w, eye).reshape(hh * n, hh * m)


def _lru(proj, conv_c, conv_c_b, w_cat, b_cat, lru_lambda, layer):
    s = proj.shape[0]
    tm = 512
    vec = lambda a: pl.BlockSpec((None,) + a.shape[1:], lambda i, nd=a.ndim: (layer,) + (0,) * (nd - 1))
    conv_c_b = conv_c_b.reshape(DEPTH, 1, BR)
    lam = lru_lambda.reshape(DEPTH, 1, BR)
    return pl.pallas_call(
        _lru_kernel,
        grid=(s // tm,),
        in_specs=[pl.BlockSpec((tm, BR), lambda i: (i, COL_C_X)),
                  pl.BlockSpec((tm, BR), lambda i: (i, COL_C_G)),
                  vec(conv_c), vec(conv_c_b), vec(w_cat), vec(b_cat), vec(lam)],
        out_specs=pl.BlockSpec((tm, BR), lambda i: (i, 0)),
        out_shape=jax.ShapeDtypeStruct((s, BR), jnp.bfloat16),
        scratch_shapes=[pltpu.VMEM((SUBLANES, BR), jnp.float32), pltpu.VMEM((1, BR), jnp.float32)],
        compiler_params=_params("arbitrary"),
        name="rglru",
    )(proj, proj, conv_c, conv_c_b, w_cat, b_cat, lam)


S5_COLS = 256
N_STATE = S5_GROUPS * S5_STATE
N_COLCH = N_STATE // S5_COLS


def _s5_kernel(u_ref, g_ref, bmat_ref, apr_ref, api_ref, cmat_ref, d_ref, wglu_ref, bglu_ref,
               y_ref, xr_ref, xi_ref, sr_ref, si_ref):
    @pl.when(pl.program_id(0) == 0)
    def _():
        sr_ref[...] = jnp.zeros_like(sr_ref)
        si_ref[...] = jnp.zeros_like(si_ref)

    u = u_ref[...]
    tm = u.shape[0]
    nsteps = int(math.log2(tm))
    ub = u.astype(jnp.bfloat16)
    for j in range(N_COLCH):
        lo = j * S5_COLS
        xr_ref[j] = jnp.dot(ub, bmat_ref[:, lo:lo + S5_COLS], preferred_element_type=jnp.float32)
        xi_ref[j] = jnp.dot(ub, bmat_ref[:, N_STATE + lo:N_STATE + lo + S5_COLS],
                            preferred_element_type=jnp.float32)

    def scan_cols(c, carry):
        xr = xr_ref[c]
        xi = xi_ref[c]
        for k in range(nsteps):
            d = 1 << k
            ar = apr_ref[c, d - 1:d, :]
            ai = api_ref[c, d - 1:d, :]
            sr = _shift_rows_fill(xr, d, 0.0)
            si = _shift_rows_fill(xi, d, 0.0)
            xr, xi = xr + (ar * sr - ai * si), xi + (ar * si + ai * sr)
        ar = apr_ref[c]
        ai = api_ref[c]
        cr = sr_ref[c]
        ci = si_ref[c]
        xr = xr + (ar * cr - ai * ci)
        xi = xi + (ar * ci + ai * cr)
        xr_ref[c] = xr
        xi_ref[c] = xi
        sr_ref[c] = xr[tm - 1:tm]
        si_ref[c] = xi[tm - 1:tm]
        return carry

    lax.fori_loop(0, N_COLCH, scan_cols, 0)

    y = d_ref[...] * u
    for j in range(N_COLCH):
        lo = j * S5_COLS
        y = y + jnp.dot(xr_ref[j].astype(jnp.bfloat16), cmat_ref[lo:lo + S5_COLS, :],
                        preferred_element_type=jnp.float32)
        y = y + jnp.dot(xi_ref[j].astype(jnp.bfloat16), cmat_ref[N_STATE + lo:N_STATE + lo + S5_COLS, :],
                        preferred_element_type=jnp.float32)
    y = 0.5 * y * (1.0 + jnp.tanh(math.sqrt(2.0 / math.pi) * (y + 0.044715 * (y * y * y))))
    gate = jnp.dot(y.astype(jnp.bfloat16), wglu_ref[...], preferred_element_type=jnp.float32) + bglu_ref[...]
    y = y * jax.nn.sigmoid(gate)
    y_ref[...] = (y * _silu(g_ref[...])).astype(y_ref.dtype)


def _s5_discretize(lam_re, lam_im, log_dt, b_re, b_im, c_re, c_im, tm):
    dt = jnp.exp(log_dt)[:, None]
    mag = jnp.exp(lam_re * dt)
    ab_re = mag * jnp.cos(lam_im * dt)
    ab_im = mag * jnp.sin(lam_im * dt)
    den = lam_re * lam_re + lam_im * lam_im
    f_re = ((ab_re - 1.0) * lam_re + ab_im * lam_im) / den
    f_im = (ab_im * lam_re - (ab_re - 1.0) * lam_im) / den
    bb_re = f_re[..., None] * b_re - f_im[..., None] * b_im
    bb_im = f_re[..., None] * b_im + f_im[..., None] * b_re
    bmat = jnp.concatenate([_block_diag(jnp.swapaxes(bb_re, 1, 2)),
                            _block_diag(jnp.swapaxes(bb_im, 1, 2))], axis=1)
    cmat = jnp.concatenate([_block_diag(jnp.swapaxes(c_re, 1, 2)),
                            -_block_diag(jnp.swapaxes(c_im, 1, 2))], axis=0)
    pr, pi = ab_re.reshape(1, N_STATE), ab_im.reshape(1, N_STATE)
    while pr.shape[0] < tm:
        lr, li = pr[-1:], pi[-1:]
        pr, pi = (jnp.concatenate([pr, pr * lr - pi * li], axis=0),
                  jnp.concatenate([pi, pr * li + pi * lr], axis=0))
    by_cols = lambda a: a.reshape(tm, N_COLCH, S5_COLS).transpose(1, 0, 2)
    return bmat.astype(jnp.bfloat16), cmat.astype(jnp.bfloat16), by_cols(pr), by_cols(pi)


def _s5(proj, bmat, apr, api, cmat, s5_d, w_glu_bf16, b_glu, layer, tm):
    s = proj.shape[0]
    const = lambda a: pl.BlockSpec(a.shape, lambda i, nd=a.ndim: (0,) * nd)
    vec = lambda a: pl.BlockSpec((None,) + a.shape[1:], lambda i, nd=a.ndim: (layer,) + (0,) * (nd - 1))
    s5_d = s5_d.reshape(DEPTH, 1, BR)
    b_glu = b_glu.reshape(DEPTH, 1, BR)
    return pl.pallas_call(
        _s5_kernel,
        grid=(s // tm,),
        in_specs=[pl.BlockSpec((tm, BR), lambda i: (i, COL_D_U)),
                  pl.BlockSpec((tm, BR), lambda i: (i, COL_D_G)),
                  const(bmat), const(apr), const(api), const(cmat),
                  vec(s5_d), vec(w_glu_bf16), vec(b_glu)],
        out_specs=pl.BlockSpec((tm, BR), lambda i: (i, 0)),
        out_shape=jax.ShapeDtypeStruct((s, BR), jnp.bfloat16),
        scratch_shapes=[pltpu.VMEM((N_COLCH, tm, S5_COLS), jnp.float32),
                        pltpu.VMEM((N_COLCH, tm, S5_COLS), jnp.float32),
                        pltpu.VMEM((N_COLCH, 1, S5_COLS), jnp.float32),
                        pltpu.VMEM((N_COLCH, 1, S5_COLS), jnp.float32)],
        compiler_params=_params("arbitrary"),
        name="s5",
    )(proj, proj, bmat, apr, api, cmat, s5_d, w_glu_bf16, b_glu)


def _out_kernel(x_ref, ya_ref, yb_ref, yc_ref, yd_ref, w_ref, gate_ref, g_ref, b_ref, o_ref):
    y = jnp.dot(ya_ref[...], w_ref[0 * BR:1 * BR, :], preferred_element_type=jnp.float32)
    y = y + jnp.dot(yb_ref[...], w_ref[1 * BR:2 * BR, :], preferred_element_type=jnp.float32)
    y = y + jnp.dot(yc_ref[...], w_ref[2 * BR:3 * BR, :], preferred_element_type=jnp.float32)
    y = y + jnp.dot(yd_ref[...], w_ref[3 * BR:4 * BR, :], preferred_element_type=jnp.float32)
    z = ALPHA * x_ref[...] + (1.0 + gate_ref[...]) * y
    mu = jnp.mean(z, axis=-1, keepdims=True)
    zc = z - mu
    var = jnp.mean(zc * zc, axis=-1, keepdims=True)
    o_ref[...] = zc * lax.rsqrt(var + LN_EPS) * g_ref[...] + b_ref[...]


def _out(x, ya, yb, yc, yd, w_out_bf16, ada, ln_g, ln_b, layer):
    s, d = x.shape
    tm = 256
    branch = pl.BlockSpec((tm, BR), lambda i: (i, 0))
    vec = pl.BlockSpec((None, 1, d), lambda i: (layer, 0, 0))
    return pl.pallas_call(
        _out_kernel,
        grid=(s // tm,),
        in_specs=[pl.BlockSpec((tm, d), lambda i: (i, 0)), branch, branch, branch, branch,
                  pl.BlockSpec((None, 4 * BR, d), lambda i: (layer, 0, 0)),
                  pl.BlockSpec((1, d), lambda i: (0, 2)),
                  vec, vec],
        out_specs=pl.BlockSpec((tm, d), lambda i: (i, 0)),
        out_shape=jax.ShapeDtypeStruct((s, d), jnp.float32),
        compiler_params=_params("arbitrary"),
        name="out_proj_ln",
    )(x, ya, yb, yc, yd, w_out_bf16, ada, ln_g.reshape(DEPTH, 1, d), ln_b.reshape(DEPTH, 1, d))


def kernel(x, c, rel_bias, w_ada, b_ada, w_in, conv_a, conv_c, conv_c_b, lru_wa, lru_ba, lru_wx, lru_bx, lru_lambda, s5_lam_re, s5_lam_im, s5_log_dt, s5_b_re, s5_b_im, s5_c_re, s5_c_im, s5_d, s5_w_glu, s5_b_glu, w_out, ln_g, ln_b):
    bsz, s, d = x.shape
    assert bsz == 1 and w_in.shape == (DEPTH, d, N_IN)
    s5_tm = 256
    xs = x.reshape(s, d)
    c_col = c.reshape(d, 1)
    b_ada3 = b_ada.reshape(DEPTH, 1, 3 * d)
    w_in_bf16 = w_in.astype(jnp.bfloat16)
    w_out_bf16 = w_out.astype(jnp.bfloat16)
    w_glu_bf16 = s5_w_glu.astype(jnp.bfloat16)
    lru_w = jnp.stack([jnp.concatenate([_block_diag(lru_wa[l]), _block_diag(lru_wx[l])], axis=1)
                       for l in range(DEPTH)]).astype(jnp.bfloat16)
    lru_b = jnp.concatenate([lru_ba, lru_bx], axis=1).reshape(DEPTH, 1, 2 * BR)

    for l in range(DEPTH):
        ada = _ada(c_col, w_ada, b_ada3, l)
        proj = _proj(xs, ada, w_in_bf16, l)
        ya = _conv_a(proj, conv_a, l)
        yb = _attn(proj, rel_bias)
        yc = _lru(proj, conv_c, conv_c_b, lru_w, lru_b, lru_lambda, l)
        bmat, cmat, apr, api = _s5_discretize(
            s5_lam_re[l], s5_lam_im[l], s5_log_dt[l], s5_b_re[l], s5_b_im[l], s5_c_re[l], s5_c_im[l], s5_tm)
        yd = _s5(proj, bmat, apr, api, cmat, s5_d, w_glu_bf16, s5_b_glu, l, s5_tm)
        xs = _out(xs, ya, yb, yc, yd, w_out_bf16, ada, ln_g, ln_b, l)
    return xs.reshape(bsz, s, d)
```

```python
import functools
import math

import numpy as np
import jax
import jax.numpy as jnp
from jax import lax
from jax.experimental import pallas as pl
from jax.experimental.pallas import tpu as pltpu

BR = 512
N_IN = 12 * BR
ATT_HEADS = 8
ATT_HEAD_DIM = 64
BLK = 128
SPAN = 128
DILATIONS = (1, 4, 16)
REL_BUCKETS = 32
REL_MAX_DIST = 2048
LRU_HEADS = 8
LRU_C = 8.0
S5_CH = 16
S5_GROUPS = 32
S5_STATE = 64
DEPTH = 2
ALPHA = (2 * DEPTH) ** 0.25
LN_EPS = 1e-5

SUBLANES = 8
LANES = 128
VMEM_LIMIT = 48 * 1024 * 1024

COL_A_B, COL_A_C, COL_A_X, COL_A_G = 0, 1, 2, 3
COL_Q, COL_K, COL_V, COL_B_G = 4, 5, 6, 7
COL_C_X, COL_C_G = 8, 9
COL_D_U, COL_D_G = 10, 11


def _silu(x):
    return x * jax.nn.sigmoid(x)


def _log1p(x):
    w = 1.0 + x
    return jnp.where(w == 1.0, x, x * jnp.log(w) / (w - 1.0))


def _expm1(x):
    e = jnp.exp(x)
    return jnp.where(e == 1.0, x, (e - 1.0) * x / jnp.log(e))


def _params(*sem):
    return pltpu.CompilerParams(dimension_semantics=sem, vmem_limit_bytes=VMEM_LIMIT)


def _shift_rows(x, d, prev):
    assert 0 < d < SUBLANES and prev.shape[0] == SUBLANES
    rolled = pltpu.roll(x, d, 0)
    rolled_prev = pltpu.roll(prev, d, 0)
    row = lax.broadcasted_iota(jnp.int32, prev.shape, 0)
    top = jnp.where(row < d, rolled_prev, rolled[:SUBLANES])
    return jnp.concatenate([top, rolled[SUBLANES:]], axis=0)


def _shift_rows_fill(x, d, fill):
    n = x.shape[0]
    if d % SUBLANES == 0:
        head = jnp.full((d,) + x.shape[1:], fill, x.dtype)
        return jnp.concatenate([head, x[:n - d]], axis=0)
    rolled = pltpu.roll(x, d, 0)
    row = lax.broadcasted_iota(jnp.int32, x.shape, 0)
    return jnp.where(row < d, jnp.asarray(fill, x.dtype), rolled)


def _ada_kernel(c_ref, w_ref, b_ref, o_ref):
    cond = _silu(c_ref[...])
    o_ref[...] = jnp.sum(cond * w_ref[...], axis=0, keepdims=True) + b_ref[...]


def _ada(c_col, w_ada, b_ada, layer):
    d = c_col.shape[0]
    n = w_ada.shape[2]
    tn = 512
    return pl.pallas_call(
        _ada_kernel,
        grid=(n // tn,),
        in_specs=[
            pl.BlockSpec((d, 1), lambda j: (0, 0)),
            pl.BlockSpec((None, d, tn), lambda j: (layer, 0, j)),
            pl.BlockSpec((None, 1, tn), lambda j: (layer, 0, j)),
        ],
        out_specs=pl.BlockSpec((1, tn), lambda j: (0, j)),
        out_shape=jax.ShapeDtypeStruct((1, n), jnp.float32),
        compiler_params=_params("arbitrary"),
        name="ada",
    )(c_col, w_ada, b_ada)


def _proj_kernel(x_ref, shift_ref, scale_ref, w_ref, o_ref, h_ref):
    @pl.when(pl.program_id(1) == 0)
    def _():
        h = x_ref[...] * (1.0 + scale_ref[...]) + shift_ref[...]
        h_ref[...] = h.astype(jnp.bfloat16)

    o_ref[...] = jnp.dot(h_ref[...], w_ref[...], preferred_element_type=jnp.float32)


def _proj(x, ada, w_in_bf16, layer):
    s, d = x.shape
    n = w_in_bf16.shape[2]
    tm, tn = 1024, 512
    return pl.pallas_call(
        _proj_kernel,
        grid=(s // tm, n // tn),
        in_specs=[
            pl.BlockSpec((tm, d), lambda i, j: (i, 0)),
            pl.BlockSpec((1, d), lambda i, j: (0, 0)),
            pl.BlockSpec((1, d), lambda i, j: (0, 1)),
            pl.BlockSpec((None, d, tn), lambda i, j: (layer, 0, j)),
        ],
        out_specs=pl.BlockSpec((tm, tn), lambda i, j: (i, j)),
        out_shape=jax.ShapeDtypeStruct((s, n), jnp.float32),
        scratch_shapes=[pltpu.VMEM((tm, d), jnp.bfloat16)],
        compiler_params=_params("arbitrary", "arbitrary"),
        name="proj",
    )(x, ada, ada, w_in_bf16)


def _conv_a_kernel(ab_ref, ac_ref, ax_ref, ag_ref, w_ref, y_ref, prev_ref):
    @pl.when(pl.program_id(0) == 0)
    def _():
        prev_ref[...] = jnp.zeros_like(prev_ref)

    u = ac_ref[...] * ax_ref[...]
    prev = prev_ref[...]
    tm = u.shape[0]
    conv = _shift_rows(u, 2, prev) * w_ref[0:1, :]
    conv = conv + _shift_rows(u, 1, prev) * w_ref[1:2, :]
    conv = conv + u * w_ref[2:3, :]
    prev_ref[...] = u[tm - SUBLANES:]
    y_ref[...] = (ab_ref[...] * conv * _silu(ag_ref[...])).astype(y_ref.dtype)


def _conv_a(proj, conv_a, layer):
    s = proj.shape[0]
    tm = 512
    col = lambda c: pl.BlockSpec((tm, BR), lambda i, c=c: (i, c))
    return pl.pallas_call(
        _conv_a_kernel,
        grid=(s // tm,),
        in_specs=[col(COL_A_B), col(COL_A_C), col(COL_A_X), col(COL_A_G),
                  pl.BlockSpec((None,) + conv_a.shape[1:], lambda i: (layer, 0, 0))],
        out_specs=pl.BlockSpec((tm, BR), lambda i: (i, 0)),
        out_shape=jax.ShapeDtypeStruct((s, BR), jnp.bfloat16),
        scratch_shapes=[pltpu.VMEM((SUBLANES, BR), jnp.float32)],
        compiler_params=_params("arbitrary"),
        name="conv_a",
    )(proj, proj, proj, proj, conv_a)


ATT_TILE = max(DILATIONS) * BLK
ATT_BLOCKS = ATT_TILE // BLK
N_PAIRS = ATT_HEADS // 2
DEINT = 4
PLANE = ATT_TILE // DEINT
QROWS = BLK // DEINT
MIX_ROWS = 256
ATT_UNROLL = 4
MASKED = -1e30


def _t5_bucket_tables():
    assert DILATIONS == (1, 4, 16) and DEINT == 4
    i = np.arange(BLK)[:, None]
    j = np.arange(2 * BLK)[None, :]
    delta = i + BLK - j
    valid = (delta >= 0) & (delta <= SPAN)
    max_exact = REL_BUCKETS // 2
    tables = []
    for dil in DILATIONS:
        dist = np.clip(delta, 0, SPAN) * dil
        nf = np.maximum(dist, 1).astype(np.float32)
        large = max_exact + (np.log(nf / np.float32(max_exact)) / np.float32(math.log(REL_MAX_DIST / max_exact))
                             * np.float32(REL_BUCKETS - max_exact)).astype(np.int32)
        bucket = np.where(dist < max_exact, dist, np.minimum(large, REL_BUCKETS - 1))
        table = np.stack([np.where(valid & (j >= BLK), bucket, -1), np.where(valid, bucket, -1)])
        if dil == 1:
            rows = np.array([DEINT * a + r for r in range(DEINT) for a in range(QROWS)])
            cols = np.array([blk * BLK + DEINT * a + r
                             for r in range(DEINT) for blk in range(2) for a in range(QROWS)])
            table = table[:, rows][:, :, cols]
        tables.append(table)
    return jnp.asarray(np.stack(tables), jnp.int32)


def _attend(q, k, v, bias, low):
    q = (q * (ATT_HEAD_DIM ** -0.5)).astype(jnp.bfloat16)
    zero = jnp.zeros_like(q)
    q2 = jnp.concatenate([jnp.where(low, q, zero), jnp.where(low, zero, q)], axis=0)
    sc = lax.dot_general(q2, k.astype(jnp.bfloat16), (((1,), (1,)), ((), ())),
                         preferred_element_type=jnp.float32) + bias
    m = jnp.max(sc, axis=-1, keepdims=True)
    p = jnp.exp(sc - m)
    l = jnp.sum(p, axis=-1, keepdims=True)
    pv = jnp.dot(p.astype(jnp.bfloat16), v.astype(jnp.bfloat16), preferred_element_type=jnp.float32)
    o = jnp.where(low, pv[:BLK], pv[BLK:])
    l2 = jnp.where(low, l[:BLK], l[BLK:])
    m2 = jnp.where(low, m[:BLK], m[BLK:])
    return o / l2, m2 + jnp.log(l2)


def _attn_kernel(rb_ref, bucket_ref, q_ref, k_ref, v_ref, g_ref, y_ref,
                 q4_ref, k4_ref, v4_ref, o4_ref, lse4_ref, ynat_ref, bias_ref):
    pair = pl.program_id(0)
    t = pl.program_id(1)
    lane = lax.broadcasted_iota(jnp.int32, (BLK, LANES), 1)
    low = lane < ATT_HEAD_DIM
    n_pat = len(DILATIONS)

    @pl.when(t == 0)
    def _():
        k4_ref[:, 0:PLANE, :] = jnp.zeros((DEINT, PLANE, LANES), jnp.float32)
        v4_ref[:, 0:PLANE, :] = jnp.zeros((DEINT, PLANE, LANES), jnp.float32)
        for g in range(n_pat):
            bucket = bucket_ref[g, 1]
            no_prev = bucket_ref[g, 0] < 0
            hits = [bucket == b for b in range(REL_BUCKETS)]
            for half in range(2):
                acc = jnp.full(bucket.shape, MASKED, jnp.float32)
                for b in range(REL_BUCKETS):
                    acc = jnp.where(hits[b], rb_ref[b, 2 * pair + half], acc)
                bias_ref[g, 1, half * BLK:(half + 1) * BLK, :] = acc
                bias_ref[g, 0, half * BLK:(half + 1) * BLK, :] = jnp.where(no_prev, MASKED, acc)

    for r in range(DEINT):
        q4_ref[r] = q_ref[pl.ds(r, PLANE, stride=DEINT), :]
        k4_ref[r, PLANE:2 * PLANE, :] = k_ref[pl.ds(r, PLANE, stride=DEINT), :]
        v4_ref[r, PLANE:2 * PLANE, :] = v_ref[pl.ds(r, PLANE, stride=DEINT), :]

    def unit_d1(n, carry):
        a0 = pl.multiple_of(n * QROWS, QROWS)
        q = jnp.concatenate([q4_ref[r, pl.ds(a0, QROWS), :] for r in range(DEINT)], axis=0)
        k = jnp.concatenate([k4_ref[r, pl.ds(PLANE - QROWS + a0, 2 * QROWS), :] for r in range(DEINT)], axis=0)
        v = jnp.concatenate([v4_ref[r, pl.ds(PLANE - QROWS + a0, 2 * QROWS), :] for r in range(DEINT)], axis=0)
        hp = jnp.where((t > 0) | (n > 0), 1, 0)
        o, lse = _attend(q, k, v, bias_ref[0, hp], low)
        for r in range(DEINT):
            o4_ref[0, r, pl.ds(a0, QROWS), :] = o[r * QROWS:(r + 1) * QROWS]
            lse4_ref[0, r, pl.ds(a0, QROWS), :] = lse[r * QROWS:(r + 1) * QROWS]
        return carry

    def unit_d4(b, carry):
        r = b // DEINT
        n = b % DEINT
        a0 = pl.multiple_of(n * BLK, BLK)
        q = q4_ref[r, pl.ds(a0, BLK), :]
        k = k4_ref[r, pl.ds(PLANE - BLK + a0, 2 * BLK), :]
        v = v4_ref[r, pl.ds(PLANE - BLK + a0, 2 * BLK), :]
        hp = jnp.where((t > 0) | (n > 0), 1, 0)
        o, lse = _attend(q, k, v, bias_ref[1, hp], low)
        o4_ref[1, r, pl.ds(a0, BLK), :] = o
        lse4_ref[1, r, pl.ds(a0, BLK), :] = lse
        return carry

    def unit_d16(b, carry):
        lo = b // DEINT
        hi = b % DEINT
        q = q4_ref[lo, pl.ds(hi, BLK, stride=DEINT), :]
        k = k4_ref[lo, pl.ds(hi, 2 * BLK, stride=DEINT), :]
        v = v4_ref[lo, pl.ds(hi, 2 * BLK, stride=DEINT), :]
        hp = jnp.where(t > 0, 1, 0)
        o, lse = _attend(q, k, v, bias_ref[2, hp], low)
        o4_ref[2, lo, pl.ds(hi, BLK, stride=DEINT), :] = o
        lse4_ref[2, lo, pl.ds(hi, BLK, stride=DEINT), :] = lse
        return carry

    for unit in (unit_d1, unit_d4, unit_d16):
        lax.fori_loop(0, ATT_BLOCKS, unit, 0, unroll=ATT_UNROLL)

    for r in range(DEINT):
        def mix(c, carry, r=r):
            a0 = pl.multiple_of(c * MIX_ROWS, MIX_ROWS)
            nat = pl.ds(DEINT * a0 + r, MIX_ROWS, stride=DEINT)
            a = [lse4_ref[g, r, pl.ds(a0, MIX_ROWS), :] for g in range(n_pat)]
            m = functools.reduce(jnp.maximum, a)
            e = [jnp.exp(x - m) for x in a]
            num = sum(e[g] * o4_ref[g, r, pl.ds(a0, MIX_ROWS), :] for g in range(n_pat))
            ynat_ref[nat, :] = num / sum(e) * _silu(g_ref[nat, :])
            return carry

        lax.fori_loop(0, PLANE // MIX_ROWS, mix, 0)

    y_ref[...] = ynat_ref[...].astype(y_ref.dtype)
    k4_ref[:, 0:PLANE, :] = k4_ref[:, PLANE:2 * PLANE, :]
    v4_ref[:, 0:PLANE, :] = v4_ref[:, PLANE:2 * PLANE, :]


def _attn(proj, rel_bias):
    s = proj.shape[0]
    assert s % ATT_TILE == 0
    lanes_per_col = BR // LANES
    col = lambda c: pl.BlockSpec((ATT_TILE, LANES), lambda p, t, c=c: (t, c * lanes_per_col + p))
    n_pat = len(DILATIONS)
    return pl.pallas_call(
        _attn_kernel,
        grid=(N_PAIRS, s // ATT_TILE),
        in_specs=[
            pl.BlockSpec(memory_space=pltpu.SMEM),
            pl.BlockSpec((n_pat, 2, BLK, 2 * BLK), lambda p, t: (0, 0, 0, 0)),
            col(COL_Q), col(COL_K), col(COL_V), col(COL_B_G),
        ],
        out_specs=pl.BlockSpec((ATT_TILE, LANES), lambda p, t: (t, p)),
        out_shape=jax.ShapeDtypeStruct((s, BR), jnp.bfloat16),
        scratch_shapes=[pltpu.VMEM((DEINT, PLANE, LANES), jnp.float32),
                        pltpu.VMEM((DEINT, 2 * PLANE, LANES), jnp.float32),
                        pltpu.VMEM((DEINT, 2 * PLANE, LANES), jnp.float32),
                        pltpu.VMEM((n_pat, DEINT, PLANE, LANES), jnp.float32),
                        pltpu.VMEM((n_pat, DEINT, PLANE, LANES), jnp.float32),
                        pltpu.VMEM((ATT_TILE, LANES), jnp.float32),
                        pltpu.VMEM((n_pat, 2, 2 * BLK, 2 * BLK), jnp.float32)],
        compiler_params=_params("arbitrary", "arbitrary"),
        name="dilated_attn",
    )(rel_bias, _t5_bucket_tables(), proj, proj, proj, proj)


def _lru_kernel(cx_ref, cg_ref, cw_ref, cb_ref, w_ref, b_ref, lam_ref, y_ref, prev_ref, h_ref):
    @pl.when(pl.program_id(0) == 0)
    def _():
        prev_ref[...] = jnp.zeros_like(prev_ref)
        h_ref[...] = jnp.zeros_like(h_ref)

    cx = cx_ref[...]
    tm = cx.shape[0]
    prev = prev_ref[...]
    xc = _shift_rows(cx, 3, prev) * cw_ref[0:1, :]
    xc = xc + _shift_rows(cx, 2, prev) * cw_ref[1:2, :]
    xc = xc + _shift_rows(cx, 1, prev) * cw_ref[2:3, :]
    xc = xc + cx * cw_ref[3:4, :]
    xc = xc + cb_ref[...]
    prev_ref[...] = cx[tm - SUBLANES:]

    z = jnp.dot(xc.astype(jnp.bfloat16), w_ref[...], preferred_element_type=jnp.float32) + b_ref[...]
    r = jax.nn.sigmoid(z[:, :BR])
    ig = jax.nn.sigmoid(z[:, BR:])
    neg_lam = -lam_ref[...]
    softplus = jnp.maximum(neg_lam, 0.0) + _log1p(jnp.exp(-jnp.abs(neg_lam)))
    log_a = -LRU_C * r * softplus
    a = jnp.exp(log_a)
    b = jnp.sqrt(-_expm1(2.0 * log_a)) * ig * xc

    d = 1
    while d < tm:
        b = b + a * _shift_rows_fill(b, d, 0.0)
        a = a * _shift_rows_fill(a, d, 1.0)
        d *= 2
    h = b + a * h_ref[...]
    h_ref[...] = h[tm - 1:tm]
    y_ref[...] = (h * _silu(cg_ref[...])).astype(y_ref.dtype)


def _block_diag(w):
    hh, n, m = w.shape
    eye = jnp.eye(hh, dtype=w.dtype)
    return jnp.einsum('hij,hg->higj', w, eye).reshape(hh * n, hh * m)


def _lru(proj, conv_c, conv_c_b, w_cat, b_cat, lru_lambda, layer):
    s = proj.shape[0]
    tm = 512
    vec = lambda a: pl.BlockSpec((None,) + a.shape[1:], lambda i, nd=a.ndim: (layer,) + (0,) * (nd - 1))
    conv_c_b = conv_c_b.reshape(DEPTH, 1, BR)
    lam = lru_lambda.reshape(DEPTH, 1, BR)
    return pl.pallas_call(
        _lru_kernel,
        grid=(s // tm,),
        in_specs=[pl.BlockSpec((tm, BR), lambda i: (i, COL_C_X)),
                  pl.BlockSpec((tm, BR), lambda i: (i, COL_C_G)),
                  vec(conv_c), vec(conv_c_b), vec(w_cat), vec(b_cat), vec(lam)],
        out_specs=pl.BlockSpec((tm, BR), lambda i: (i, 0)),
        out_shape=jax.ShapeDtypeStruct((s, BR), jnp.bfloat16),
        scratch_shapes=[pltpu.VMEM((SUBLANES, BR), jnp.float32), pltpu.VMEM((1, BR), jnp.float32)],
        compiler_params=_params("arbitrary"),
        name="rglru",
    )(proj, proj, conv_c, conv_c_b, w_cat, b_cat, lam)


N_STATE = S5_GROUPS * S5_STATE
S5_TM = 512
S5_CHUNKS = BR // LANES
S5_CHUNK_STATES = N_STATE // S5_CHUNKS
SLABS_PER_CHUNK = S5_CHUNK_STATES // LANES
N_SLABS = N_STATE // LANES
SLAB_PITCH = S5_TM + SUBLANES
S5_UNROLL = 8


def _s5_kernel(u_ref, g_ref, bmat_ref, ar_ref, ai_ref, cmat_ref, d_ref, wglu_ref, bglu_ref,
               y_ref, xs_ref, state_ref):
    @pl.when(pl.program_id(0) == 0)
    def _():
        state_ref[...] = jnp.zeros_like(state_ref)

    u = u_ref[...]
    tm = u.shape[0]
    ub = u.astype(jnp.bfloat16)
    for j in range(S5_CHUNKS):
        bu = jnp.dot(ub[:, j * LANES:(j + 1) * LANES], bmat_ref[j], preferred_element_type=jnp.float32)
        for part in range(2):
            for cc in range(SLABS_PER_CHUNK):
                slab = part * N_SLABS + j * SLABS_PER_CHUNK + cc
                col = part * S5_CHUNK_STATES + cc * LANES
                xs_ref[pl.ds(slab * SLAB_PITCH, tm), :] = bu[:, col:col + LANES]

    n_vreg = N_SLABS // SUBLANES
    a_re = [ar_ref[v * SUBLANES:(v + 1) * SUBLANES, :] for v in range(n_vreg)]
    a_im = [ai_ref[v * SUBLANES:(v + 1) * SUBLANES, :] for v in range(n_vreg)]

    def rows(part, v, t):
        return pl.ds((part * N_SLABS + v * SUBLANES) * SLAB_PITCH + t, SUBLANES, stride=SLAB_PITCH)

    def step(t, x):
        new = []
        for v in range(n_vreg):
            xr, xi = x[2 * v], x[2 * v + 1]
            nr = (a_re[v] * xr + xs_ref[rows(0, v, t), :]) - a_im[v] * xi
            ni = (a_re[v] * xi + xs_ref[rows(1, v, t), :]) + a_im[v] * xr
            xs_ref[rows(0, v, t), :] = nr
            xs_ref[rows(1, v, t), :] = ni
            new += [nr, ni]
        return tuple(new)

    x0 = tuple(state_ref[i] for i in range(2 * n_vreg))
    x1 = lax.fori_loop(0, tm, step, x0, unroll=S5_UNROLL)
    for i in range(2 * n_vreg):
        state_ref[i] = x1[i]

    ys = []
    for j in range(S5_CHUNKS):
        xcat = jnp.concatenate(
            [xs_ref[pl.ds((part * N_SLABS + j * SLABS_PER_CHUNK + cc) * SLAB_PITCH, tm), :]
             for part in range(2) for cc in range(SLABS_PER_CHUNK)], axis=1)
        ys.append(jnp.dot(xcat.astype(jnp.bfloat16), cmat_ref[j], preferred_element_type=jnp.float32))
    y = jnp.concatenate(ys, axis=1) + d_ref[...] * u
    y = 0.5 * y * (1.0 + jnp.tanh(math.sqrt(2.0 / math.pi) * (y + 0.044715 * (y * y * y))))
    gate = jnp.dot(y.astype(jnp.bfloat16), wglu_ref[...], preferred_element_type=jnp.float32) + bglu_ref[...]
    y = y * jax.nn.sigmoid(gate)
    y_ref[...] = (y * _silu(g_ref[...])).astype(y_ref.dtype)


def _chunk_block_diag(w):
    per = w.shape[0] // S5_CHUNKS
    return jnp.stack([_block_diag(w[j * per:(j + 1) * per]) for j in range(S5_CHUNKS)])


def _s5_discretize(lam_re, lam_im, log_dt, b_re, b_im, c_re, c_im):
    dt = jnp.exp(log_dt)[:, None]
    mag = jnp.exp(lam_re * dt)
    ab_re = mag * jnp.cos(lam_im * dt)
    ab_im = mag * jnp.sin(lam_im * dt)
    den = lam_re * lam_re + lam_im * lam_im
    f_re = ((ab_re - 1.0) * lam_re + ab_im * lam_im) / den
    f_im = (ab_im * lam_re - (ab_re - 1.0) * lam_im) / den
    bb_re = f_re[..., None] * b_re - f_im[..., None] * b_im
    bb_im = f_re[..., None] * b_im + f_im[..., None] * b_re
    bmat = jnp.concatenate([_chunk_block_diag(jnp.swapaxes(bb_re, 1, 2)),
                            _chunk_block_diag(jnp.swapaxes(bb_im, 1, 2))], axis=2)
    cmat = jnp.concatenate([_chunk_block_diag(jnp.swapaxes(c_re, 1, 2)),
                            -_chunk_block_diag(jnp.swapaxes(c_im, 1, 2))], axis=1)
    return (bmat.astype(jnp.bfloat16), cmat.astype(jnp.bfloat16),
            ab_re.reshape(N_SLABS, LANES), ab_im.reshape(N_SLABS, LANES))


def _s5(proj, bmat, a_re, a_im, cmat, s5_d, w_glu_bf16, b_glu, layer):
    s = proj.shape[0]
    tm = S5_TM
    const = lambda a: pl.BlockSpec(a.shape, lambda i, nd=a.ndim: (0,) * nd)
    vec = lambda a: pl.BlockSpec((None,) + a.shape[1:], lambda i, nd=a.ndim: (layer,) + (0,) * (nd - 1))
    s5_d = s5_d.reshape(DEPTH, 1, BR)
    b_glu = b_glu.reshape(DEPTH, 1, BR)
    return pl.pallas_call(
        _s5_kernel,
        grid=(s // tm,),
        in_specs=[pl.BlockSpec((tm, BR), lambda i: (i, COL_D_U)),
                  pl.BlockSpec((tm, BR), lambda i: (i, COL_D_G)),
                  const(bmat), const(a_re), const(a_im), const(cmat),
                  vec(s5_d), vec(w_glu_bf16), vec(b_glu)],
        out_specs=pl.BlockSpec((tm, BR), lambda i: (i, 0)),
        out_shape=jax.ShapeDtypeStruct((s, BR), jnp.bfloat16),
        scratch_shapes=[pltpu.VMEM((2 * N_SLABS * SLAB_PITCH, LANES), jnp.float32),
                        pltpu.VMEM((2 * N_SLABS // SUBLANES, SUBLANES, LANES), jnp.float32)],
        compiler_params=_params("arbitrary"),
        name="s5",
    )(proj, proj, bmat, a_re, a_im, cmat, s5_d, w_glu_bf16, b_glu)


def _out_kernel(x_ref, ya_ref, yb_ref, yc_ref, yd_ref, w_ref, gate_ref, g_ref, b_ref, o_ref):
    y = jnp.dot(ya_ref[...], w_ref[0 * BR:1 * BR, :], preferred_element_type=jnp.float32)
    y = y + jnp.dot(yb_ref[...], w_ref[1 * BR:2 * BR, :], preferred_element_type=jnp.float32)
    y = y + jnp.dot(yc_ref[...], w_ref[2 * BR:3 * BR, :], preferred_element_type=jnp.float32)
    y = y + jnp.dot(yd_ref[...], w_ref[3 * BR:4 * BR, :], preferred_element_type=jnp.float32)
    z = ALPHA * x_ref[...] + (1.0 + gate_ref[...]) * y
    mu = jnp.mean(z, axis=-1, keepdims=True)
    zc = z - mu
    var = jnp.mean(zc * zc, axis=-1, keepdims=True)
    o_ref[...] = zc * lax.rsqrt(var + LN_EPS) * g_ref[...] + b_ref[...]


def _out(x, ya, yb, yc, yd, w_out_bf16, ada, ln_g, ln_b, layer):
    s, d = x.shape
    tm = 256
    branch = pl.BlockSpec((tm, BR), lambda i: (i, 0))
    vec = pl.BlockSpec((None, 1, d), lambda i: (layer, 0, 0))
    return pl.pallas_call(
        _out_kernel,
        grid=(s // tm,),
        in_specs=[pl.BlockSpec((tm, d), lambda i: (i, 0)), branch, branch, branch, branch,
                  pl.BlockSpec((None, 4 * BR, d), lambda i: (layer, 0, 0)),
                  pl.BlockSpec((1, d), lambda i: (0, 2)),
                  vec, vec],
        out_specs=pl.BlockSpec((tm, d), lambda i: (i, 0)),
        out_shape=jax.ShapeDtypeStruct((s, d), jnp.float32),
        compiler_params=_params("arbitrary"),
        name="out_proj_ln",
    )(x, ya, yb, yc, yd, w_out_bf16, ada, ln_g.reshape(DEPTH, 1, d), ln_b.reshape(DEPTH, 1, d))


def kernel(x, c, rel_bias, w_ada, b_ada, w_in, conv_a, conv_c, conv_c_b, lru_wa, lru_ba, lru_wx, lru_bx, lru_lambda, s5_lam_re, s5_lam_im, s5_log_dt, s5_b_re, s5_b_im, s5_c_re, s5_c_im, s5_d, s5_w_glu, s5_b_glu, w_out, ln_g, ln_b):
    bsz, s, d = x.shape
    assert bsz == 1 and w_in.shape == (DEPTH, d, N_IN)
    xs = x.reshape(s, d)
    c_col = c.reshape(d, 1)
    b_ada3 = b_ada.reshape(DEPTH, 1, 3 * d)
    w_in_bf16 = w_in.astype(jnp.bfloat16)
    w_out_bf16 = w_out.astype(jnp.bfloat16)
    w_glu_bf16 = s5_w_glu.astype(jnp.bfloat16)
    lru_w = jnp.stack([jnp.concatenate([_block_diag(lru_wa[l]), _block_diag(lru_wx[l])], axis=1)
                       for l in range(DEPTH)]).astype(jnp.bfloat16)
    lru_b = jnp.concatenate([lru_ba, lru_bx], axis=1).reshape(DEPTH, 1, 2 * BR)

    for l in range(DEPTH):
        ada = _ada(c_col, w_ada, b_ada3, l)
        proj = _proj(xs, ada, w_in_bf16, l)
        ya = _conv_a(proj, conv_a, l)
        yb = _attn(proj, rel_bias)
        yc = _lru(proj, conv_c, conv_c_b, lru_w, lru_b, lru_lambda, l)
        bmat, cmat, a_re, a_im = _s5_discretize(
            s5_lam_re[l], s5_lam_im[l], s5_log_dt[l], s5_b_re[l], s5_b_im[l], s5_c_re[l], s5_c_im[l])
        yd = _s5(proj, bmat, a_re, a_im, cmat, s5_d, w_glu_bf16, s5_b_glu, l)
        xs = _out(xs, ya, yb, yc, yd, w_out_bf16, ada, ln_g, ln_b, l)
    return xs.reshape(bsz, s, d)
```

```python
import functools
import math

import numpy as np
import jax
import jax.numpy as jnp
from jax import lax
from jax.experimental import pallas as pl
from jax.experimental.pallas import tpu as pltpu

BR = 512
N_IN = 12 * BR
ATT_HEADS = 8
ATT_HEAD_DIM = 64
BLK = 128
SPAN = 128
DILATIONS = (1, 4, 16)
REL_BUCKETS = 32
REL_MAX_DIST = 2048
LRU_HEADS = 8
LRU_C = 8.0
S5_CH = 16
S5_GROUPS = 32
S5_STATE = 64
DEPTH = 2
ALPHA = (2 * DEPTH) ** 0.25
LN_EPS = 1e-5

SUBLANES = 8
LANES = 128
VMEM_LIMIT = 48 * 1024 * 1024

COL_A_B, COL_A_C, COL_A_X, COL_A_G = 0, 1, 2, 3
COL_Q, COL_K, COL_V, COL_B_G = 4, 5, 6, 7
COL_C_X, COL_C_G = 8, 9
COL_D_U, COL_D_G = 10, 11


def _silu(x):
    return x * jax.nn.sigmoid(x)


def _log1p(x):
    w = 1.0 + x
    return jnp.where(w == 1.0, x, x * jnp.log(w) / (w - 1.0))


def _expm1(x):
    e = jnp.exp(x)
    return jnp.where(e == 1.0, x, (e - 1.0) * x / jnp.log(e))


def _params(*sem):
    return pltpu.CompilerParams(dimension_semantics=sem, vmem_limit_bytes=VMEM_LIMIT)


def _shift_rows(x, d, prev):
    assert 0 < d < SUBLANES and prev.shape[0] == SUBLANES
    rolled = pltpu.roll(x, d, 0)
    rolled_prev = pltpu.roll(prev, d, 0)
    row = lax.broadcasted_iota(jnp.int32, prev.shape, 0)
    top = jnp.where(row < d, rolled_prev, rolled[:SUBLANES])
    return jnp.concatenate([top, rolled[SUBLANES:]], axis=0)


def _shift_rows_fill(x, d, fill):
    n = x.shape[0]
    if d % SUBLANES == 0:
        head = jnp.full((d,) + x.shape[1:], fill, x.dtype)
        return jnp.concatenate([head, x[:n - d]], axis=0)
    rolled = pltpu.roll(x, d, 0)
    row = lax.broadcasted_iota(jnp.int32, x.shape, 0)
    return jnp.where(row < d, jnp.asarray(fill, x.dtype), rolled)


def _ada_kernel(c_ref, w_ref, b_ref, o_ref):
    cond = _silu(c_ref[...])
    o_ref[...] = jnp.sum(cond * w_ref[...], axis=0, keepdims=True) + b_ref[...]


def _ada(c_col, w_ada, b_ada, layer):
    d = c_col.shape[0]
    n = w_ada.shape[2]
    tn = 512
    return pl.pallas_call(
        _ada_kernel,
        grid=(n // tn,),
        in_specs=[
            pl.BlockSpec((d, 1), lambda j: (0, 0)),
            pl.BlockSpec((None, d, tn), lambda j: (layer, 0, j)),
            pl.BlockSpec((None, 1, tn), lambda j: (layer, 0, j)),
        ],
        out_specs=pl.BlockSpec((1, tn), lambda j: (0, j)),
        out_shape=jax.ShapeDtypeStruct((1, n), jnp.float32),
        compiler_params=_params("arbitrary"),
        name="ada",
    )(c_col, w_ada, b_ada)


def _proj_kernel(x_ref, shift_ref, scale_ref, w_ref, o_ref, h_ref):
    @pl.when(pl.program_id(1) == 0)
    def _():
        h = x_ref[...] * (1.0 + scale_ref[...]) + shift_ref[...]
        h_ref[...] = h.astype(jnp.bfloat16)

    o_ref[...] = jnp.dot(h_ref[...], w_ref[...], preferred_element_type=jnp.float32)


def _proj(x, ada, w_in_bf16, layer):
    s, d = x.shape
    n = w_in_bf16.shape[2]
    tm, tn = 1024, 1024
    return pl.pallas_call(
        _proj_kernel,
        grid=(s // tm, n // tn),
        in_specs=[
            pl.BlockSpec((tm, d), lambda i, j: (i, 0)),
            pl.BlockSpec((1, d), lambda i, j: (0, 0)),
            pl.BlockSpec((1, d), lambda i, j: (0, 1)),
            pl.BlockSpec((None, d, tn), lambda i, j: (layer, 0, j)),
        ],
        out_specs=pl.BlockSpec((tm, tn), lambda i, j: (i, j)),
        out_shape=jax.ShapeDtypeStruct((s, n), jnp.float32),
        scratch_shapes=[pltpu.VMEM((tm, d), jnp.bfloat16)],
        compiler_params=_params("arbitrary", "arbitrary"),
        name="proj",
    )(x, ada, ada, w_in_bf16)


def _conv_a_kernel(ab_ref, ac_ref, ax_ref, ag_ref, w_ref, y_ref, prev_ref):
    @pl.when(pl.program_id(0) == 0)
    def _():
        prev_ref[...] = jnp.zeros_like(prev_ref)

    u = ac_ref[...] * ax_ref[...]
    prev = prev_ref[...]
    tm = u.shape[0]
    conv = _shift_rows(u, 2, prev) * w_ref[0:1, :]
    conv = conv + _shift_rows(u, 1, prev) * w_ref[1:2, :]
    conv = conv + u * w_ref[2:3, :]
    prev_ref[...] = u[tm - SUBLANES:]
    y_ref[...] = (ab_ref[...] * conv * _silu(ag_ref[...])).astype(y_ref.dtype)


def _conv_a(proj, conv_a, layer):
    s = proj.shape[0]
    tm = 512
    col = lambda c: pl.BlockSpec((tm, BR), lambda i, c=c: (i, c))
    return pl.pallas_call(
        _conv_a_kernel,
        grid=(s // tm,),
        in_specs=[col(COL_A_B), col(COL_A_C), col(COL_A_X), col(COL_A_G),
                  pl.BlockSpec((None,) + conv_a.shape[1:], lambda i: (layer, 0, 0))],
        out_specs=pl.BlockSpec((tm, BR), lambda i: (i, 0)),
        out_shape=jax.ShapeDtypeStruct((s, BR), jnp.bfloat16),
        scratch_shapes=[pltpu.VMEM((SUBLANES, BR), jnp.float32)],
        compiler_params=_params("arbitrary"),
        name="conv_a",
    )(proj, proj, proj, proj, conv_a)


ATT_TILE = max(DILATIONS) * BLK
ATT_BLOCKS = ATT_TILE // BLK
N_PAIRS = ATT_HEADS // 2
DEINT = 4
PLANE = ATT_TILE // DEINT
QROWS = BLK // DEINT
MIX_ROWS = 256
MASKED = -1e30


def _t5_bucket_tables():
    assert DILATIONS == (1, 4, 16) and DEINT == 4
    i = np.arange(BLK)[:, None]
    j = np.arange(2 * BLK)[None, :]
    delta = i + BLK - j
    valid = (delta >= 0) & (delta <= SPAN)
    max_exact = REL_BUCKETS // 2
    tables = []
    for dil in DILATIONS:
        dist = np.clip(delta, 0, SPAN) * dil
        nf = np.maximum(dist, 1).astype(np.float32)
        large = max_exact + (np.log(nf / np.float32(max_exact)) / np.float32(math.log(REL_MAX_DIST / max_exact))
                             * np.float32(REL_BUCKETS - max_exact)).astype(np.int32)
        bucket = np.where(dist < max_exact, dist, np.minimum(large, REL_BUCKETS - 1))
        table = np.stack([np.where(valid & (j >= BLK), bucket, -1), np.where(valid, bucket, -1)])
        if dil == 1:
            rows = np.array([DEINT * a + r for r in range(DEINT) for a in range(QROWS)])
            cols = np.array([blk * BLK + DEINT * a + r
                             for r in range(DEINT) for blk in range(2) for a in range(QROWS)])
            table = table[:, rows][:, :, cols]
        tables.append(table)
    return jnp.asarray(np.stack(tables), jnp.int32)


def _stage_scores(q, k, bias, low, sc_ref):
    q = (q * (ATT_HEAD_DIM ** -0.5)).astype(jnp.bfloat16)
    zero = jnp.zeros_like(q)
    q2 = jnp.concatenate([jnp.where(low, q, zero), jnp.where(low, zero, q)], axis=0)
    sc_ref[...] = lax.dot_general(q2, k.astype(jnp.bfloat16), (((1,), (1,)), ((), ())),
                                  preferred_element_type=jnp.float32) + bias


def _stage_softmax(sc_ref, p_ref, m_ref, l_ref):
    sc = sc_ref[...]
    m = jnp.max(sc, axis=-1, keepdims=True)
    p = jnp.exp(sc - m)
    l = jnp.sum(p, axis=-1, keepdims=True)
    p_ref[...] = p.astype(p_ref.dtype)
    m_ref[...] = jnp.broadcast_to(m, m_ref.shape)
    l_ref[...] = jnp.broadcast_to(l, l_ref.shape)


def _stage_output(p_ref, m_ref, l_ref, v, low):
    pv = jnp.dot(p_ref[...], v.astype(jnp.bfloat16), preferred_element_type=jnp.float32)
    o = jnp.where(low, pv[:BLK], pv[BLK:])
    l2 = jnp.where(low, l_ref[:BLK, :], l_ref[BLK:, :])
    m2 = jnp.where(low, m_ref[:BLK, :], m_ref[BLK:, :])
    return o / l2, m2 + jnp.log(l2)


def _attn_kernel(rb_ref, bucket_ref, q_ref, k_ref, v_ref, g_ref, y_ref,
                 q4_ref, k4_ref, v4_ref, o4_ref, lse4_ref, ynat_ref, bias_ref,
                 sc0_ref, sc1_ref, p0_ref, p1_ref, m0_ref, m1_ref, l0_ref, l1_ref):
    pair = pl.program_id(0)
    t = pl.program_id(1)
    lane = lax.broadcasted_iota(jnp.int32, (BLK, LANES), 1)
    low = lane < ATT_HEAD_DIM
    n_pat = len(DILATIONS)

    @pl.when(t == 0)
    def _():
        k4_ref[:, 0:PLANE, :] = jnp.zeros((DEINT, PLANE, LANES), jnp.float32)
        v4_ref[:, 0:PLANE, :] = jnp.zeros((DEINT, PLANE, LANES), jnp.float32)
        for g in range(n_pat):
            bucket = bucket_ref[g, 1]
            no_prev = bucket_ref[g, 0] < 0
            hits = [bucket == b for b in range(REL_BUCKETS)]
            for half in range(2):
                acc = jnp.full(bucket.shape, MASKED, jnp.float32)
                for b in range(REL_BUCKETS):
                    acc = jnp.where(hits[b], rb_ref[b, 2 * pair + half], acc)
                bias_ref[g, 1, half * BLK:(half + 1) * BLK, :] = acc
                bias_ref[g, 0, half * BLK:(half + 1) * BLK, :] = jnp.where(no_prev, MASKED, acc)

    for r in range(DEINT):
        q4_ref[r] = q_ref[pl.ds(r, PLANE, stride=DEINT), :]
        k4_ref[r, PLANE:2 * PLANE, :] = k_ref[pl.ds(r, PLANE, stride=DEINT), :]
        v4_ref[r, PLANE:2 * PLANE, :] = v_ref[pl.ds(r, PLANE, stride=DEINT), :]

    def d1_qk(n):
        a0 = pl.multiple_of(n * QROWS, QROWS)
        q = jnp.concatenate([q4_ref[r, pl.ds(a0, QROWS), :] for r in range(DEINT)], axis=0)
        k = jnp.concatenate([k4_ref[r, pl.ds(PLANE - QROWS + a0, 2 * QROWS), :] for r in range(DEINT)], axis=0)
        return q, k, (t > 0) | (n > 0)

    def d1_v(n):
        a0 = pl.multiple_of(n * QROWS, QROWS)
        return jnp.concatenate([v4_ref[r, pl.ds(PLANE - QROWS + a0, 2 * QROWS), :] for r in range(DEINT)], axis=0)

    def d1_store(n, o, lse):
        a0 = pl.multiple_of(n * QROWS, QROWS)
        for r in range(DEINT):
            o4_ref[0, r, pl.ds(a0, QROWS), :] = o[r * QROWS:(r + 1) * QROWS]
            lse4_ref[0, r, pl.ds(a0, QROWS), :] = lse[r * QROWS:(r + 1) * QROWS]

    def d4_qk(b):
        r, n = b // DEINT, b % DEINT
        a0 = pl.multiple_of(n * BLK, BLK)
        return (q4_ref[r, pl.ds(a0, BLK), :], k4_ref[r, pl.ds(PLANE - BLK + a0, 2 * BLK), :],
                (t > 0) | (n > 0))

    def d4_v(b):
        r, n = b // DEINT, b % DEINT
        return v4_ref[r, pl.ds(PLANE - BLK + pl.multiple_of(n * BLK, BLK), 2 * BLK), :]

    def d4_store(b, o, lse):
        r, n = b // DEINT, b % DEINT
        a0 = pl.multiple_of(n * BLK, BLK)
        o4_ref[1, r, pl.ds(a0, BLK), :] = o
        lse4_ref[1, r, pl.ds(a0, BLK), :] = lse

    def d16_qk(b):
        lo, hi = b // DEINT, b % DEINT
        return (q4_ref[lo, pl.ds(hi, BLK, stride=DEINT), :],
                k4_ref[lo, pl.ds(hi, 2 * BLK, stride=DEINT), :], t > 0)

    def d16_v(b):
        lo, hi = b // DEINT, b % DEINT
        return v4_ref[lo, pl.ds(hi, 2 * BLK, stride=DEINT), :]

    def d16_store(b, o, lse):
        lo, hi = b // DEINT, b % DEINT
        o4_ref[2, lo, pl.ds(hi, BLK, stride=DEINT), :] = o
        lse4_ref[2, lo, pl.ds(hi, BLK, stride=DEINT), :] = lse

    sc, p, m, l = (sc0_ref, sc1_ref), (p0_ref, p1_ref), (m0_ref, m1_ref), (l0_ref, l1_ref)

    for g, (qk, vload, store) in enumerate(((d1_qk, d1_v, d1_store), (d4_qk, d4_v, d4_store),
                                            (d16_qk, d16_v, d16_store))):
        def scores(u, slot, g=g, qk=qk):
            q, k, has_prev = qk(u)
            _stage_scores(q, k, bias_ref[g, jnp.where(has_prev, 1, 0)], low, sc[slot])

        def softmax(slot):
            _stage_softmax(sc[slot], p[slot], m[slot], l[slot])

        def output(u, slot, vload=vload, store=store):
            o, lse = _stage_output(p[slot], m[slot], l[slot], vload(u), low)
            store(u, o, lse)

        def body(j, carry, scores=scores, softmax=softmax, output=output):
            u = 2 * j
            scores(u, 0)
            softmax(1)
            output(u - 2, 0)
            scores(u + 1, 1)
            softmax(0)
            output(u - 1, 1)
            return carry

        scores(jnp.int32(0), 0)
        scores(jnp.int32(1), 1)
        softmax(0)
        lax.fori_loop(1, ATT_BLOCKS // 2, body, 0)
        softmax(1)
        output(jnp.int32(ATT_BLOCKS - 2), 0)
        output(jnp.int32(ATT_BLOCKS - 1), 1)

    for r in range(DEINT):
        def mix(c, carry, r=r):
            a0 = pl.multiple_of(c * MIX_ROWS, MIX_ROWS)
            nat = pl.ds(DEINT * a0 + r, MIX_ROWS, stride=DEINT)
            a = [lse4_ref[g, r, pl.ds(a0, MIX_ROWS), :] for g in range(n_pat)]
            m = functools.reduce(jnp.maximum, a)
            e = [jnp.exp(x - m) for x in a]
            num = sum(e[g] * o4_ref[g, r, pl.ds(a0, MIX_ROWS), :] for g in range(n_pat))
            ynat_ref[nat, :] = num / sum(e) * _silu(g_ref[nat, :])
            return carry

        lax.fori_loop(0, PLANE // MIX_ROWS, mix, 0)

    y_ref[...] = ynat_ref[...].astype(y_ref.dtype)
    k4_ref[:, 0:PLANE, :] = k4_ref[:, PLANE:2 * PLANE, :]
    v4_ref[:, 0:PLANE, :] = v4_ref[:, PLANE:2 * PLANE, :]


def _attn(proj, rel_bias):
    s = proj.shape[0]
    assert s % ATT_TILE == 0
    lanes_per_col = BR // LANES
    col = lambda c: pl.BlockSpec((ATT_TILE, LANES), lambda p, t, c=c: (t, c * lanes_per_col + p))
    n_pat = len(DILATIONS)
    return pl.pallas_call(
        _attn_kernel,
        grid=(N_PAIRS, s // ATT_TILE),
        in_specs=[
            pl.BlockSpec(memory_space=pltpu.SMEM),
            pl.BlockSpec((n_pat, 2, BLK, 2 * BLK), lambda p, t: (0, 0, 0, 0)),
            col(COL_Q), col(COL_K), col(COL_V), col(COL_B_G),
        ],
        out_specs=pl.BlockSpec((ATT_TILE, LANES), lambda p, t: (t, p)),
        out_shape=jax.ShapeDtypeStruct((s, BR), jnp.bfloat16),
        scratch_shapes=[pltpu.VMEM((DEINT, PLANE, LANES), jnp.float32),
                        pltpu.VMEM((DEINT, 2 * PLANE, LANES), jnp.float32),
                        pltpu.VMEM((DEINT, 2 * PLANE, LANES), jnp.float32),
                        pltpu.VMEM((n_pat, DEINT, PLANE, LANES), jnp.float32),
                        pltpu.VMEM((n_pat, DEINT, PLANE, LANES), jnp.float32),
                        pltpu.VMEM((ATT_TILE, LANES), jnp.float32),
                        pltpu.VMEM((n_pat, 2, 2 * BLK, 2 * BLK), jnp.float32)]
                       + [pltpu.VMEM((2 * BLK, 2 * BLK), jnp.float32)] * 2
                       + [pltpu.VMEM((2 * BLK, 2 * BLK), jnp.bfloat16)] * 2
                       + [pltpu.VMEM((2 * BLK, LANES), jnp.float32)] * 4,
        compiler_params=_params("arbitrary", "arbitrary"),
        name="dilated_attn",
    )(rel_bias, _t5_bucket_tables(), proj, proj, proj, proj)


def _lru_kernel(cx_ref, cg_ref, cw_ref, cb_ref, w_ref, b_ref, lam_ref, y_ref, prev_ref, h_ref):
    @pl.when(pl.program_id(0) == 0)
    def _():
        prev_ref[...] = jnp.zeros_like(prev_ref)
        h_ref[...] = jnp.zeros_like(h_ref)

    cx = cx_ref[...]
    tm = cx.shape[0]
    prev = prev_ref[...]
    xc = _shift_rows(cx, 3, prev) * cw_ref[0:1, :]
    xc = xc + _shift_rows(cx, 2, prev) * cw_ref[1:2, :]
    xc = xc + _shift_rows(cx, 1, prev) * cw_ref[2:3, :]
    xc = xc + cx * cw_ref[3:4, :]
    xc = xc + cb_ref[...]
    prev_ref[...] = cx[tm - SUBLANES:]

    z = jnp.dot(xc.astype(jnp.bfloat16), w_ref[...], preferred_element_type=jnp.float32) + b_ref[...]
    r = jax.nn.sigmoid(z[:, :BR])
    ig = jax.nn.sigmoid(z[:, BR:])
    neg_lam = -lam_ref[...]
    softplus = jnp.maximum(neg_lam, 0.0) + _log1p(jnp.exp(-jnp.abs(neg_lam)))
    log_a = -LRU_C * r * softplus
    a = jnp.exp(log_a)
    b = jnp.sqrt(-_expm1(2.0 * log_a)) * ig * xc

    d = 1
    while d < tm:
        b = b + a * _shift_rows_fill(b, d, 0.0)
        a = a * _shift_rows_fill(a, d, 1.0)
        d *= 2
    h = b + a * h_ref[...]
    h_ref[...] = h[tm - 1:tm]
    y_ref[...] = (h * _silu(cg_ref[...])).astype(y_ref.dtype)


def _block_diag(w):
    hh, n, m = w.shape
    eye = jnp.eye(hh, dtype=w.dtype)
    return jnp.einsum('hij,hg->higj', w, eye).reshape(hh * n, hh * m)


def _lru(proj, conv_c, conv_c_b, w_cat, b_cat, lru_lambda, layer):
    s = proj.shape[0]
    tm = 512
    vec = lambda a: pl.BlockSpec((None,) + a.shape[1:], lambda i, nd=a.ndim: (layer,) + (0,) * (nd - 1))
    conv_c_b = conv_c_b.reshape(DEPTH, 1, BR)
    lam = lru_lambda.reshape(DEPTH, 1, BR)
    return pl.pallas_call(
        _lru_kernel,
        grid=(s // tm,),
        in_specs=[pl.BlockSpec((tm, BR), lambda i: (i, COL_C_X)),
                  pl.BlockSpec((tm, BR), lambda i: (i, COL_C_G)),
                  vec(conv_c), vec(conv_c_b), vec(w_cat), vec(b_cat), vec(lam)],
        out_specs=pl.BlockSpec((tm, BR), lambda i: (i, 0)),
        out_shape=jax.ShapeDtypeStruct((s, BR), jnp.bfloat16),
        scratch_shapes=[pltpu.VMEM((SUBLANES, BR), jnp.float32), pltpu.VMEM((1, BR), jnp.float32)],
        compiler_params=_params("arbitrary"),
        name="rglru",
    )(proj, proj, conv_c, conv_c_b, w_cat, b_cat, lam)


N_STATE = S5_GROUPS * S5_STATE
S5_TM = 512
S5_CHUNKS = BR // LANES
S5_CHUNK_STATES = N_STATE // S5_CHUNKS
SLABS_PER_CHUNK = S5_CHUNK_STATES // LANES
N_SLABS = N_STATE // LANES
SLAB_PITCH = S5_TM + SUBLANES
S5_UNROLL = 8


def _s5_kernel(u_ref, g_ref, bmat_ref, ar_ref, ai_ref, cmat_ref, d_ref, wglu_ref, bglu_ref,
               y_ref, xs_ref, state_ref):
    @pl.when(pl.program_id(0) == 0)
    def _():
        state_ref[...] = jnp.zeros_like(state_ref)

    u = u_ref[...]
    tm = u.shape[0]
    ub = u.astype(jnp.bfloat16)
    for j in range(S5_CHUNKS):
        bu = jnp.dot(ub[:, j * LANES:(j + 1) * LANES], bmat_ref[j], preferred_element_type=jnp.float32)
        for part in range(2):
            for cc in range(SLABS_PER_CHUNK):
                slab = part * N_SLABS + j * SLABS_PER_CHUNK + cc
                col = part * S5_CHUNK_STATES + cc * LANES
                xs_ref[pl.ds(slab * SLAB_PITCH, tm), :] = bu[:, col:col + LANES]

    n_vreg = N_SLABS // SUBLANES
    a_re = [ar_ref[v * SUBLANES:(v + 1) * SUBLANES, :] for v in range(n_vreg)]
    a_im = [ai_ref[v * SUBLANES:(v + 1) * SUBLANES, :] for v in range(n_vreg)]

    def rows(part, v, t):
        return pl.ds((part * N_SLABS + v * SUBLANES) * SLAB_PITCH + t, SUBLANES, stride=SLAB_PITCH)

    def step(t, x):
        new = []
        for v in range(n_vreg):
            xr, xi = x[2 * v], x[2 * v + 1]
            nr = (a_re[v] * xr + xs_ref[rows(0, v, t), :]) - a_im[v] * xi
            ni = (a_re[v] * xi + xs_ref[rows(1, v, t), :]) + a_im[v] * xr
            xs_ref[rows(0, v, t), :] = nr
            xs_ref[rows(1, v, t), :] = ni
            new += [nr, ni]
        return tuple(new)

    x0 = tuple(state_ref[i] for i in range(2 * n_vreg))
    x1 = lax.fori_loop(0, tm, step, x0, unroll=S5_UNROLL)
    for i in range(2 * n_vreg):
        state_ref[i] = x1[i]

    ys = []
    for j in range(S5_CHUNKS):
        xcat = jnp.concatenate(
            [xs_ref[pl.ds((part * N_SLABS + j * SLABS_PER_CHUNK + cc) * SLAB_PITCH, tm), :]
             for part in range(2) for cc in range(SLABS_PER_CHUNK)], axis=1)
        ys.append(jnp.dot(xcat.astype(jnp.bfloat16), cmat_ref[j], preferred_element_type=jnp.float32))
    y = jnp.concatenate(ys, axis=1) + d_ref[...] * u
    y = 0.5 * y * (1.0 + jnp.tanh(math.sqrt(2.0 / math.pi) * (y + 0.044715 * (y * y * y))))
    gate = jnp.dot(y.astype(jnp.bfloat16), wglu_ref[...], preferred_element_type=jnp.float32) + bglu_ref[...]
    y = y * jax.nn.sigmoid(gate)
    y_ref[...] = (y * _silu(g_ref[...])).astype(y_ref.dtype)


def _chunk_block_diag(w):
    per = w.shape[0] // S5_CHUNKS
    return jnp.stack([_block_diag(w[j * per:(j + 1) * per]) for j in range(S5_CHUNKS)])


def _s5_discretize(lam_re, lam_im, log_dt, b_re, b_im, c_re, c_im):
    dt = jnp.exp(log_dt)[:, None]
    mag = jnp.exp(lam_re * dt)
    ab_re = mag * jnp.cos(lam_im * dt)
    ab_im = mag * jnp.sin(lam_im * dt)
    den = lam_re * lam_re + lam_im * lam_im
    f_re = ((ab_re - 1.0) * lam_re + ab_im * lam_im) / den
    f_im = (ab_im * lam_re - (ab_re - 1.0) * lam_im) / den
    bb_re = f_re[..., None] * b_re - f_im[..., None] * b_im
    bb_im = f_re[..., None] * b_im + f_im[..., None] * b_re
    bmat = jnp.concatenate([_chunk_block_diag(jnp.swapaxes(bb_re, 1, 2)),
                            _chunk_block_diag(jnp.swapaxes(bb_im, 1, 2))], axis=2)
    cmat = jnp.concatenate([_chunk_block_diag(jnp.swapaxes(c_re, 1, 2)),
                            -_chunk_block_diag(jnp.swapaxes(c_im, 1, 2))], axis=1)
    return (bmat.astype(jnp.bfloat16), cmat.astype(jnp.bfloat16),
            ab_re.reshape(N_SLABS, LANES), ab_im.reshape(N_SLABS, LANES))


def _s5(proj, bmat, a_re, a_im, cmat, s5_d, w_glu_bf16, b_glu, layer):
    s = proj.shape[0]
    tm = S5_TM
    const = lambda a: pl.BlockSpec(a.shape, lambda i, nd=a.ndim: (0,) * nd)
    vec = lambda a: pl.BlockSpec((None,) + a.shape[1:], lambda i, nd=a.ndim: (layer,) + (0,) * (nd - 1))
    s5_d = s5_d.reshape(DEPTH, 1, BR)
    b_glu = b_glu.reshape(DEPTH, 1, BR)
    return pl.pallas_call(
        _s5_kernel,
        grid=(s // tm,),
        in_specs=[pl.BlockSpec((tm, BR), lambda i: (i, COL_D_U)),
                  pl.BlockSpec((tm, BR), lambda i: (i, COL_D_G)),
                  const(bmat), const(a_re), const(a_im), const(cmat),
                  vec(s5_d), vec(w_glu_bf16), vec(b_glu)],
        out_specs=pl.BlockSpec((tm, BR), lambda i: (i, 0)),
        out_shape=jax.ShapeDtypeStruct((s, BR), jnp.bfloat16),
        scratch_shapes=[pltpu.VMEM((2 * N_SLABS * SLAB_PITCH, LANES), jnp.float32),
                        pltpu.VMEM((2 * N_SLABS // SUBLANES, SUBLANES, LANES), jnp.float32)],
        compiler_params=_params("arbitrary"),
        name="s5",
    )(proj, proj, bmat, a_re, a_im, cmat, s5_d, w_glu_bf16, b_glu)


def _out_kernel(x_ref, ya_ref, yb_ref, yc_ref, yd_ref, w_ref, gate_ref, g_ref, b_ref, o_ref):
    y = jnp.dot(ya_ref[...], w_ref[0 * BR:1 * BR, :], preferred_element_type=jnp.float32)
    y = y + jnp.dot(yb_ref[...], w_ref[1 * BR:2 * BR, :], preferred_element_type=jnp.float32)
    y = y + jnp.dot(yc_ref[...], w_ref[2 * BR:3 * BR, :], preferred_element_type=jnp.float32)
    y = y + jnp.dot(yd_ref[...], w_ref[3 * BR:4 * BR, :], preferred_element_type=jnp.float32)
    z = ALPHA * x_ref[...] + (1.0 + gate_ref[...]) * y
    mu = jnp.mean(z, axis=-1, keepdims=True)
    zc = z - mu
    var = jnp.mean(zc * zc, axis=-1, keepdims=True)
    o_ref[...] = zc * lax.rsqrt(var + LN_EPS) * g_ref[...] + b_ref[...]


def _out(x, ya, yb, yc, yd, w_out_bf16, ada, ln_g, ln_b, layer):
    s, d = x.shape
    tm = 512
    branch = pl.BlockSpec((tm, BR), lambda i: (i, 0))
    vec = pl.BlockSpec((None, 1, d), lambda i: (layer, 0, 0))
    return pl.pallas_call(
        _out_kernel,
        grid=(s // tm,),
        in_specs=[pl.BlockSpec((tm, d), lambda i: (i, 0)), branch, branch, branch, branch,
                  pl.BlockSpec((None, 4 * BR, d), lambda i: (layer, 0, 0)),
                  pl.BlockSpec((1, d), lambda i: (0, 2)),
                  vec, vec],
        out_specs=pl.BlockSpec((tm, d), lambda i: (i, 0)),
        out_shape=jax.ShapeDtypeStruct((s, d), jnp.float32),
        compiler_params=_params("arbitrary"),
        name="out_proj_ln",
    )(x, ya, yb, yc, yd, w_out_bf16, ada, ln_g.reshape(DEPTH, 1, d), ln_b.reshape(DEPTH, 1, d))


def kernel(x, c, rel_bias, w_ada, b_ada, w_in, conv_a, conv_c, conv_c_b, lru_wa, lru_ba, lru_wx, lru_bx, lru_lambda, s5_lam_re, s5_lam_im, s5_log_dt, s5_b_re, s5_b_im, s5_c_re, s5_c_im, s5_d, s5_w_glu, s5_b_glu, w_out, ln_g, ln_b):
    bsz, s, d = x.shape
    assert bsz == 1 and w_in.shape == (DEPTH, d, N_IN)
    xs = x.reshape(s, d)
    c_col = c.reshape(d, 1)
    b_ada3 = b_ada.reshape(DEPTH, 1, 3 * d)
    w_in_bf16 = w_in.astype(jnp.bfloat16)
    w_out_bf16 = w_out.astype(jnp.bfloat16)
    w_glu_bf16 = s5_w_glu.astype(jnp.bfloat16)
    lru_w = jnp.stack([jnp.concatenate([_block_diag(lru_wa[l]), _block_diag(lru_wx[l])], axis=1)
                       for l in range(DEPTH)]).astype(jnp.bfloat16)
    lru_b = jnp.concatenate([lru_ba, lru_bx], axis=1).reshape(DEPTH, 1, 2 * BR)

    for l in range(DEPTH):
        ada = _ada(c_col, w_ada, b_ada3, l)
        proj = _proj(xs, ada, w_in_bf16, l)
        ya = _conv_a(proj, conv_a, l)
        yb = _attn(proj, rel_bias)
        yc = _lru(proj, conv_c, conv_c_b, lru_w, lru_b, lru_lambda, l)
        bmat, cmat, a_re, a_im = _s5_discretize(
            s5_lam_re[l], s5_lam_im[l], s5_log_dt[l], s5_b_re[l], s5_b_im[l], s5_c_re[l], s5_c_im[l])
        yd = _s5(proj, bmat, a_re, a_im, cmat, s5_d, w_glu_bf16, s5_b_glu, l)
        xs = _out(xs, ya, yb, yc, yd, w_out_bf16, ada, ln_g, ln_b, l)
    return xs.reshape(bsz, s, d)
```

```python
import functools
import math

import numpy as np
import jax
import jax.numpy as jnp
from jax import lax
from jax.experimental import pallas as pl
from jax.experimental.pallas import tpu as pltpu

BR = 512
N_IN = 12 * BR
ATT_HEADS = 8
ATT_HEAD_DIM = 64
BLK = 128
SPAN = 128
DILATIONS = (1, 4, 16)
REL_BUCKETS = 32
REL_MAX_DIST = 2048
LRU_HEADS = 8
LRU_C = 8.0
S5_CH = 16
S5_GROUPS = 32
S5_STATE = 64
DEPTH = 2
ALPHA = (2 * DEPTH) ** 0.25
LN_EPS = 1e-5

SUBLANES = 8
LANES = 128
VMEM_LIMIT = 48 * 1024 * 1024

COL_A_B, COL_A_C, COL_A_X, COL_A_G = 0, 1, 2, 3
COL_Q, COL_K, COL_V, COL_B_G = 4, 5, 6, 7
COL_C_X, COL_C_G = 8, 9
COL_D_U, COL_D_G = 10, 11


def _silu(x):
    return x * jax.nn.sigmoid(x)


def _log1p(x):
    w = 1.0 + x
    return jnp.where(w == 1.0, x, x * jnp.log(w) / (w - 1.0))


def _expm1(x):
    e = jnp.exp(x)
    return jnp.where(e == 1.0, x, (e - 1.0) * x / jnp.log(e))


def _params(*sem):
    return pltpu.CompilerParams(dimension_semantics=sem, vmem_limit_bytes=VMEM_LIMIT)


def _shift_rows(x, d, prev):
    assert 0 < d < SUBLANES and prev.shape[0] == SUBLANES
    rolled = pltpu.roll(x, d, 0)
    rolled_prev = pltpu.roll(prev, d, 0)
    row = lax.broadcasted_iota(jnp.int32, prev.shape, 0)
    top = jnp.where(row < d, rolled_prev, rolled[:SUBLANES])
    return jnp.concatenate([top, rolled[SUBLANES:]], axis=0)


def _shift_rows_fill(x, d, fill):
    n = x.shape[0]
    if d % SUBLANES == 0:
        head = jnp.full((d,) + x.shape[1:], fill, x.dtype)
        return jnp.concatenate([head, x[:n - d]], axis=0)
    rolled = pltpu.roll(x, d, 0)
    row = lax.broadcasted_iota(jnp.int32, x.shape, 0)
    return jnp.where(row < d, jnp.asarray(fill, x.dtype), rolled)


def _ada_kernel(c_ref, w_ref, b_ref, o_ref):
    cond = _silu(c_ref[...])
    o_ref[...] = jnp.sum(cond * w_ref[...], axis=0, keepdims=True) + b_ref[...]


def _ada(c_col, w_ada, b_ada, layer):
    d = c_col.shape[0]
    n = w_ada.shape[2]
    tn = 512
    return pl.pallas_call(
        _ada_kernel,
        grid=(n // tn,),
        in_specs=[
            pl.BlockSpec((d, 1), lambda j: (0, 0)),
            pl.BlockSpec((None, d, tn), lambda j: (layer, 0, j)),
            pl.BlockSpec((None, 1, tn), lambda j: (layer, 0, j)),
        ],
        out_specs=pl.BlockSpec((1, tn), lambda j: (0, j)),
        out_shape=jax.ShapeDtypeStruct((1, n), jnp.float32),
        compiler_params=_params("arbitrary"),
        name="ada",
    )(c_col, w_ada, b_ada)


def _proj_kernel(x_ref, shift_ref, scale_ref, w_ref, o_ref, h_ref):
    @pl.when(pl.program_id(1) == 0)
    def _():
        h = x_ref[...] * (1.0 + scale_ref[...]) + shift_ref[...]
        h_ref[...] = h.astype(jnp.bfloat16)

    o_ref[...] = jnp.dot(h_ref[...], w_ref[...], preferred_element_type=jnp.float32)


def _proj(x, ada, w_in_bf16, layer):
    s, d = x.shape
    n = w_in_bf16.shape[2]
    tm, tn = 1024, 1024
    return pl.pallas_call(
        _proj_kernel,
        grid=(s // tm, n // tn),
        in_specs=[
            pl.BlockSpec((tm, d), lambda i, j: (i, 0)),
            pl.BlockSpec((1, d), lambda i, j: (0, 0)),
            pl.BlockSpec((1, d), lambda i, j: (0, 1)),
            pl.BlockSpec((None, d, tn), lambda i, j: (layer, 0, j)),
        ],
        out_specs=pl.BlockSpec((tm, tn), lambda i, j: (i, j)),
        out_shape=jax.ShapeDtypeStruct((s, n), jnp.float32),
        scratch_shapes=[pltpu.VMEM((tm, d), jnp.bfloat16)],
        compiler_params=_params("arbitrary", "arbitrary"),
        name="proj",
    )(x, ada, ada, w_in_bf16)


ATT_TILE = max(DILATIONS) * BLK
ATT_BLOCKS = ATT_TILE // BLK
N_PAIRS = ATT_HEADS // 2
DEINT = 4
PLANE = ATT_TILE // DEINT
QROWS = BLK // DEINT
MIX_ROWS = 256
ATT_UNROLL = 8
MASKED = -1e30


def _t5_bucket_tables():
    assert DILATIONS == (1, 4, 16) and DEINT == 4
    i = np.arange(BLK)[:, None]
    j = np.arange(2 * BLK)[None, :]
    delta = i + BLK - j
    valid = (delta >= 0) & (delta <= SPAN)
    max_exact = REL_BUCKETS // 2
    tables = []
    for dil in DILATIONS:
        dist = np.clip(delta, 0, SPAN) * dil
        nf = np.maximum(dist, 1).astype(np.float32)
        large = max_exact + (np.log(nf / np.float32(max_exact)) / np.float32(math.log(REL_MAX_DIST / max_exact))
                             * np.float32(REL_BUCKETS - max_exact)).astype(np.int32)
        bucket = np.where(dist < max_exact, dist, np.minimum(large, REL_BUCKETS - 1))
        table = np.stack([np.where(valid & (j >= BLK), bucket, -1), np.where(valid, bucket, -1)])
        if dil == 1:
            rows = np.array([DEINT * a + r for r in range(DEINT) for a in range(QROWS)])
            cols = np.array([blk * BLK + DEINT * a + r
                             for r in range(DEINT) for blk in range(2) for a in range(QROWS)])
            table = table[:, rows][:, :, cols]
        tables.append(table)
    return jnp.asarray(np.stack(tables), jnp.int32)


def _attend(q, k, v, bias, low):
    q = (q * (ATT_HEAD_DIM ** -0.5)).astype(jnp.bfloat16)
    zero = jnp.zeros_like(q)
    q2 = jnp.concatenate([jnp.where(low, q, zero), jnp.where(low, zero, q)], axis=0)
    sc = lax.dot_general(q2, k.astype(jnp.bfloat16), (((1,), (1,)), ((), ())),
                         preferred_element_type=jnp.float32) + bias
    m = jnp.max(sc, axis=-1, keepdims=True)
    p = jnp.exp(sc - m)
    l = jnp.sum(p, axis=-1, keepdims=True)
    pv = jnp.dot(p.astype(jnp.bfloat16), v.astype(jnp.bfloat16), preferred_element_type=jnp.float32)
    o = jnp.where(low, pv[:BLK], pv[BLK:])
    l2 = jnp.where(low, l[:BLK], l[BLK:])
    m2 = jnp.where(low, m[:BLK], m[BLK:])
    return o / l2, m2 + jnp.log(l2)


def _attn_kernel(rb_ref, bucket_ref, q_ref, k_ref, v_ref, g_ref, y_ref,
                 q4_ref, k4_ref, v4_ref, o4_ref, lse4_ref, ynat_ref, bias_ref):
    pair = pl.program_id(0)
    t = pl.program_id(1)
    lane = lax.broadcasted_iota(jnp.int32, (BLK, LANES), 1)
    low = lane < ATT_HEAD_DIM
    n_pat = len(DILATIONS)

    @pl.when(t == 0)
    def _():
        k4_ref[:, 0:PLANE, :] = jnp.zeros((DEINT, PLANE, LANES), jnp.float32)
        v4_ref[:, 0:PLANE, :] = jnp.zeros((DEINT, PLANE, LANES), jnp.float32)
        for g in range(n_pat):
            bucket = bucket_ref[g, 1]
            no_prev = bucket_ref[g, 0] < 0
            hits = [bucket == b for b in range(REL_BUCKETS)]
            for half in range(2):
                acc = jnp.full(bucket.shape, MASKED, jnp.float32)
                for b in range(REL_BUCKETS):
                    acc = jnp.where(hits[b], rb_ref[b, 2 * pair + half], acc)
                bias_ref[g, 1, half * BLK:(half + 1) * BLK, :] = acc
                bias_ref[g, 0, half * BLK:(half + 1) * BLK, :] = jnp.where(no_prev, MASKED, acc)

    for r in range(DEINT):
        q4_ref[r] = q_ref[pl.ds(r, PLANE, stride=DEINT), :]
        k4_ref[r, PLANE:2 * PLANE, :] = k_ref[pl.ds(r, PLANE, stride=DEINT), :]
        v4_ref[r, PLANE:2 * PLANE, :] = v_ref[pl.ds(r, PLANE, stride=DEINT), :]

    def d1_qk(n):
        a0 = pl.multiple_of(n * QROWS, QROWS)
        q = jnp.concatenate([q4_ref[r, pl.ds(a0, QROWS), :] for r in range(DEINT)], axis=0)
        k = jnp.concatenate([k4_ref[r, pl.ds(PLANE - QROWS + a0, 2 * QROWS), :] for r in range(DEINT)], axis=0)
        return q, k, (t > 0) | (n > 0)

    def d1_v(n):
        a0 = pl.multiple_of(n * QROWS, QROWS)
        return jnp.concatenate([v4_ref[r, pl.ds(PLANE - QROWS + a0, 2 * QROWS), :] for r in range(DEINT)], axis=0)

    def d1_store(n, o, lse):
        a0 = pl.multiple_of(n * QROWS, QROWS)
        for r in range(DEINT):
            o4_ref[0, r, pl.ds(a0, QROWS), :] = o[r * QROWS:(r + 1) * QROWS]
            lse4_ref[0, r, pl.ds(a0, QROWS), :] = lse[r * QROWS:(r + 1) * QROWS]

    def d4_qk(b):
        r, n = b // DEINT, b % DEINT
        a0 = pl.multiple_of(n * BLK, BLK)
        return (q4_ref[r, pl.ds(a0, BLK), :], k4_ref[r, pl.ds(PLANE - BLK + a0, 2 * BLK), :],
                (t > 0) | (n > 0))

    def d4_v(b):
        r, n = b // DEINT, b % DEINT
        return v4_ref[r, pl.ds(PLANE - BLK + pl.multiple_of(n * BLK, BLK), 2 * BLK), :]

    def d4_store(b, o, lse):
        r, n = b // DEINT, b % DEINT
        a0 = pl.multiple_of(n * BLK, BLK)
        o4_ref[1, r, pl.ds(a0, BLK), :] = o
        lse4_ref[1, r, pl.ds(a0, BLK), :] = lse

    def d16_qk(b):
        lo, hi = b // DEINT, b % DEINT
        return (q4_ref[lo, pl.ds(hi, BLK, stride=DEINT), :],
                k4_ref[lo, pl.ds(hi, 2 * BLK, stride=DEINT), :], t > 0)

    def d16_v(b):
        lo, hi = b // DEINT, b % DEINT
        return v4_ref[lo, pl.ds(hi, 2 * BLK, stride=DEINT), :]

    def d16_store(b, o, lse):
        lo, hi = b // DEINT, b % DEINT
        o4_ref[2, lo, pl.ds(hi, BLK, stride=DEINT), :] = o
        lse4_ref[2, lo, pl.ds(hi, BLK, stride=DEINT), :] = lse

    for g, (qk, vload, store) in enumerate(((d1_qk, d1_v, d1_store), (d4_qk, d4_v, d4_store),
                                            (d16_qk, d16_v, d16_store))):
        def unit(u, carry, g=g, qk=qk, vload=vload, store=store):
            q, k, has_prev = qk(u)
            o, lse = _attend(q, k, vload(u), bias_ref[g, jnp.where(has_prev, 1, 0)], low)
            store(u, o, lse)
            return carry

        lax.fori_loop(0, ATT_BLOCKS, unit, 0, unroll=ATT_UNROLL)

    for r in range(DEINT):
        def mix(c, carry, r=r):
            a0 = pl.multiple_of(c * MIX_ROWS, MIX_ROWS)
            nat = pl.ds(DEINT * a0 + r, MIX_ROWS, stride=DEINT)
            a = [lse4_ref[g, r, pl.ds(a0, MIX_ROWS), :] for g in range(n_pat)]
            m = functools.reduce(jnp.maximum, a)
            e = [jnp.exp(x - m) for x in a]
            num = sum(e[g] * o4_ref[g, r, pl.ds(a0, MIX_ROWS), :] for g in range(n_pat))
            ynat_ref[nat, :] = num / sum(e) * _silu(g_ref[nat, :])
            return carry

        lax.fori_loop(0, PLANE // MIX_ROWS, mix, 0)

    y_ref[...] = ynat_ref[...].astype(y_ref.dtype)
    k4_ref[:, 0:PLANE, :] = k4_ref[:, PLANE:2 * PLANE, :]
    v4_ref[:, 0:PLANE, :] = v4_ref[:, PLANE:2 * PLANE, :]


def _attn(proj, rel_bias):
    s = proj.shape[0]
    assert s % ATT_TILE == 0
    lanes_per_col = BR // LANES
    col = lambda c: pl.BlockSpec((ATT_TILE, LANES), lambda p, t, c=c: (t, c * lanes_per_col + p))
    n_pat = len(DILATIONS)
    return pl.pallas_call(
        _attn_kernel,
        grid=(N_PAIRS, s // ATT_TILE),
        in_specs=[
            pl.BlockSpec(memory_space=pltpu.SMEM),
            pl.BlockSpec((n_pat, 2, BLK, 2 * BLK), lambda p, t: (0, 0, 0, 0)),
            col(COL_Q), col(COL_K), col(COL_V), col(COL_B_G),
        ],
        out_specs=pl.BlockSpec((ATT_TILE, LANES), lambda p, t: (t, p)),
        out_shape=jax.ShapeDtypeStruct((s, BR), jnp.bfloat16),
        scratch_shapes=[pltpu.VMEM((DEINT, PLANE, LANES), jnp.float32),
                        pltpu.VMEM((DEINT, 2 * PLANE, LANES), jnp.float32),
                        pltpu.VMEM((DEINT, 2 * PLANE, LANES), jnp.float32),
                        pltpu.VMEM((n_pat, DEINT, PLANE, LANES), jnp.float32),
                        pltpu.VMEM((n_pat, DEINT, PLANE, LANES), jnp.float32),
                        pltpu.VMEM((ATT_TILE, LANES), jnp.float32),
                        pltpu.VMEM((n_pat, 2, 2 * BLK, 2 * BLK), jnp.float32)],
        compiler_params=_params("arbitrary", "arbitrary"),
        name="dilated_attn",
    )(rel_bias, _t5_bucket_tables(), proj, proj, proj, proj)


def _lru_kernel(cx_ref, cg_ref, cw_ref, cb_ref, w_ref, b_ref, lam_ref, y_ref, prev_ref, h_ref):
    @pl.when(pl.program_id(0) == 0)
    def _():
        prev_ref[...] = jnp.zeros_like(prev_ref)
        h_ref[...] = jnp.zeros_like(h_ref)

    cx = cx_ref[...]
    tm = cx.shape[0]
    prev = prev_ref[...]
    xc = _shift_rows(cx, 3, prev) * cw_ref[0:1, :]
    xc = xc + _shift_rows(cx, 2, prev) * cw_ref[1:2, :]
    xc = xc + _shift_rows(cx, 1, prev) * cw_ref[2:3, :]
    xc = xc + cx * cw_ref[3:4, :]
    xc = xc + cb_ref[...]
    prev_ref[...] = cx[tm - SUBLANES:]

    z = jnp.dot(xc.astype(jnp.bfloat16), w_ref[...], preferred_element_type=jnp.float32) + b_ref[...]
    r = jax.nn.sigmoid(z[:, :BR])
    ig = jax.nn.sigmoid(z[:, BR:])
    neg_lam = -lam_ref[...]
    softplus = jnp.maximum(neg_lam, 0.0) + _log1p(jnp.exp(-jnp.abs(neg_lam)))
    log_a = -LRU_C * r * softplus
    a = jnp.exp(log_a)
    b = jnp.sqrt(-_expm1(2.0 * log_a)) * ig * xc

    d = 1
    while d < tm:
        b = b + a * _shift_rows_fill(b, d, 0.0)
        a = a * _shift_rows_fill(a, d, 1.0)
        d *= 2
    h = b + a * h_ref[...]
    h_ref[...] = h[tm - 1:tm]
    y_ref[...] = (h * _silu(cg_ref[...])).astype(y_ref.dtype)


def _block_diag(w):
    hh, n, m = w.shape
    eye = jnp.eye(hh, dtype=w.dtype)
    return jnp.einsum('hij,hg->higj', w, eye).reshape(hh * n, hh * m)


def _lru(proj, conv_c, conv_c_b, w_cat, b_cat, lru_lambda, layer):
    s = proj.shape[0]
    tm = 512
    vec = lambda a: pl.BlockSpec((None,) + a.shape[1:], lambda i, nd=a.ndim: (layer,) + (0,) * (nd - 1))
    conv_c_b = conv_c_b.reshape(DEPTH, 1, BR)
    lam = lru_lambda.reshape(DEPTH, 1, BR)
    return pl.pallas_call(
        _lru_kernel,
        grid=(s // tm,),
        in_specs=[pl.BlockSpec((tm, BR), lambda i: (i, COL_C_X)),
                  pl.BlockSpec((tm, BR), lambda i: (i, COL_C_G)),
                  vec(conv_c), vec(conv_c_b), vec(w_cat), vec(b_cat), vec(lam)],
        out_specs=pl.BlockSpec((tm, BR), lambda i: (i, 0)),
        out_shape=jax.ShapeDtypeStruct((s, BR), jnp.bfloat16),
        scratch_shapes=[pltpu.VMEM((SUBLANES, BR), jnp.float32), pltpu.VMEM((1, BR), jnp.float32)],
        compiler_params=_params("arbitrary"),
        name="rglru",
    )(proj, proj, conv_c, conv_c_b, w_cat, b_cat, lam)


N_STATE = S5_GROUPS * S5_STATE
S5_TM = 512
S5_CHUNKS = BR // LANES
S5_CHUNK_STATES = N_STATE // S5_CHUNKS
SLABS_PER_CHUNK = S5_CHUNK_STATES // LANES
N_SLABS = N_STATE // LANES
SLAB_PITCH = S5_TM + SUBLANES
S5_UNROLL = 8


def _s5_kernel(u_ref, g_ref, bmat_ref, ar_ref, ai_ref, cmat_ref, d_ref, wglu_ref, bglu_ref,
               y_ref, xs_ref, state_ref):
    @pl.when(pl.program_id(0) == 0)
    def _():
        state_ref[...] = jnp.zeros_like(state_ref)

    u = u_ref[...]
    tm = u.shape[0]
    ub = u.astype(jnp.bfloat16)
    for j in range(S5_CHUNKS):
        bu = jnp.dot(ub[:, j * LANES:(j + 1) * LANES], bmat_ref[j], preferred_element_type=jnp.float32)
        for part in range(2):
            for cc in range(SLABS_PER_CHUNK):
                slab = part * N_SLABS + j * SLABS_PER_CHUNK + cc
                col = part * S5_CHUNK_STATES + cc * LANES
                xs_ref[pl.ds(slab * SLAB_PITCH, tm), :] = bu[:, col:col + LANES]

    n_vreg = N_SLABS // SUBLANES
    a_re = [ar_ref[v * SUBLANES:(v + 1) * SUBLANES, :] for v in range(n_vreg)]
    a_im = [ai_ref[v * SUBLANES:(v + 1) * SUBLANES, :] for v in range(n_vreg)]

    def rows(part, v, t):
        return pl.ds((part * N_SLABS + v * SUBLANES) * SLAB_PITCH + t, SUBLANES, stride=SLAB_PITCH)

    def step(t, x):
        new = []
        for v in range(n_vreg):
            xr, xi = x[2 * v], x[2 * v + 1]
            nr = (a_re[v] * xr + xs_ref[rows(0, v, t), :]) - a_im[v] * xi
            ni = (a_re[v] * xi + xs_ref[rows(1, v, t), :]) + a_im[v] * xr
            xs_ref[rows(0, v, t), :] = nr
            xs_ref[rows(1, v, t), :] = ni
            new += [nr, ni]
        return tuple(new)

    x0 = tuple(state_ref[i] for i in range(2 * n_vreg))
    x1 = lax.fori_loop(0, tm, step, x0, unroll=S5_UNROLL)
    for i in range(2 * n_vreg):
        state_ref[i] = x1[i]

    ys = []
    for j in range(S5_CHUNKS):
        xcat = jnp.concatenate(
            [xs_ref[pl.ds((part * N_SLABS + j * SLABS_PER_CHUNK + cc) * SLAB_PITCH, tm), :]
             for part in range(2) for cc in range(SLABS_PER_CHUNK)], axis=1)
        ys.append(jnp.dot(xcat.astype(jnp.bfloat16), cmat_ref[j], preferred_element_type=jnp.float32))
    y = jnp.concatenate(ys, axis=1) + d_ref[...] * u
    y = 0.5 * y * (1.0 + jnp.tanh(math.sqrt(2.0 / math.pi) * (y + 0.044715 * (y * y * y))))
    gate = jnp.dot(y.astype(jnp.bfloat16), wglu_ref[...], preferred_element_type=jnp.float32) + bglu_ref[...]
    y = y * jax.nn.sigmoid(gate)
    y_ref[...] = (y * _silu(g_ref[...])).astype(y_ref.dtype)


def _chunk_block_diag(w):
    per = w.shape[0] // S5_CHUNKS
    return jnp.stack([_block_diag(w[j * per:(j + 1) * per]) for j in range(S5_CHUNKS)])


def _s5_discretize(lam_re, lam_im, log_dt, b_re, b_im, c_re, c_im):
    dt = jnp.exp(log_dt)[:, None]
    mag = jnp.exp(lam_re * dt)
    ab_re = mag * jnp.cos(lam_im * dt)
    ab_im = mag * jnp.sin(lam_im * dt)
    den = lam_re * lam_re + lam_im * lam_im
    f_re = ((ab_re - 1.0) * lam_re + ab_im * lam_im) / den
    f_im = (ab_im * lam_re - (ab_re - 1.0) * lam_im) / den
    bb_re = f_re[..., None] * b_re - f_im[..., None] * b_im
    bb_im = f_re[..., None] * b_im + f_im[..., None] * b_re
    bmat = jnp.concatenate([_chunk_block_diag(jnp.swapaxes(bb_re, 1, 2)),
                            _chunk_block_diag(jnp.swapaxes(bb_im, 1, 2))], axis=2)
    cmat = jnp.concatenate([_chunk_block_diag(jnp.swapaxes(c_re, 1, 2)),
                            -_chunk_block_diag(jnp.swapaxes(c_im, 1, 2))], axis=1)
    return (bmat.astype(jnp.bfloat16), cmat.astype(jnp.bfloat16),
            ab_re.reshape(N_SLABS, LANES), ab_im.reshape(N_SLABS, LANES))


def _s5(proj, bmat, a_re, a_im, cmat, s5_d, w_glu_bf16, b_glu, layer):
    s = proj.shape[0]
    tm = S5_TM
    const = lambda a: pl.BlockSpec(a.shape, lambda i, nd=a.ndim: (0,) * nd)
    vec = lambda a: pl.BlockSpec((None,) + a.shape[1:], lambda i, nd=a.ndim: (layer,) + (0,) * (nd - 1))
    s5_d = s5_d.reshape(DEPTH, 1, BR)
    b_glu = b_glu.reshape(DEPTH, 1, BR)
    return pl.pallas_call(
        _s5_kernel,
        grid=(s // tm,),
        in_specs=[pl.BlockSpec((tm, BR), lambda i: (i, COL_D_U)),
                  pl.BlockSpec((tm, BR), lambda i: (i, COL_D_G)),
                  const(bmat), const(a_re), const(a_im), const(cmat),
                  vec(s5_d), vec(w_glu_bf16), vec(b_glu)],
        out_specs=pl.BlockSpec((tm, BR), lambda i: (i, 0)),
        out_shape=jax.ShapeDtypeStruct((s, BR), jnp.bfloat16),
        scratch_shapes=[pltpu.VMEM((2 * N_SLABS * SLAB_PITCH, LANES), jnp.float32),
                        pltpu.VMEM((2 * N_SLABS // SUBLANES, SUBLANES, LANES), jnp.float32)],
        compiler_params=_params("arbitrary"),
        name="s5",
    )(proj, proj, bmat, a_re, a_im, cmat, s5_d, w_glu_bf16, b_glu)


def _out_kernel(x_ref, ab_ref, ac_ref, ax_ref, ag_ref, cw_ref, yb_ref, yc_ref, yd_ref,
                w_ref, gate_ref, g_ref, b_ref, o_ref, prev_ref):
    @pl.when(pl.program_id(0) == 0)
    def _():
        prev_ref[...] = jnp.zeros_like(prev_ref)

    u = ac_ref[...] * ax_ref[...]
    prev = prev_ref[...]
    tm = u.shape[0]
    conv = _shift_rows(u, 2, prev) * cw_ref[0:1, :]
    conv = conv + _shift_rows(u, 1, prev) * cw_ref[1:2, :]
    conv = conv + u * cw_ref[2:3, :]
    prev_ref[...] = u[tm - SUBLANES:]
    ya = (ab_ref[...] * conv * _silu(ag_ref[...])).astype(jnp.bfloat16)

    y = jnp.dot(ya, w_ref[0 * BR:1 * BR, :], preferred_element_type=jnp.float32)
    y = y + jnp.dot(yb_ref[...], w_ref[1 * BR:2 * BR, :], preferred_element_type=jnp.float32)
    y = y + jnp.dot(yc_ref[...], w_ref[2 * BR:3 * BR, :], preferred_element_type=jnp.float32)
    y = y + jnp.dot(yd_ref[...], w_ref[3 * BR:4 * BR, :], preferred_element_type=jnp.float32)
    z = ALPHA * x_ref[...] + (1.0 + gate_ref[...]) * y
    mu = jnp.mean(z, axis=-1, keepdims=True)
    zc = z - mu
    var = jnp.mean(zc * zc, axis=-1, keepdims=True)
    o_ref[...] = zc * lax.rsqrt(var + LN_EPS) * g_ref[...] + b_ref[...]


def _out(x, proj, conv_a, yb, yc, yd, w_out_bf16, ada, ln_g, ln_b, layer):
    s, d = x.shape
    tm = 512
    col = lambda c: pl.BlockSpec((tm, BR), lambda i, c=c: (i, c))
    branch = pl.BlockSpec((tm, BR), lambda i: (i, 0))
    vec = pl.BlockSpec((None, 1, d), lambda i: (layer, 0, 0))
    return pl.pallas_call(
        _out_kernel,
        grid=(s // tm,),
        in_specs=[pl.BlockSpec((tm, d), lambda i: (i, 0)),
                  col(COL_A_B), col(COL_A_C), col(COL_A_X), col(COL_A_G),
                  pl.BlockSpec((None,) + conv_a.shape[1:], lambda i: (layer, 0, 0)),
                  branch, branch, branch,
                  pl.BlockSpec((None, 4 * BR, d), lambda i: (layer, 0, 0), pipeline_mode=pl.Buffered(1)),
                  pl.BlockSpec((1, d), lambda i: (0, 2)),
                  vec, vec],
        out_specs=pl.BlockSpec((tm, d), lambda i: (i, 0)),
        out_shape=jax.ShapeDtypeStruct((s, d), jnp.float32),
        scratch_shapes=[pltpu.VMEM((SUBLANES, BR), jnp.float32)],
        compiler_params=_params("arbitrary"),
        name="out_proj_ln",
    )(x, proj, proj, proj, proj, conv_a, yb, yc, yd, w_out_bf16, ada,
      ln_g.reshape(DEPTH, 1, d), ln_b.reshape(DEPTH, 1, d))


def kernel(x, c, rel_bias, w_ada, b_ada, w_in, conv_a, conv_c, conv_c_b, lru_wa, lru_ba, lru_wx, lru_bx, lru_lambda, s5_lam_re, s5_lam_im, s5_log_dt, s5_b_re, s5_b_im, s5_c_re, s5_c_im, s5_d, s5_w_glu, s5_b_glu, w_out, ln_g, ln_b):
    bsz, s, d = x.shape
    assert bsz == 1 and w_in.shape == (DEPTH, d, N_IN)
    xs = x.reshape(s, d)
    c_col = c.reshape(d, 1)
    b_ada3 = b_ada.reshape(DEPTH, 1, 3 * d)
    w_in_bf16 = w_in.astype(jnp.bfloat16)
    w_out_bf16 = w_out.astype(jnp.bfloat16)
    w_glu_bf16 = s5_w_glu.astype(jnp.bfloat16)
    lru_w = jnp.stack([jnp.concatenate([_block_diag(lru_wa[l]), _block_diag(lru_wx[l])], axis=1)
                       for l in range(DEPTH)]).astype(jnp.bfloat16)
    lru_b = jnp.concatenate([lru_ba, lru_bx], axis=1).reshape(DEPTH, 1, 2 * BR)

    for l in range(DEPTH):
        ada = _ada(c_col, w_ada, b_ada3, l)
        proj = _proj(xs, ada, w_in_bf16, l)
        yb = _attn(proj, rel_bias)
        yc = _lru(proj, conv_c, conv_c_b, lru_w, lru_b, lru_lambda, l)
        bmat, cmat, a_re, a_im = _s5_discretize(
            s5_lam_re[l], s5_lam_im[l], s5_log_dt[l], s5_b_re[l], s5_b_im[l], s5_c_re[l], s5_c_im[l])
        yd = _s5(proj, bmat, a_re, a_im, cmat, s5_d, w_glu_bf16, s5_b_glu, l)
        xs = _out(xs, proj, conv_a, yb, yc, yd, w_out_bf16, ada, ln_g, ln_b, l)
    return xs.reshape(bsz, s, d)
```

```python
import functools
import math

import numpy as np
import jax
import jax.numpy as jnp
from jax import lax
from jax.experimental import pallas as pl
from jax.experimental.pallas import tpu as pltpu

BR = 512
N_IN = 12 * BR
ATT_HEADS = 8
ATT_HEAD_DIM = 64
BLK = 128
SPAN = 128
DILATIONS = (1, 4, 16)
REL_BUCKETS = 32
REL_MAX_DIST = 2048
LRU_HEADS = 8
LRU_C = 8.0
S5_CH = 16
S5_GROUPS = 32
S5_STATE = 64
DEPTH = 2
ALPHA = (2 * DEPTH) ** 0.25
LN_EPS = 1e-5

SUBLANES = 8
LANES = 128
VMEM_LIMIT = 56 * 1024 * 1024

COL_A_B, COL_A_C, COL_A_X, COL_A_G = 0, 1, 2, 3
COL_Q, COL_K, COL_V, COL_B_G = 4, 5, 6, 7
COL_C_X, COL_C_G = 8, 9
COL_D_U, COL_D_G = 10, 11


def _silu(x):
    return x * jax.nn.sigmoid(x)


def _log1p(x):
    w = 1.0 + x
    return jnp.where(w == 1.0, x, x * jnp.log(w) / (w - 1.0))


def _expm1(x):
    e = jnp.exp(x)
    return jnp.where(e == 1.0, x, (e - 1.0) * x / jnp.log(e))


def _params(*sem):
    return pltpu.CompilerParams(dimension_semantics=sem, vmem_limit_bytes=VMEM_LIMIT)


def _shift_rows(x, d, prev):
    assert 0 < d < SUBLANES and prev.shape[0] == SUBLANES
    rolled = pltpu.roll(x, d, 0)
    rolled_prev = pltpu.roll(prev, d, 0)
    row = lax.broadcasted_iota(jnp.int32, prev.shape, 0)
    top = jnp.where(row < d, rolled_prev, rolled[:SUBLANES])
    return jnp.concatenate([top, rolled[SUBLANES:]], axis=0)


def _shift_rows_fill(x, d, fill):
    n = x.shape[0]
    if d % SUBLANES == 0:
        head = jnp.full((d,) + x.shape[1:], fill, x.dtype)
        return jnp.concatenate([head, x[:n - d]], axis=0)
    rolled = pltpu.roll(x, d, 0)
    row = lax.broadcasted_iota(jnp.int32, x.shape, 0)
    return jnp.where(row < d, jnp.asarray(fill, x.dtype), rolled)


def _ada_kernel(c_ref, w_ref, b_ref, o_ref):
    cond = _silu(c_ref[...])
    o_ref[...] = jnp.sum(cond * w_ref[...], axis=0, keepdims=True) + b_ref[...]


def _ada(c_col, w_ada, b_ada, layer):
    d = c_col.shape[0]
    n = w_ada.shape[2]
    tn = 512
    return pl.pallas_call(
        _ada_kernel,
        grid=(n // tn,),
        in_specs=[
            pl.BlockSpec((d, 1), lambda j: (0, 0)),
            pl.BlockSpec((None, d, tn), lambda j: (layer, 0, j)),
            pl.BlockSpec((None, 1, tn), lambda j: (layer, 0, j)),
        ],
        out_specs=pl.BlockSpec((1, tn), lambda j: (0, j)),
        out_shape=jax.ShapeDtypeStruct((1, n), jnp.float32),
        compiler_params=_params("arbitrary"),
        name="ada",
    )(c_col, w_ada, b_ada)


def _proj_kernel(x_ref, shift_ref, scale_ref, w_ref, o_ref, h_ref):
    @pl.when(pl.program_id(1) == 0)
    def _():
        h = x_ref[...] * (1.0 + scale_ref[...]) + shift_ref[...]
        h_ref[...] = h.astype(jnp.bfloat16)

    o_ref[...] = jnp.dot(h_ref[...], w_ref[...].astype(jnp.bfloat16), preferred_element_type=jnp.float32)


def _proj(x, ada, w_in, layer):
    s, d = x.shape
    n = w_in.shape[2]
    tm, tn = 1024, 1024
    return pl.pallas_call(
        _proj_kernel,
        grid=(s // tm, n // tn),
        in_specs=[
            pl.BlockSpec((tm, d), lambda i, j: (i, 0)),
            pl.BlockSpec((1, d), lambda i, j: (0, 0)),
            pl.BlockSpec((1, d), lambda i, j: (0, 1)),
            pl.BlockSpec((None, d, tn), lambda i, j: (layer, 0, j)),
        ],
        out_specs=pl.BlockSpec((tm, tn), lambda i, j: (i, j)),
        out_shape=jax.ShapeDtypeStruct((s, n), jnp.float32),
        scratch_shapes=[pltpu.VMEM((tm, d), jnp.bfloat16)],
        compiler_params=_params("arbitrary", "arbitrary"),
        name="proj",
    )(x, ada, ada, w_in)


ATT_TILE = max(DILATIONS) * BLK
ATT_BLOCKS = ATT_TILE // BLK
N_PAIRS = ATT_HEADS // 2
DEINT = 4
PLANE = ATT_TILE // DEINT
QROWS = BLK // DEINT
MIX_ROWS = 256
ATT_UNROLL = 16
MASKED = -1e30


def _t5_bucket_tables():
    assert DILATIONS == (1, 4, 16) and DEINT == 4
    i = np.arange(BLK)[:, None]
    j = np.arange(2 * BLK)[None, :]
    delta = i + BLK - j
    valid = (delta >= 0) & (delta <= SPAN)
    max_exact = REL_BUCKETS // 2
    tables = []
    for dil in DILATIONS:
        dist = np.clip(delta, 0, SPAN) * dil
        nf = np.maximum(dist, 1).astype(np.float32)
        large = max_exact + (np.log(nf / np.float32(max_exact)) / np.float32(math.log(REL_MAX_DIST / max_exact))
                             * np.float32(REL_BUCKETS - max_exact)).astype(np.int32)
        bucket = np.where(dist < max_exact, dist, np.minimum(large, REL_BUCKETS - 1))
        table = np.stack([np.where(valid & (j >= BLK), bucket, -1), np.where(valid, bucket, -1)])
        if dil == 1:
            rows = np.array([DEINT * a + r for r in range(DEINT) for a in range(QROWS)])
            cols = np.array([blk * BLK + DEINT * a + r
                             for r in range(DEINT) for blk in range(2) for a in range(QROWS)])
            table = table[:, rows][:, :, cols]
        tables.append(table)
    return jnp.asarray(np.stack(tables), jnp.int32)


def _attend(q, k, v, bias, low):
    q = (q * (ATT_HEAD_DIM ** -0.5)).astype(jnp.bfloat16)
    zero = jnp.zeros_like(q)
    q2 = jnp.concatenate([jnp.where(low, q, zero), jnp.where(low, zero, q)], axis=0)
    sc = lax.dot_general(q2, k.astype(jnp.bfloat16), (((1,), (1,)), ((), ())),
                         preferred_element_type=jnp.float32) + bias
    m = jnp.max(sc, axis=-1, keepdims=True)
    p = jnp.exp(sc - m)
    l = jnp.sum(p, axis=-1, keepdims=True)
    pv = jnp.dot(p.astype(jnp.bfloat16), v.astype(jnp.bfloat16), preferred_element_type=jnp.float32)
    o = jnp.where(low, pv[:BLK], pv[BLK:])
    l2 = jnp.where(low, l[:BLK], l[BLK:])
    m2 = jnp.where(low, m[:BLK], m[BLK:])
    return o / l2, m2 + jnp.log(l2)


def _attn_kernel(rb_ref, bucket_ref, q_ref, k_ref, v_ref, g_ref, y_ref,
                 q4_ref, k4_ref, v4_ref, o4_ref, lse4_ref, ynat_ref, bias_ref):
    pair = pl.program_id(0)
    t = pl.program_id(1)
    lane = lax.broadcasted_iota(jnp.int32, (BLK, LANES), 1)
    low = lane < ATT_HEAD_DIM
    n_pat = len(DILATIONS)

    @pl.when(t == 0)
    def _():
        k4_ref[:, 0:PLANE, :] = jnp.zeros((DEINT, PLANE, LANES), jnp.float32)
        v4_ref[:, 0:PLANE, :] = jnp.zeros((DEINT, PLANE, LANES), jnp.float32)
        for g in range(n_pat):
            bucket = bucket_ref[g, 1]
            no_prev = bucket_ref[g, 0] < 0
            hits = [bucket == b for b in range(REL_BUCKETS)]
            for half in range(2):
                acc = jnp.full(bucket.shape, MASKED, jnp.float32)
                for b in range(REL_BUCKETS):
                    acc = jnp.where(hits[b], rb_ref[b, 2 * pair + half], acc)
                bias_ref[g, 1, half * BLK:(half + 1) * BLK, :] = acc
                bias_ref[g, 0, half * BLK:(half + 1) * BLK, :] = jnp.where(no_prev, MASKED, acc)

    for r in range(DEINT):
        q4_ref[r] = q_ref[pl.ds(r, PLANE, stride=DEINT), :]
        k4_ref[r, PLANE:2 * PLANE, :] = k_ref[pl.ds(r, PLANE, stride=DEINT), :]
        v4_ref[r, PLANE:2 * PLANE, :] = v_ref[pl.ds(r, PLANE, stride=DEINT), :]

    def d1_qk(n):
        a0 = pl.multiple_of(n * QROWS, QROWS)
        q = jnp.concatenate([q4_ref[r, pl.ds(a0, QROWS), :] for r in range(DEINT)], axis=0)
        k = jnp.concatenate([k4_ref[r, pl.ds(PLANE - QROWS + a0, 2 * QROWS), :] for r in range(DEINT)], axis=0)
        return q, k, (t > 0) | (n > 0)

    def d1_v(n):
        a0 = pl.multiple_of(n * QROWS, QROWS)
        return jnp.concatenate([v4_ref[r, pl.ds(PLANE - QROWS + a0, 2 * QROWS), :] for r in range(DEINT)], axis=0)

    def d1_store(n, o, lse):
        a0 = pl.multiple_of(n * QROWS, QROWS)
        for r in range(DEINT):
            o4_ref[0, r, pl.ds(a0, QROWS), :] = o[r * QROWS:(r + 1) * QROWS]
            lse4_ref[0, r, pl.ds(a0, QROWS), :] = lse[r * QROWS:(r + 1) * QROWS]

    def d4_qk(b):
        r, n = b // DEINT, b % DEINT
        a0 = pl.multiple_of(n * BLK, BLK)
        return (q4_ref[r, pl.ds(a0, BLK), :], k4_ref[r, pl.ds(PLANE - BLK + a0, 2 * BLK), :],
                (t > 0) | (n > 0))

    def d4_v(b):
        r, n = b // DEINT, b % DEINT
        return v4_ref[r, pl.ds(PLANE - BLK + pl.multiple_of(n * BLK, BLK), 2 * BLK), :]

    def d4_store(b, o, lse):
        r, n = b // DEINT, b % DEINT
        a0 = pl.multiple_of(n * BLK, BLK)
        o4_ref[1, r, pl.ds(a0, BLK), :] = o
        lse4_ref[1, r, pl.ds(a0, BLK), :] = lse

    def d16_qk(b):
        lo, hi = b // DEINT, b % DEINT
        return (q4_ref[lo, pl.ds(hi, BLK, stride=DEINT), :],
                k4_ref[lo, pl.ds(hi, 2 * BLK, stride=DEINT), :], t > 0)

    def d16_v(b):
        lo, hi = b // DEINT, b % DEINT
        return v4_ref[lo, pl.ds(hi, 2 * BLK, stride=DEINT), :]

    def d16_store(b, o, lse):
        lo, hi = b // DEINT, b % DEINT
        o4_ref[2, lo, pl.ds(hi, BLK, stride=DEINT), :] = o
        lse4_ref[2, lo, pl.ds(hi, BLK, stride=DEINT), :] = lse

    for g, (qk, vload, store) in enumerate(((d1_qk, d1_v, d1_store), (d4_qk, d4_v, d4_store),
                                            (d16_qk, d16_v, d16_store))):
        def unit(u, carry, g=g, qk=qk, vload=vload, store=store):
            q, k, has_prev = qk(u)
            o, lse = _attend(q, k, vload(u), bias_ref[g, jnp.where(has_prev, 1, 0)], low)
            store(u, o, lse)
            return carry

        lax.fori_loop(0, ATT_BLOCKS, unit, 0, unroll=ATT_UNROLL)

    for r in range(DEINT):
        def mix(c, carry, r=r):
            a0 = pl.multiple_of(c * MIX_ROWS, MIX_ROWS)
            nat = pl.ds(DEINT * a0 + r, MIX_ROWS, stride=DEINT)
            a = [lse4_ref[g, r, pl.ds(a0, MIX_ROWS), :] for g in range(n_pat)]
            m = functools.reduce(jnp.maximum, a)
            e = [jnp.exp(x - m) for x in a]
            num = sum(e[g] * o4_ref[g, r, pl.ds(a0, MIX_ROWS), :] for g in range(n_pat))
            ynat_ref[nat, :] = num / sum(e) * _silu(g_ref[nat, :])
            return carry

        lax.fori_loop(0, PLANE // MIX_ROWS, mix, 0)

    y_ref[...] = ynat_ref[...].astype(y_ref.dtype)
    k4_ref[:, 0:PLANE, :] = k4_ref[:, PLANE:2 * PLANE, :]
    v4_ref[:, 0:PLANE, :] = v4_ref[:, PLANE:2 * PLANE, :]


def _attn(proj, rel_bias):
    s = proj.shape[0]
    assert s % ATT_TILE == 0
    lanes_per_col = BR // LANES
    col = lambda c: pl.BlockSpec((ATT_TILE, LANES), lambda p, t, c=c: (t, c * lanes_per_col + p))
    n_pat = len(DILATIONS)
    return pl.pallas_call(
        _attn_kernel,
        grid=(N_PAIRS, s // ATT_TILE),
        in_specs=[
            pl.BlockSpec(memory_space=pltpu.SMEM),
            pl.BlockSpec((n_pat, 2, BLK, 2 * BLK), lambda p, t: (0, 0, 0, 0)),
            col(COL_Q), col(COL_K), col(COL_V), col(COL_B_G),
        ],
        out_specs=pl.BlockSpec((ATT_TILE, LANES), lambda p, t: (t, p)),
        out_shape=jax.ShapeDtypeStruct((s, BR), jnp.bfloat16),
        scratch_shapes=[pltpu.VMEM((DEINT, PLANE, LANES), jnp.float32),
                        pltpu.VMEM((DEINT, 2 * PLANE, LANES), jnp.float32),
                        pltpu.VMEM((DEINT, 2 * PLANE, LANES), jnp.float32),
                        pltpu.VMEM((n_pat, DEINT, PLANE, LANES), jnp.float32),
                        pltpu.VMEM((n_pat, DEINT, PLANE, LANES), jnp.float32),
                        pltpu.VMEM((ATT_TILE, LANES), jnp.float32),
                        pltpu.VMEM((n_pat, 2, 2 * BLK, 2 * BLK), jnp.float32)],
        compiler_params=_params("arbitrary", "arbitrary"),
        name="dilated_attn",
    )(rel_bias, _t5_bucket_tables(), proj, proj, proj, proj)


def _lru_kernel(cx_ref, cg_ref, cw_ref, cb_ref, w_ref, b_ref, lam_ref, y_ref, prev_ref, h_ref):
    @pl.when(pl.program_id(0) == 0)
    def _():
        prev_ref[...] = jnp.zeros_like(prev_ref)
        h_ref[...] = jnp.zeros_like(h_ref)

    cx = cx_ref[...]
    tm = cx.shape[0]
    prev = prev_ref[...]
    xc = _shift_rows(cx, 3, prev) * cw_ref[0:1, :]
    xc = xc + _shift_rows(cx, 2, prev) * cw_ref[1:2, :]
    xc = xc + _shift_rows(cx, 1, prev) * cw_ref[2:3, :]
    xc = xc + cx * cw_ref[3:4, :]
    xc = xc + cb_ref[...]
    prev_ref[...] = cx[tm - SUBLANES:]

    z = jnp.dot(xc.astype(jnp.bfloat16), w_ref[...], preferred_element_type=jnp.float32) + b_ref[...]
    r = jax.nn.sigmoid(z[:, :BR])
    ig = jax.nn.sigmoid(z[:, BR:])
    neg_lam = -lam_ref[...]
    softplus = jnp.maximum(neg_lam, 0.0) + _log1p(jnp.exp(-jnp.abs(neg_lam)))
    log_a = -LRU_C * r * softplus
    a = jnp.exp(log_a)
    b = jnp.sqrt(-_expm1(2.0 * log_a)) * ig * xc

    d = 1
    while d < tm:
        b = b + a * _shift_rows_fill(b, d, 0.0)
        a = a * _shift_rows_fill(a, d, 1.0)
        d *= 2
    h = b + a * h_ref[...]
    h_ref[...] = h[tm - 1:tm]
    y_ref[...] = (h * _silu(cg_ref[...])).astype(y_ref.dtype)


def _block_diag(w):
    hh, n, m = w.shape
    eye = jnp.eye(hh, dtype=w.dtype)
    return jnp.einsum('hij,hg->higj', w, eye).reshape(hh * n, hh * m)


def _lru(proj, conv_c, conv_c_b, w_cat, b_cat, lru_lambda, layer):
    s = proj.shape[0]
    tm = 512
    vec = lambda a: pl.BlockSpec((None,) + a.shape[1:], lambda i, nd=a.ndim: (layer,) + (0,) * (nd - 1))
    conv_c_b = conv_c_b.reshape(DEPTH, 1, BR)
    lam = lru_lambda.reshape(DEPTH, 1, BR)
    return pl.pallas_call(
        _lru_kernel,
        grid=(s // tm,),
        in_specs=[pl.BlockSpec((tm, BR), lambda i: (i, COL_C_X)),
                  pl.BlockSpec((tm, BR), lambda i: (i, COL_C_G)),
                  vec(conv_c), vec(conv_c_b), vec(w_cat), vec(b_cat), vec(lam)],
        out_specs=pl.BlockSpec((tm, BR), lambda i: (i, 0)),
        out_shape=jax.ShapeDtypeStruct((s, BR), jnp.bfloat16),
        scratch_shapes=[pltpu.VMEM((SUBLANES, BR), jnp.float32), pltpu.VMEM((1, BR), jnp.float32)],
        compiler_params=_params("arbitrary"),
        name="rglru",
    )(proj, proj, conv_c, conv_c_b, w_cat, b_cat, lam)


N_STATE = S5_GROUPS * S5_STATE
S5_TM = 512
S5_CHUNKS = BR // LANES
S5_CHUNK_STATES = N_STATE // S5_CHUNKS
SLABS_PER_CHUNK = S5_CHUNK_STATES // LANES
N_SLABS = N_STATE // LANES
SLAB_PITCH = S5_TM + SUBLANES
S5_UNROLL = 8


def _s5_kernel(u_ref, g_ref, bmat_ref, ar_ref, ai_ref, cmat_ref, d_ref, wglu_ref, bglu_ref,
               y_ref, xs_ref, state_ref):
    @pl.when(pl.program_id(0) == 0)
    def _():
        state_ref[...] = jnp.zeros_like(state_ref)

    u = u_ref[...]
    tm = u.shape[0]
    ub = u.astype(jnp.bfloat16)
    for j in range(S5_CHUNKS):
        bu = jnp.dot(ub[:, j * LANES:(j + 1) * LANES], bmat_ref[j], preferred_element_type=jnp.float32)
        for part in range(2):
            for cc in range(SLABS_PER_CHUNK):
                slab = part * N_SLABS + j * SLABS_PER_CHUNK + cc
                col = part * S5_CHUNK_STATES + cc * LANES
                xs_ref[pl.ds(slab * SLAB_PITCH, tm), :] = bu[:, col:col + LANES]

    n_vreg = N_SLABS // SUBLANES
    a_re = [ar_ref[v * SUBLANES:(v + 1) * SUBLANES, :] for v in range(n_vreg)]
    a_im = [ai_ref[v * SUBLANES:(v + 1) * SUBLANES, :] for v in range(n_vreg)]

    def rows(part, v, t):
        return pl.ds((part * N_SLABS + v * SUBLANES) * SLAB_PITCH + t, SUBLANES, stride=SLAB_PITCH)

    def step(t, x):
        new = []
        for v in range(n_vreg):
            xr, xi = x[2 * v], x[2 * v + 1]
            nr = (a_re[v] * xr + xs_ref[rows(0, v, t), :]) - a_im[v] * xi
            ni = (a_re[v] * xi + xs_ref[rows(1, v, t), :]) + a_im[v] * xr
            xs_ref[rows(0, v, t), :] = nr
            xs_ref[rows(1, v, t), :] = ni
            new += [nr, ni]
        return tuple(new)

    x0 = tuple(state_ref[i] for i in range(2 * n_vreg))
    x1 = lax.fori_loop(0, tm, step, x0, unroll=S5_UNROLL)
    for i in range(2 * n_vreg):
        state_ref[i] = x1[i]

    ys = []
    for j in range(S5_CHUNKS):
        xcat = jnp.concatenate(
            [xs_ref[pl.ds((part * N_SLABS + j * SLABS_PER_CHUNK + cc) * SLAB_PITCH, tm), :]
             for part in range(2) for cc in range(SLABS_PER_CHUNK)], axis=1)
        ys.append(jnp.dot(xcat.astype(jnp.bfloat16), cmat_ref[j], preferred_element_type=jnp.float32))
    y = jnp.concatenate(ys, axis=1) + d_ref[...] * u
    y = 0.5 * y * (1.0 + jnp.tanh(math.sqrt(2.0 / math.pi) * (y + 0.044715 * (y * y * y))))
    gate = jnp.dot(y.astype(jnp.bfloat16), wglu_ref[...], preferred_element_type=jnp.float32) + bglu_ref[...]
    y = y * jax.nn.sigmoid(gate)
    y_ref[...] = (y * _silu(g_ref[...])).astype(y_ref.dtype)


def _chunk_block_diag(w):
    per = w.shape[0] // S5_CHUNKS
    return jnp.stack([_block_diag(w[j * per:(j + 1) * per]) for j in range(S5_CHUNKS)])


def _s5_discretize(lam_re, lam_im, log_dt, b_re, b_im, c_re, c_im):
    dt = jnp.exp(log_dt)[:, None]
    mag = jnp.exp(lam_re * dt)
    ab_re = mag * jnp.cos(lam_im * dt)
    ab_im = mag * jnp.sin(lam_im * dt)
    den = lam_re * lam_re + lam_im * lam_im
    f_re = ((ab_re - 1.0) * lam_re + ab_im * lam_im) / den
    f_im = (ab_im * lam_re - (ab_re - 1.0) * lam_im) / den
    bb_re = f_re[..., None] * b_re - f_im[..., None] * b_im
    bb_im = f_re[..., None] * b_im + f_im[..., None] * b_re
    bmat = jnp.concatenate([_chunk_block_diag(jnp.swapaxes(bb_re, 1, 2)),
                            _chunk_block_diag(jnp.swapaxes(bb_im, 1, 2))], axis=2)
    cmat = jnp.concatenate([_chunk_block_diag(jnp.swapaxes(c_re, 1, 2)),
                            -_chunk_block_diag(jnp.swapaxes(c_im, 1, 2))], axis=1)
    return (bmat.astype(jnp.bfloat16), cmat.astype(jnp.bfloat16),
            ab_re.reshape(N_SLABS, LANES), ab_im.reshape(N_SLABS, LANES))


def _s5(proj, bmat, a_re, a_im, cmat, s5_d, w_glu_bf16, b_glu, layer):
    s = proj.shape[0]
    tm = S5_TM
    const = lambda a: pl.BlockSpec(a.shape, lambda i, nd=a.ndim: (0,) * nd)
    vec = lambda a: pl.BlockSpec((None,) + a.shape[1:], lambda i, nd=a.ndim: (layer,) + (0,) * (nd - 1))
    s5_d = s5_d.reshape(DEPTH, 1, BR)
    b_glu = b_glu.reshape(DEPTH, 1, BR)
    return pl.pallas_call(
        _s5_kernel,
        grid=(s // tm,),
        in_specs=[pl.BlockSpec((tm, BR), lambda i: (i, COL_D_U)),
                  pl.BlockSpec((tm, BR), lambda i: (i, COL_D_G)),
                  const(bmat), const(a_re), const(a_im), const(cmat),
                  vec(s5_d), vec(w_glu_bf16), vec(b_glu)],
        out_specs=pl.BlockSpec((tm, BR), lambda i: (i, 0)),
        out_shape=jax.ShapeDtypeStruct((s, BR), jnp.bfloat16),
        scratch_shapes=[pltpu.VMEM((2 * N_SLABS * SLAB_PITCH, LANES), jnp.float32),
                        pltpu.VMEM((2 * N_SLABS // SUBLANES, SUBLANES, LANES), jnp.float32)],
        compiler_params=_params("arbitrary"),
        name="s5",
    )(proj, proj, bmat, a_re, a_im, cmat, s5_d, w_glu_bf16, b_glu)


def _out_kernel(x_ref, ab_ref, ac_ref, ax_ref, ag_ref, cw_ref, yb_ref, yc_ref, yd_ref,
                w_ref, gate_ref, g_ref, b_ref, o_ref, prev_ref):
    @pl.when(pl.program_id(0) == 0)
    def _():
        prev_ref[...] = jnp.zeros_like(prev_ref)

    u = ac_ref[...] * ax_ref[...]
    prev = prev_ref[...]
    tm = u.shape[0]
    conv = _shift_rows(u, 2, prev) * cw_ref[0:1, :]
    conv = conv + _shift_rows(u, 1, prev) * cw_ref[1:2, :]
    conv = conv + u * cw_ref[2:3, :]
    prev_ref[...] = u[tm - SUBLANES:]
    ya = (ab_ref[...] * conv * _silu(ag_ref[...])).astype(jnp.bfloat16)

    y = jnp.dot(ya, w_ref[0 * BR:1 * BR, :], preferred_element_type=jnp.float32)
    y = y + jnp.dot(yb_ref[...], w_ref[1 * BR:2 * BR, :], preferred_element_type=jnp.float32)
    y = y + jnp.dot(yc_ref[...], w_ref[2 * BR:3 * BR, :], preferred_element_type=jnp.float32)
    y = y + jnp.dot(yd_ref[...], w_ref[3 * BR:4 * BR, :], preferred_element_type=jnp.float32)
    z = ALPHA * x_ref[...] + (1.0 + gate_ref[...]) * y
    mu = jnp.mean(z, axis=-1, keepdims=True)
    zc = z - mu
    var = jnp.mean(zc * zc, axis=-1, keepdims=True)
    o_ref[...] = zc * lax.rsqrt(var + LN_EPS) * g_ref[...] + b_ref[...]


def _out(x, proj, conv_a, yb, yc, yd, w_out_bf16, ada, ln_g, ln_b, layer):
    s, d = x.shape
    tm = 512
    col = lambda c: pl.BlockSpec((tm, BR), lambda i, c=c: (i, c))
    branch = pl.BlockSpec((tm, BR), lambda i: (i, 0))
    vec = pl.BlockSpec((None, 1, d), lambda i: (layer, 0, 0))
    return pl.pallas_call(
        _out_kernel,
        grid=(s // tm,),
        in_specs=[pl.BlockSpec((tm, d), lambda i: (i, 0)),
                  col(COL_A_B), col(COL_A_C), col(COL_A_X), col(COL_A_G),
                  pl.BlockSpec((None,) + conv_a.shape[1:], lambda i: (layer, 0, 0)),
                  branch, branch, branch,
                  pl.BlockSpec((None, 4 * BR, d), lambda i: (layer, 0, 0), pipeline_mode=pl.Buffered(1)),
                  pl.BlockSpec((1, d), lambda i: (0, 2)),
                  vec, vec],
        out_specs=pl.BlockSpec((tm, d), lambda i: (i, 0)),
        out_shape=jax.ShapeDtypeStruct((s, d), jnp.float32),
        scratch_shapes=[pltpu.VMEM((SUBLANES, BR), jnp.float32)],
        compiler_params=_params("arbitrary"),
        name="out_proj_ln",
    )(x, proj, proj, proj, proj, conv_a, yb, yc, yd, w_out_bf16, ada,
      ln_g.reshape(DEPTH, 1, d), ln_b.reshape(DEPTH, 1, d))


def kernel(x, c, rel_bias, w_ada, b_ada, w_in, conv_a, conv_c, conv_c_b, lru_wa, lru_ba, lru_wx, lru_bx, lru_lambda, s5_lam_re, s5_lam_im, s5_log_dt, s5_b_re, s5_b_im, s5_c_re, s5_c_im, s5_d, s5_w_glu, s5_b_glu, w_out, ln_g, ln_b):
    bsz, s, d = x.shape
    assert bsz == 1 and w_in.shape == (DEPTH, d, N_IN)
    xs = x.reshape(s, d)
    c_col = c.reshape(d, 1)
    b_ada3 = b_ada.reshape(DEPTH, 1, 3 * d)
    w_out_bf16 = w_out.astype(jnp.bfloat16)
    w_glu_bf16 = s5_w_glu.astype(jnp.bfloat16)
    lru_w = jnp.stack([jnp.concatenate([_block_diag(lru_wa[l]), _block_diag(lru_wx[l])], axis=1)
                       for l in range(DEPTH)]).astype(jnp.bfloat16)
    lru_b = jnp.concatenate([lru_ba, lru_bx], axis=1).reshape(DEPTH, 1, 2 * BR)

    for l in range(DEPTH):
        ada = _ada(c_col, w_ada, b_ada3, l)
        proj = _proj(xs, ada, w_in, l)
        yb = _attn(proj, rel_bias)
        yc = _lru(proj, conv_c, conv_c_b, lru_w, lru_b, lru_lambda, l)
        bmat, cmat, a_re, a_im = _s5_discretize(
            s5_lam_re[l], s5_lam_im[l], s5_log_dt[l], s5_b_re[l], s5_b_im[l], s5_c_re[l], s5_c_im[l])
        yd = _s5(proj, bmat, a_re, a_im, cmat, s5_d, w_glu_bf16, s5_b_glu, l)
        xs = _out(xs, proj, conv_a, yb, yc, yd, w_out_bf16, ada, ln_g, ln_b, l)
    return xs.reshape(bsz, s, d)
```

```python
import functools
import math

import numpy as np
import jax
import jax.numpy as jnp
from jax import lax
from jax.experimental import pallas as pl
from jax.experimental.pallas import tpu as pltpu

BR = 512
N_IN = 12 * BR
ATT_HEADS = 8
ATT_HEAD_DIM = 64
BLK = 128
SPAN = 128
DILATIONS = (1, 4, 16)
REL_BUCKETS = 32
REL_MAX_DIST = 2048
LRU_HEADS = 8
LRU_C = 8.0
S5_CH = 16
S5_GROUPS = 32
S5_STATE = 64
DEPTH = 2
ALPHA = (2 * DEPTH) ** 0.25
LN_EPS = 1e-5

SUBLANES = 8
LANES = 128
VMEM_LIMIT = 56 * 1024 * 1024

COL_A_B, COL_A_C, COL_A_X, COL_A_G = 0, 1, 2, 3
COL_Q, COL_K, COL_V, COL_B_G = 4, 5, 6, 7
COL_C_X, COL_C_G = 8, 9
COL_D_U, COL_D_G = 10, 11


def _silu(x):
    return x * jax.nn.sigmoid(x)


def _log1p(x):
    w = 1.0 + x
    return jnp.where(w == 1.0, x, x * jnp.log(w) / (w - 1.0))


def _expm1(x):
    e = jnp.exp(x)
    return jnp.where(e == 1.0, x, (e - 1.0) * x / jnp.log(e))


def _params(*sem):
    return pltpu.CompilerParams(dimension_semantics=sem, vmem_limit_bytes=VMEM_LIMIT)


def _shift_rows(x, d, prev):
    assert 0 < d < SUBLANES and prev.shape[0] == SUBLANES
    rolled = pltpu.roll(x, d, 0)
    rolled_prev = pltpu.roll(prev, d, 0)
    row = lax.broadcasted_iota(jnp.int32, prev.shape, 0)
    top = jnp.where(row < d, rolled_prev, rolled[:SUBLANES])
    return jnp.concatenate([top, rolled[SUBLANES:]], axis=0)


def _shift_rows_fill(x, d, fill):
    n = x.shape[0]
    if d % SUBLANES == 0:
        head = jnp.full((d,) + x.shape[1:], fill, x.dtype)
        return jnp.concatenate([head, x[:n - d]], axis=0)
    rolled = pltpu.roll(x, d, 0)
    row = lax.broadcasted_iota(jnp.int32, x.shape, 0)
    return jnp.where(row < d, jnp.asarray(fill, x.dtype), rolled)


def _ada_kernel(c_ref, w_ref, b_ref, o_ref):
    cond = _silu(c_ref[...])
    o_ref[...] = jnp.sum(cond * w_ref[...], axis=0, keepdims=True) + b_ref[...]


def _ada(c_col, w_ada, b_ada, layer):
    d = c_col.shape[0]
    n = w_ada.shape[2]
    tn = 512
    return pl.pallas_call(
        _ada_kernel,
        grid=(n // tn,),
        in_specs=[
            pl.BlockSpec((d, 1), lambda j: (0, 0)),
            pl.BlockSpec((None, d, tn), lambda j: (layer, 0, j)),
            pl.BlockSpec((None, 1, tn), lambda j: (layer, 0, j)),
        ],
        out_specs=pl.BlockSpec((1, tn), lambda j: (0, j)),
        out_shape=jax.ShapeDtypeStruct((1, n), jnp.float32),
        compiler_params=_params("arbitrary"),
        name="ada",
    )(c_col, w_ada, b_ada)


def _proj_kernel(x_ref, shift_ref, scale_ref, w_ref, o_ref, h_ref):
    @pl.when(pl.program_id(1) == 0)
    def _():
        h = x_ref[...] * (1.0 + scale_ref[...]) + shift_ref[...]
        h_ref[...] = h.astype(jnp.bfloat16)

    o_ref[...] = jnp.dot(h_ref[...], w_ref[...].astype(jnp.bfloat16), preferred_element_type=jnp.float32)


def _proj(x, ada, w_in, layer):
    s, d = x.shape
    n = w_in.shape[2]
    tm, tn = 1024, 1024
    return pl.pallas_call(
        _proj_kernel,
        grid=(s // tm, n // tn),
        in_specs=[
            pl.BlockSpec((tm, d), lambda i, j: (i, 0)),
            pl.BlockSpec((1, d), lambda i, j: (0, 0)),
            pl.BlockSpec((1, d), lambda i, j: (0, 1)),
            pl.BlockSpec((None, d, tn), lambda i, j: (layer, 0, j)),
        ],
        out_specs=pl.BlockSpec((tm, tn), lambda i, j: (i, j)),
        out_shape=jax.ShapeDtypeStruct((s, n), jnp.float32),
        scratch_shapes=[pltpu.VMEM((tm, d), jnp.bfloat16)],
        compiler_params=_params("arbitrary", "arbitrary"),
        name="proj",
    )(x, ada, ada, w_in)


ATT_TILE = max(DILATIONS) * BLK
ATT_BLOCKS = ATT_TILE // BLK
N_PAIRS = ATT_HEADS // 2
DEINT = 4
PLANE = ATT_TILE // DEINT
QROWS = BLK // DEINT
MIX_ROWS = 256
ATT_UNROLL = 16
MASKED = -1e30


def _t5_bucket_tables():
    assert DILATIONS == (1, 4, 16) and DEINT == 4
    i = np.arange(BLK)[:, None]
    j = np.arange(2 * BLK)[None, :]
    delta = i + BLK - j
    valid = (delta >= 0) & (delta <= SPAN)
    max_exact = REL_BUCKETS // 2
    tables = []
    for dil in DILATIONS:
        dist = np.clip(delta, 0, SPAN) * dil
        nf = np.maximum(dist, 1).astype(np.float32)
        large = max_exact + (np.log(nf / np.float32(max_exact)) / np.float32(math.log(REL_MAX_DIST / max_exact))
                             * np.float32(REL_BUCKETS - max_exact)).astype(np.int32)
        bucket = np.where(dist < max_exact, dist, np.minimum(large, REL_BUCKETS - 1))
        table = np.stack([np.where(valid & (j >= BLK), bucket, -1), np.where(valid, bucket, -1)])
        if dil == 1:
            rows = np.array([DEINT * a + r for r in range(DEINT) for a in range(QROWS)])
            cols = np.array([blk * BLK + DEINT * a + r
                             for r in range(DEINT) for blk in range(2) for a in range(QROWS)])
            table = table[:, rows][:, :, cols]
        tables.append(table)
    return jnp.asarray(np.stack(tables), jnp.int32)


def _attend(q, k, v, bias, low):
    q = (q * (ATT_HEAD_DIM ** -0.5)).astype(jnp.bfloat16)
    zero = jnp.zeros_like(q)
    q2 = jnp.concatenate([jnp.where(low, q, zero), jnp.where(low, zero, q)], axis=0)
    sc = lax.dot_general(q2, k.astype(jnp.bfloat16), (((1,), (1,)), ((), ())),
                         preferred_element_type=jnp.float32) + bias
    m = jnp.max(sc, axis=-1, keepdims=True)
    p = jnp.exp(sc - m)
    l = jnp.sum(p, axis=-1, keepdims=True)
    pv = jnp.dot(p.astype(jnp.bfloat16), v.astype(jnp.bfloat16), preferred_element_type=jnp.float32)
    o = jnp.where(low, pv[:BLK], pv[BLK:])
    l2 = jnp.where(low, l[:BLK], l[BLK:])
    m2 = jnp.where(low, m[:BLK], m[BLK:])
    return o / l2, m2 + jnp.log(l2)


def _attn_kernel(rb_ref, bucket_ref, q_ref, k_ref, v_ref, g_ref, y_ref,
                 q4_ref, k4_ref, v4_ref, o4_ref, lse4_ref, ynat_ref, bias_ref):
    pair = pl.program_id(0)
    t = pl.program_id(1)
    lane = lax.broadcasted_iota(jnp.int32, (BLK, LANES), 1)
    low = lane < ATT_HEAD_DIM
    n_pat = len(DILATIONS)

    @pl.when(t == 0)
    def _():
        k4_ref[:, 0:PLANE, :] = jnp.zeros((DEINT, PLANE, LANES), jnp.float32)
        v4_ref[:, 0:PLANE, :] = jnp.zeros((DEINT, PLANE, LANES), jnp.float32)
        for g in range(n_pat):
            bucket = bucket_ref[g, 1]
            no_prev = bucket_ref[g, 0] < 0
            hits = [bucket == b for b in range(REL_BUCKETS)]
            for half in range(2):
                acc = jnp.full(bucket.shape, MASKED, jnp.float32)
                for b in range(REL_BUCKETS):
                    acc = jnp.where(hits[b], rb_ref[b, 2 * pair + half], acc)
                bias_ref[g, 1, half * BLK:(half + 1) * BLK, :] = acc
                bias_ref[g, 0, half * BLK:(half + 1) * BLK, :] = jnp.where(no_prev, MASKED, acc)

    for r in range(DEINT):
        q4_ref[r] = q_ref[pl.ds(r, PLANE, stride=DEINT), :]
        k4_ref[r, PLANE:2 * PLANE, :] = k_ref[pl.ds(r, PLANE, stride=DEINT), :]
        v4_ref[r, PLANE:2 * PLANE, :] = v_ref[pl.ds(r, PLANE, stride=DEINT), :]

    def d1_qk(n):
        a0 = pl.multiple_of(n * QROWS, QROWS)
        q = jnp.concatenate([q4_ref[r, pl.ds(a0, QROWS), :] for r in range(DEINT)], axis=0)
        k = jnp.concatenate([k4_ref[r, pl.ds(PLANE - QROWS + a0, 2 * QROWS), :] for r in range(DEINT)], axis=0)
        return q, k, (t > 0) | (n > 0)

    def d1_v(n):
        a0 = pl.multiple_of(n * QROWS, QROWS)
        return jnp.concatenate([v4_ref[r, pl.ds(PLANE - QROWS + a0, 2 * QROWS), :] for r in range(DEINT)], axis=0)

    def d1_store(n, o, lse):
        a0 = pl.multiple_of(n * QROWS, QROWS)
        for r in range(DEINT):
            o4_ref[0, r, pl.ds(a0, QROWS), :] = o[r * QROWS:(r + 1) * QROWS]
            lse4_ref[0, r, pl.ds(a0, QROWS), :] = lse[r * QROWS:(r + 1) * QROWS]

    def d4_qk(b):
        r, n = b // DEINT, b % DEINT
        a0 = pl.multiple_of(n * BLK, BLK)
        return (q4_ref[r, pl.ds(a0, BLK), :], k4_ref[r, pl.ds(PLANE - BLK + a0, 2 * BLK), :],
                (t > 0) | (n > 0))

    def d4_v(b):
        r, n = b // DEINT, b % DEINT
        return v4_ref[r, pl.ds(PLANE - BLK + pl.multiple_of(n * BLK, BLK), 2 * BLK), :]

    def d4_store(b, o, lse):
        r, n = b // DEINT, b % DEINT
        a0 = pl.multiple_of(n * BLK, BLK)
        o4_ref[1, r, pl.ds(a0, BLK), :] = o
        lse4_ref[1, r, pl.ds(a0, BLK), :] = lse

    def d16_qk(b):
        lo, hi = b // DEINT, b % DEINT
        return (q4_ref[lo, pl.ds(hi, BLK, stride=DEINT), :],
                k4_ref[lo, pl.ds(hi, 2 * BLK, stride=DEINT), :], t > 0)

    def d16_v(b):
        lo, hi = b // DEINT, b % DEINT
        return v4_ref[lo, pl.ds(hi, 2 * BLK, stride=DEINT), :]

    def d16_store(b, o, lse):
        lo, hi = b // DEINT, b % DEINT
        o4_ref[2, lo, pl.ds(hi, BLK, stride=DEINT), :] = o
        lse4_ref[2, lo, pl.ds(hi, BLK, stride=DEINT), :] = lse

    for g, (qk, vload, store) in enumerate(((d1_qk, d1_v, d1_store), (d4_qk, d4_v, d4_store),
                                            (d16_qk, d16_v, d16_store))):
        def unit(u, carry, g=g, qk=qk, vload=vload, store=store):
            q, k, has_prev = qk(u)
            o, lse = _attend(q, k, vload(u), bias_ref[g, jnp.where(has_prev, 1, 0)], low)
            store(u, o, lse)
            return carry

        lax.fori_loop(0, ATT_BLOCKS, unit, 0, unroll=ATT_UNROLL)

    for r in range(DEINT):
        def mix(c, carry, r=r):
            a0 = pl.multiple_of(c * MIX_ROWS, MIX_ROWS)
            nat = pl.ds(DEINT * a0 + r, MIX_ROWS, stride=DEINT)
            a = [lse4_ref[g, r, pl.ds(a0, MIX_ROWS), :] for g in range(n_pat)]
            m = functools.reduce(jnp.maximum, a)
            e = [jnp.exp(x - m) for x in a]
            num = sum(e[g] * o4_ref[g, r, pl.ds(a0, MIX_ROWS), :] for g in range(n_pat))
            ynat_ref[nat, :] = num / sum(e) * _silu(g_ref[nat, :])
            return carry

        lax.fori_loop(0, PLANE // MIX_ROWS, mix, 0)

    y_ref[...] = ynat_ref[...].astype(y_ref.dtype)
    k4_ref[:, 0:PLANE, :] = k4_ref[:, PLANE:2 * PLANE, :]
    v4_ref[:, 0:PLANE, :] = v4_ref[:, PLANE:2 * PLANE, :]


def _attn(proj, rel_bias):
    s = proj.shape[0]
    assert s % ATT_TILE == 0
    lanes_per_col = BR // LANES
    col = lambda c: pl.BlockSpec((ATT_TILE, LANES), lambda p, t, c=c: (t, c * lanes_per_col + p))
    n_pat = len(DILATIONS)
    return pl.pallas_call(
        _attn_kernel,
        grid=(N_PAIRS, s // ATT_TILE),
        in_specs=[
            pl.BlockSpec(memory_space=pltpu.SMEM),
            pl.BlockSpec((n_pat, 2, BLK, 2 * BLK), lambda p, t: (0, 0, 0, 0)),
            col(COL_Q), col(COL_K), col(COL_V), col(COL_B_G),
        ],
        out_specs=pl.BlockSpec((ATT_TILE, LANES), lambda p, t: (t, p)),
        out_shape=jax.ShapeDtypeStruct((s, BR), jnp.bfloat16),
        scratch_shapes=[pltpu.VMEM((DEINT, PLANE, LANES), jnp.float32),
                        pltpu.VMEM((DEINT, 2 * PLANE, LANES), jnp.float32),
                        pltpu.VMEM((DEINT, 2 * PLANE, LANES), jnp.float32),
                        pltpu.VMEM((n_pat, DEINT, PLANE, LANES), jnp.float32),
                        pltpu.VMEM((n_pat, DEINT, PLANE, LANES), jnp.float32),
                        pltpu.VMEM((ATT_TILE, LANES), jnp.float32),
                        pltpu.VMEM((n_pat, 2, 2 * BLK, 2 * BLK), jnp.float32)],
        compiler_params=_params("arbitrary", "arbitrary"),
        name="dilated_attn",
    )(rel_bias, _t5_bucket_tables(), proj, proj, proj, proj)


def _rglru_tile(cx, cg, cw_ref, cb_ref, w_ref, b_ref, lam_ref, prev_ref, h_ref):
    tm = cx.shape[0]
    prev = prev_ref[...]
    xc = _shift_rows(cx, 3, prev) * cw_ref[0:1, :]
    xc = xc + _shift_rows(cx, 2, prev) * cw_ref[1:2, :]
    xc = xc + _shift_rows(cx, 1, prev) * cw_ref[2:3, :]
    xc = xc + cx * cw_ref[3:4, :]
    xc = xc + cb_ref[...]
    prev_ref[...] = cx[tm - SUBLANES:]

    z = jnp.dot(xc.astype(jnp.bfloat16), w_ref[...], preferred_element_type=jnp.float32) + b_ref[...]
    r = jax.nn.sigmoid(z[:, :BR])
    ig = jax.nn.sigmoid(z[:, BR:])
    neg_lam = -lam_ref[...]
    softplus = jnp.maximum(neg_lam, 0.0) + _log1p(jnp.exp(-jnp.abs(neg_lam)))
    log_a = -LRU_C * r * softplus
    a = jnp.exp(log_a)
    b = jnp.sqrt(-_expm1(2.0 * log_a)) * ig * xc

    d = 1
    while d < tm:
        b = b + a * _shift_rows_fill(b, d, 0.0)
        a = a * _shift_rows_fill(a, d, 1.0)
        d *= 2
    h = b + a * h_ref[...]
    h_ref[...] = h[tm - 1:tm]
    return (h * _silu(cg)).astype(jnp.bfloat16)


def _block_diag(w):
    hh, n, m = w.shape
    eye = jnp.eye(hh, dtype=w.dtype)
    return jnp.einsum('hij,hg->higj', w, eye).reshape(hh * n, hh * m)


N_STATE = S5_GROUPS * S5_STATE
S5_TM = 512
S5_CHUNKS = BR // LANES
S5_CHUNK_STATES = N_STATE // S5_CHUNKS
SLABS_PER_CHUNK = S5_CHUNK_STATES // LANES
N_SLABS = N_STATE // LANES
SLAB_PITCH = S5_TM + SUBLANES
S5_UNROLL = 8


def _s5_kernel(u_ref, g_ref, bmat_ref, ar_ref, ai_ref, cmat_ref, d_ref, wglu_ref, bglu_ref,
               y_ref, xs_ref, state_ref):
    @pl.when(pl.program_id(0) == 0)
    def _():
        state_ref[...] = jnp.zeros_like(state_ref)

    u = u_ref[...]
    tm = u.shape[0]
    ub = u.astype(jnp.bfloat16)
    for j in range(S5_CHUNKS):
        bu = jnp.dot(ub[:, j * LANES:(j + 1) * LANES], bmat_ref[j], preferred_element_type=jnp.float32)
        for part in range(2):
            for cc in range(SLABS_PER_CHUNK):
                slab = part * N_SLABS + j * SLABS_PER_CHUNK + cc
                col = part * S5_CHUNK_STATES + cc * LANES
                xs_ref[pl.ds(slab * SLAB_PITCH, tm), :] = bu[:, col:col + LANES]

    n_vreg = N_SLABS // SUBLANES
    a_re = [ar_ref[v * SUBLANES:(v + 1) * SUBLANES, :] for v in range(n_vreg)]
    a_im = [ai_ref[v * SUBLANES:(v + 1) * SUBLANES, :] for v in range(n_vreg)]

    def rows(part, v, t):
        return pl.ds((part * N_SLABS + v * SUBLANES) * SLAB_PITCH + t, SUBLANES, stride=SLAB_PITCH)

    def step(t, x):
        new = []
        for v in range(n_vreg):
            xr, xi = x[2 * v], x[2 * v + 1]
            nr = (a_re[v] * xr + xs_ref[rows(0, v, t), :]) - a_im[v] * xi
            ni = (a_re[v] * xi + xs_ref[rows(1, v, t), :]) + a_im[v] * xr
            xs_ref[rows(0, v, t), :] = nr
            xs_ref[rows(1, v, t), :] = ni
            new += [nr, ni]
        return tuple(new)

    x0 = tuple(state_ref[i] for i in range(2 * n_vreg))
    x1 = lax.fori_loop(0, tm, step, x0, unroll=S5_UNROLL)
    for i in range(2 * n_vreg):
        state_ref[i] = x1[i]

    ys = []
    for j in range(S5_CHUNKS):
        xcat = jnp.concatenate(
            [xs_ref[pl.ds((part * N_SLABS + j * SLABS_PER_CHUNK + cc) * SLAB_PITCH, tm), :]
             for part in range(2) for cc in range(SLABS_PER_CHUNK)], axis=1)
        ys.append(jnp.dot(xcat.astype(jnp.bfloat16), cmat_ref[j], preferred_element_type=jnp.float32))
    y = jnp.concatenate(ys, axis=1) + d_ref[...] * u
    y = 0.5 * y * (1.0 + jnp.tanh(math.sqrt(2.0 / math.pi) * (y + 0.044715 * (y * y * y))))
    gate = jnp.dot(y.astype(jnp.bfloat16), wglu_ref[...], preferred_element_type=jnp.float32) + bglu_ref[...]
    y = y * jax.nn.sigmoid(gate)
    y_ref[...] = (y * _silu(g_ref[...])).astype(y_ref.dtype)


def _chunk_block_diag(w):
    per = w.shape[0] // S5_CHUNKS
    return jnp.stack([_block_diag(w[j * per:(j + 1) * per]) for j in range(S5_CHUNKS)])


def _s5_discretize(lam_re, lam_im, log_dt, b_re, b_im, c_re, c_im):
    dt = jnp.exp(log_dt)[:, None]
    mag = jnp.exp(lam_re * dt)
    ab_re = mag * jnp.cos(lam_im * dt)
    ab_im = mag * jnp.sin(lam_im * dt)
    den = lam_re * lam_re + lam_im * lam_im
    f_re = ((ab_re - 1.0) * lam_re + ab_im * lam_im) / den
    f_im = (ab_im * lam_re - (ab_re - 1.0) * lam_im) / den
    bb_re = f_re[..., None] * b_re - f_im[..., None] * b_im
    bb_im = f_re[..., None] * b_im + f_im[..., None] * b_re
    bmat = jnp.concatenate([_chunk_block_diag(jnp.swapaxes(bb_re, 1, 2)),
                            _chunk_block_diag(jnp.swapaxes(bb_im, 1, 2))], axis=2)
    cmat = jnp.concatenate([_chunk_block_diag(jnp.swapaxes(c_re, 1, 2)),
                            -_chunk_block_diag(jnp.swapaxes(c_im, 1, 2))], axis=1)
    return (bmat.astype(jnp.bfloat16), cmat.astype(jnp.bfloat16),
            ab_re.reshape(N_SLABS, LANES), ab_im.reshape(N_SLABS, LANES))


def _s5(proj, bmat, a_re, a_im, cmat, s5_d, w_glu_bf16, b_glu, layer):
    s = proj.shape[0]
    tm = S5_TM
    const = lambda a: pl.BlockSpec(a.shape, lambda i, nd=a.ndim: (0,) * nd)
    vec = lambda a: pl.BlockSpec((None,) + a.shape[1:], lambda i, nd=a.ndim: (layer,) + (0,) * (nd - 1))
    s5_d = s5_d.reshape(DEPTH, 1, BR)
    b_glu = b_glu.reshape(DEPTH, 1, BR)
    return pl.pallas_call(
        _s5_kernel,
        grid=(s // tm,),
        in_specs=[pl.BlockSpec((tm, BR), lambda i: (i, COL_D_U)),
                  pl.BlockSpec((tm, BR), lambda i: (i, COL_D_G)),
                  const(bmat), const(a_re), const(a_im), const(cmat),
                  vec(s5_d), vec(w_glu_bf16), vec(b_glu)],
        out_specs=pl.BlockSpec((tm, BR), lambda i: (i, 0)),
        out_shape=jax.ShapeDtypeStruct((s, BR), jnp.bfloat16),
        scratch_shapes=[pltpu.VMEM((2 * N_SLABS * SLAB_PITCH, LANES), jnp.float32),
                        pltpu.VMEM((2 * N_SLABS // SUBLANES, SUBLANES, LANES), jnp.float32)],
        compiler_params=_params("arbitrary"),
        name="s5",
    )(proj, proj, bmat, a_re, a_im, cmat, s5_d, w_glu_bf16, b_glu)


OUT_TM = 256


def _gated_conv_tile(ab, ac, ax, ag, cw_ref, prev_ref):
    u = ac * ax
    prev = prev_ref[...]
    conv = _shift_rows(u, 2, prev) * cw_ref[0:1, :]
    conv = conv + _shift_rows(u, 1, prev) * cw_ref[1:2, :]
    conv = conv + u * cw_ref[2:3, :]
    prev_ref[...] = u[u.shape[0] - SUBLANES:]
    return (ab * conv * _silu(ag)).astype(jnp.bfloat16)


def _out_kernel(x_ref, ab_ref, ac_ref, ax_ref, ag_ref, caw_ref,
                cx_ref, cg_ref, ccw_ref, ccb_ref, lw_ref, lb_ref, lam_ref,
                yb_ref, yd_ref, w_ref, gate_ref, g_ref, b_ref, o_ref,
                prev_a_ref, prev_c_ref, h_ref):
    @pl.when(pl.program_id(0) == 0)
    def _():
        prev_a_ref[...] = jnp.zeros_like(prev_a_ref)
        prev_c_ref[...] = jnp.zeros_like(prev_c_ref)
        h_ref[...] = jnp.zeros_like(h_ref)

    ya = _gated_conv_tile(ab_ref[...], ac_ref[...], ax_ref[...], ag_ref[...], caw_ref, prev_a_ref)
    yc = _rglru_tile(cx_ref[...], cg_ref[...], ccw_ref, ccb_ref, lw_ref, lb_ref, lam_ref, prev_c_ref, h_ref)

    y = jnp.dot(ya, w_ref[0 * BR:1 * BR, :], preferred_element_type=jnp.float32)
    y = y + jnp.dot(yb_ref[...], w_ref[1 * BR:2 * BR, :], preferred_element_type=jnp.float32)
    y = y + jnp.dot(yc, w_ref[2 * BR:3 * BR, :], preferred_element_type=jnp.float32)
    y = y + jnp.dot(yd_ref[...], w_ref[3 * BR:4 * BR, :], preferred_element_type=jnp.float32)
    z = ALPHA * x_ref[...] + (1.0 + gate_ref[...]) * y
    mu = jnp.mean(z, axis=-1, keepdims=True)
    zc = z - mu
    var = jnp.mean(zc * zc, axis=-1, keepdims=True)
    o_ref[...] = zc * lax.rsqrt(var + LN_EPS) * g_ref[...] + b_ref[...]


def _out(x, proj, conv_a, conv_c, conv_c_b, lru_w, lru_b, lru_lambda, yb, yd, w_out_bf16, ada, ln_g, ln_b, layer):
    s, d = x.shape
    tm = OUT_TM
    col = lambda c: pl.BlockSpec((tm, BR), lambda i, c=c: (i, c))
    branch = pl.BlockSpec((tm, BR), lambda i: (i, 0))
    vec = lambda a: pl.BlockSpec((None,) + a.shape[1:], lambda i, nd=a.ndim: (layer,) + (0,) * (nd - 1))
    conv_c_b = conv_c_b.reshape(DEPTH, 1, BR)
    lam = lru_lambda.reshape(DEPTH, 1, BR)
    ln_g = ln_g.reshape(DEPTH, 1, d)
    ln_b = ln_b.reshape(DEPTH, 1, d)
    return pl.pallas_call(
        _out_kernel,
        grid=(s // tm,),
        in_specs=[pl.BlockSpec((tm, d), lambda i: (i, 0)),
                  col(COL_A_B), col(COL_A_C), col(COL_A_X), col(COL_A_G), vec(conv_a),
                  col(COL_C_X), col(COL_C_G), vec(conv_c), vec(conv_c_b), vec(lru_w), vec(lru_b), vec(lam),
                  branch, branch,
                  pl.BlockSpec((None, 4 * BR, d), lambda i: (layer, 0, 0), pipeline_mode=pl.Buffered(1)),
                  pl.BlockSpec((1, d), lambda i: (0, 2)),
                  vec(ln_g), vec(ln_b)],
        out_specs=pl.BlockSpec((tm, d), lambda i: (i, 0)),
        out_shape=jax.ShapeDtypeStruct((s, d), jnp.float32),
        scratch_shapes=[pltpu.VMEM((SUBLANES, BR), jnp.float32), pltpu.VMEM((SUBLANES, BR), jnp.float32),
                        pltpu.VMEM((1, BR), jnp.float32)],
        compiler_params=_params("arbitrary"),
        name="out_proj_ln",
    )(x, proj, proj, proj, proj, conv_a, proj, proj, conv_c, conv_c_b, lru_w, lru_b, lam,
      yb, yd, w_out_bf16, ada, ln_g, ln_b)


def kernel(x, c, rel_bias, w_ada, b_ada, w_in, conv_a, conv_c, conv_c_b, lru_wa, lru_ba, lru_wx, lru_bx, lru_lambda, s5_lam_re, s5_lam_im, s5_log_dt, s5_b_re, s5_b_im, s5_c_re, s5_c_im, s5_d, s5_w_glu, s5_b_glu, w_out, ln_g, ln_b):
    bsz, s, d = x.shape
    assert bsz == 1 and w_in.shape == (DEPTH, d, N_IN)
    xs = x.reshape(s, d)
    c_col = c.reshape(d, 1)
    b_ada3 = b_ada.reshape(DEPTH, 1, 3 * d)
    w_out_bf16 = w_out.astype(jnp.bfloat16)
    w_glu_bf16 = s5_w_glu.astype(jnp.bfloat16)
    lru_w = jnp.stack([jnp.concatenate([_block_diag(lru_wa[l]), _block_diag(lru_wx[l])], axis=1)
                       for l in range(DEPTH)]).astype(jnp.bfloat16)
    lru_b = jnp.concatenate([lru_ba, lru_bx], axis=1).reshape(DEPTH, 1, 2 * BR)

    for l in range(DEPTH):
        ada = _ada(c_col, w_ada, b_ada3, l)
        proj = _proj(xs, ada, w_in, l)
        yb = _attn(proj, rel_bias)
        bmat, cmat, a_re, a_im = _s5_discretize(
            s5_lam_re[l], s5_lam_im[l], s5_log_dt[l], s5_b_re[l], s5_b_im[l], s5_c_re[l], s5_c_im[l])
        yd = _s5(proj, bmat, a_re, a_im, cmat, s5_d, w_glu_bf16, s5_b_glu, l)
        xs = _out(xs, proj, conv_a, conv_c, conv_c_b, lru_w, lru_b, lru_lambda, yb, yd, w_out_bf16, ada,
                  ln_g, ln_b, l)
    return xs.reshape(bsz, s, d)
```

```python
import functools
import math

import numpy as np
import jax
import jax.numpy as jnp
from jax import lax
from jax.experimental import pallas as pl
from jax.experimental.pallas import tpu as pltpu

BR = 512
N_IN = 12 * BR
ATT_HEADS = 8
ATT_HEAD_DIM = 64
BLK = 128
SPAN = 128
DILATIONS = (1, 4, 16)
REL_BUCKETS = 32
REL_MAX_DIST = 2048
LRU_HEADS = 8
LRU_C = 8.0
S5_CH = 16
S5_GROUPS = 32
S5_STATE = 64
DEPTH = 2
ALPHA = (2 * DEPTH) ** 0.25
LN_EPS = 1e-5

SUBLANES = 8
LANES = 128
VMEM_LIMIT = 56 * 1024 * 1024

COL_A_B, COL_A_C, COL_A_X, COL_A_G = 0, 1, 2, 3
COL_Q, COL_K, COL_V, COL_B_G = 4, 5, 6, 7
COL_C_X, COL_C_G = 8, 9
COL_D_U, COL_D_G = 10, 11


def _silu(x):
    return x * _sigmoid(x)


_sigmoid = jax.nn.sigmoid


def _log1p(x):
    w = 1.0 + x
    return jnp.where(w == 1.0, x, x * jnp.log(w) / (w - 1.0))


def _params(*sem):
    return pltpu.CompilerParams(dimension_semantics=sem, vmem_limit_bytes=VMEM_LIMIT)


def _delayed(x, prev_ref, max_delay):
    assert 0 < max_delay < SUBLANES
    prev = prev_ref[...]
    row = lax.broadcasted_iota(jnp.int32, prev.shape, 0)
    taps = []
    for d in range(1, max_delay + 1):
        rolled = pltpu.roll(x, d, 0)
        top = jnp.where(row < d, pltpu.roll(prev, d, 0), rolled[:SUBLANES])
        taps.append(jnp.concatenate([top, rolled[SUBLANES:]], axis=0))
    prev_ref[...] = x[x.shape[0] - SUBLANES:]
    return taps


def _shift_rows_fill(x, d, fill, period):
    assert 0 < d < period <= SUBLANES
    rolled = pltpu.roll(x, d, 0)
    row = lax.broadcasted_iota(jnp.int32, x.shape, 0) & (period - 1)
    return jnp.where(row < d, jnp.asarray(fill, x.dtype), rolled)


def _ada_kernel(c_ref, w_ref, b_ref, o_ref):
    cond = _silu(c_ref[...])
    o_ref[...] = jnp.sum(cond * w_ref[...], axis=0, keepdims=True) + b_ref[...]


def _ada(c_col, w_ada, b_ada, layer):
    d = c_col.shape[0]
    n = w_ada.shape[2]
    tn = 512
    return pl.pallas_call(
        _ada_kernel,
        grid=(n // tn,),
        in_specs=[
            pl.BlockSpec((d, 1), lambda j: (0, 0)),
            pl.BlockSpec((None, d, tn), lambda j: (layer, 0, j)),
            pl.BlockSpec((None, 1, tn), lambda j: (layer, 0, j)),
        ],
        out_specs=pl.BlockSpec((1, tn), lambda j: (0, j)),
        out_shape=jax.ShapeDtypeStruct((1, n), jnp.float32),
        compiler_params=_params("arbitrary"),
        name="ada",
    )(c_col, w_ada, b_ada)


def _proj_kernel(x_ref, shift_ref, scale_ref, w_ref, o_ref, h_ref):
    @pl.when(pl.program_id(1) == 0)
    def _():
        h = x_ref[...] * (1.0 + scale_ref[...]) + shift_ref[...]
        h_ref[...] = h.astype(jnp.bfloat16)

    o_ref[...] = jnp.dot(h_ref[...], w_ref[...].astype(jnp.bfloat16), preferred_element_type=jnp.float32)


def _proj(x, ada, w_in, layer):
    s, d = x.shape
    n = w_in.shape[2]
    tm, tn = 1024, 1024
    return pl.pallas_call(
        _proj_kernel,
        grid=(s // tm, n // tn),
        in_specs=[
            pl.BlockSpec((tm, d), lambda i, j: (i, 0)),
            pl.BlockSpec((1, d), lambda i, j: (0, 0)),
            pl.BlockSpec((1, d), lambda i, j: (0, 1)),
            pl.BlockSpec((None, d, tn), lambda i, j: (layer, 0, j)),
        ],
        out_specs=pl.BlockSpec((tm, tn), lambda i, j: (i, j)),
        out_shape=jax.ShapeDtypeStruct((s, n), jnp.float32),
        scratch_shapes=[pltpu.VMEM((tm, d), jnp.bfloat16)],
        compiler_params=_params("arbitrary", "arbitrary"),
        name="proj",
    )(x, ada, ada, w_in)


ATT_TILE = max(DILATIONS) * BLK
ATT_BLOCKS = ATT_TILE // BLK
N_PAIRS = ATT_HEADS // 2
DEINT = 4
PLANE = ATT_TILE // DEINT
QROWS = BLK // DEINT
MIX_ROWS = 256
ATT_UNROLL = 16
MASKED = -1e30


def _t5_bucket_tables():
    assert DILATIONS == (1, 4, 16) and DEINT == 4
    i = np.arange(BLK)[:, None]
    j = np.arange(2 * BLK)[None, :]
    delta = i + BLK - j
    valid = (delta >= 0) & (delta <= SPAN)
    max_exact = REL_BUCKETS // 2
    tables = []
    for dil in DILATIONS:
        dist = np.clip(delta, 0, SPAN) * dil
        nf = np.maximum(dist, 1).astype(np.float32)
        large = max_exact + (np.log(nf / np.float32(max_exact)) / np.float32(math.log(REL_MAX_DIST / max_exact))
                             * np.float32(REL_BUCKETS - max_exact)).astype(np.int32)
        bucket = np.where(dist < max_exact, dist, np.minimum(large, REL_BUCKETS - 1))
        table = np.stack([np.where(valid & (j >= BLK), bucket, -1), np.where(valid, bucket, -1)])
        if dil == 1:
            rows = np.array([DEINT * a + r for r in range(DEINT) for a in range(QROWS)])
            cols = np.array([blk * BLK + DEINT * a + r
                             for r in range(DEINT) for blk in range(2) for a in range(QROWS)])
            table = table[:, rows][:, :, cols]
        tables.append(table)
    return jnp.asarray(np.stack(tables), jnp.int32)


def _attend(q, k, v, bias, low):
    q = (q * (ATT_HEAD_DIM ** -0.5)).astype(jnp.bfloat16)
    zero = jnp.zeros_like(q)
    q2 = jnp.concatenate([jnp.where(low, q, zero), jnp.where(low, zero, q)], axis=0)
    sc = lax.dot_general(q2, k.astype(jnp.bfloat16), (((1,), (1,)), ((), ())),
                         preferred_element_type=jnp.float32) + bias
    m = jnp.max(sc, axis=-1, keepdims=True)
    p = jnp.exp(sc - m)
    l = jnp.sum(p, axis=-1, keepdims=True)
    pv = jnp.dot(p.astype(jnp.bfloat16), v.astype(jnp.bfloat16), preferred_element_type=jnp.float32)
    o = jnp.where(low, pv[:BLK], pv[BLK:])
    l2 = jnp.where(low, l[:BLK], l[BLK:])
    m2 = jnp.where(low, m[:BLK], m[BLK:])
    return o / l2, m2 + jnp.log(l2)


def _attn_kernel(rb_ref, bucket_ref, q_ref, k_ref, v_ref, g_ref, y_ref,
                 q4_ref, k4_ref, v4_ref, o4_ref, lse4_ref, ynat_ref, bias_ref):
    pair = pl.program_id(0)
    t = pl.program_id(1)
    lane = lax.broadcasted_iota(jnp.int32, (BLK, LANES), 1)
    low = lane < ATT_HEAD_DIM
    n_pat = len(DILATIONS)

    @pl.when(t == 0)
    def _():
        k4_ref[:, 0:PLANE, :] = jnp.zeros((DEINT, PLANE, LANES), jnp.float32)
        v4_ref[:, 0:PLANE, :] = jnp.zeros((DEINT, PLANE, LANES), jnp.float32)
        for g in range(n_pat):
            bucket = bucket_ref[g, 1]
            no_prev = bucket_ref[g, 0] < 0
            hits = [bucket == b for b in range(REL_BUCKETS)]
            for half in range(2):
                acc = jnp.full(bucket.shape, MASKED, jnp.float32)
                for b in range(REL_BUCKETS):
                    acc = jnp.where(hits[b], rb_ref[b, 2 * pair + half], acc)
                bias_ref[g, 1, half * BLK:(half + 1) * BLK, :] = acc
                bias_ref[g, 0, half * BLK:(half + 1) * BLK, :] = jnp.where(no_prev, MASKED, acc)

    for r in range(DEINT):
        q4_ref[r] = q_ref[pl.ds(r, PLANE, stride=DEINT), :]
        k4_ref[r, PLANE:2 * PLANE, :] = k_ref[pl.ds(r, PLANE, stride=DEINT), :]
        v4_ref[r, PLANE:2 * PLANE, :] = v_ref[pl.ds(r, PLANE, stride=DEINT), :]

    def d1_qk(n):
        a0 = pl.multiple_of(n * QROWS, QROWS)
        q = jnp.concatenate([q4_ref[r, pl.ds(a0, QROWS), :] for r in range(DEINT)], axis=0)
        k = jnp.concatenate([k4_ref[r, pl.ds(PLANE - QROWS + a0, 2 * QROWS), :] for r in range(DEINT)], axis=0)
        return q, k, (t > 0) | (n > 0)

    def d1_v(n):
        a0 = pl.multiple_of(n * QROWS, QROWS)
        return jnp.concatenate([v4_ref[r, pl.ds(PLANE - QROWS + a0, 2 * QROWS), :] for r in range(DEINT)], axis=0)

    def d1_store(n, o, lse):
        a0 = pl.multiple_of(n * QROWS, QROWS)
        for r in range(DEINT):
            o4_ref[0, r, pl.ds(a0, QROWS), :] = o[r * QROWS:(r + 1) * QROWS]
            lse4_ref[0, r, pl.ds(a0, QROWS), :] = lse[r * QROWS:(r + 1) * QROWS]

    def d4_qk(b):
        r, n = b // DEINT, b % DEINT
        a0 = pl.multiple_of(n * BLK, BLK)
        return (q4_ref[r, pl.ds(a0, BLK), :], k4_ref[r, pl.ds(PLANE - BLK + a0, 2 * BLK), :],
                (t > 0) | (n > 0))

    def d4_v(b):
        r, n = b // DEINT, b % DEINT
        return v4_ref[r, pl.ds(PLANE - BLK + pl.multiple_of(n * BLK, BLK), 2 * BLK), :]

    def d4_store(b, o, lse):
        r, n = b // DEINT, b % DEINT
        a0 = pl.multiple_of(n * BLK, BLK)
        o4_ref[1, r, pl.ds(a0, BLK), :] = o
        lse4_ref[1, r, pl.ds(a0, BLK), :] = lse

    def d16_qk(b):
        lo, hi = b // DEINT, b % DEINT
        return (q4_ref[lo, pl.ds(hi, BLK, stride=DEINT), :],
                k4_ref[lo, pl.ds(hi, 2 * BLK, stride=DEINT), :], t > 0)

    def d16_v(b):
        lo, hi = b // DEINT, b % DEINT
        return v4_ref[lo, pl.ds(hi, 2 * BLK, stride=DEINT), :]

    def d16_store(b, o, lse):
        lo, hi = b // DEINT, b % DEINT
        o4_ref[2, lo, pl.ds(hi, BLK, stride=DEINT), :] = o
        lse4_ref[2, lo, pl.ds(hi, BLK, stride=DEINT), :] = lse

    for g, (qk, vload, store) in enumerate(((d1_qk, d1_v, d1_store), (d4_qk, d4_v, d4_store),
                                            (d16_qk, d16_v, d16_store))):
        def unit(u, carry, g=g, qk=qk, vload=vload, store=store):
            q, k, has_prev = qk(u)
            o, lse = _attend(q, k, vload(u), bias_ref[g, jnp.where(has_prev, 1, 0)], low)
            store(u, o, lse)
            return carry

        lax.fori_loop(0, ATT_BLOCKS, unit, 0, unroll=ATT_UNROLL)

    for r in range(DEINT):
        def mix(c, carry, r=r):
            a0 = pl.multiple_of(c * MIX_ROWS, MIX_ROWS)
            nat = pl.ds(DEINT * a0 + r, MIX_ROWS, stride=DEINT)
            a = [lse4_ref[g, r, pl.ds(a0, MIX_ROWS), :] for g in range(n_pat)]
            m = functools.reduce(jnp.maximum, a)
            e = [jnp.exp(x - m) for x in a]
            num = sum(e[g] * o4_ref[g, r, pl.ds(a0, MIX_ROWS), :] for g in range(n_pat))
            ynat_ref[nat, :] = num / sum(e) * _silu(g_ref[nat, :])
            return carry

        lax.fori_loop(0, PLANE // MIX_ROWS, mix, 0)

    y_ref[...] = ynat_ref[...].astype(y_ref.dtype)
    k4_ref[:, 0:PLANE, :] = k4_ref[:, PLANE:2 * PLANE, :]
    v4_ref[:, 0:PLANE, :] = v4_ref[:, PLANE:2 * PLANE, :]


def _attn(proj, rel_bias):
    s = proj.shape[0]
    assert s % ATT_TILE == 0
    lanes_per_col = BR // LANES
    col = lambda c: pl.BlockSpec((ATT_TILE, LANES), lambda p, t, c=c: (t, c * lanes_per_col + p))
    n_pat = len(DILATIONS)
    return pl.pallas_call(
        _attn_kernel,
        grid=(N_PAIRS, s // ATT_TILE),
        in_specs=[
            pl.BlockSpec(memory_space=pltpu.SMEM),
            pl.BlockSpec((n_pat, 2, BLK, 2 * BLK), lambda p, t: (0, 0, 0, 0)),
            col(COL_Q), col(COL_K), col(COL_V), col(COL_B_G),
        ],
        out_specs=pl.BlockSpec((ATT_TILE, LANES), lambda p, t: (t, p)),
        out_shape=jax.ShapeDtypeStruct((s, BR), jnp.bfloat16),
        scratch_shapes=[pltpu.VMEM((DEINT, PLANE, LANES), jnp.float32),
                        pltpu.VMEM((DEINT, 2 * PLANE, LANES), jnp.float32),
                        pltpu.VMEM((DEINT, 2 * PLANE, LANES), jnp.float32),
                        pltpu.VMEM((n_pat, DEINT, PLANE, LANES), jnp.float32),
                        pltpu.VMEM((n_pat, DEINT, PLANE, LANES), jnp.float32),
                        pltpu.VMEM((ATT_TILE, LANES), jnp.float32),
                        pltpu.VMEM((n_pat, 2, 2 * BLK, 2 * BLK), jnp.float32)],
        compiler_params=_params("arbitrary", "arbitrary"),
        name="dilated_attn",
    )(rel_bias, _t5_bucket_tables(), proj, proj, proj, proj)


def _rglru_tile(cx, cg, cw_ref, cb_ref, w_ref, b_ref, lam_ref, hist_ref, h_ref):
    tm = cx.shape[0]
    d1, d2, d3 = _delayed(cx, hist_ref, 3)
    xc = d3 * cw_ref[0:1, :]
    xc = xc + d2 * cw_ref[1:2, :]
    xc = xc + d1 * cw_ref[2:3, :]
    xc = xc + cx * cw_ref[3:4, :]
    xc = xc + cb_ref[...]

    z = jnp.dot(xc.astype(jnp.bfloat16), w_ref[...], preferred_element_type=jnp.float32) + b_ref[...]
    r = _sigmoid(z[:, :BR])
    ig = _sigmoid(z[:, BR:])
    neg_lam = -lam_ref[...]
    softplus = jnp.maximum(neg_lam, 0.0) + _log1p(jnp.exp(-jnp.abs(neg_lam)))
    log_a = -LRU_C * r * softplus
    a = jnp.exp(log_a)
    b = jnp.sqrt(jnp.tanh(-log_a) * (a * a + 1.0)) * ig * xc

    d = 1
    while d < SUBLANES:
        b = b + a * _shift_rows_fill(b, d, 0.0, period=SUBLANES)
        a = a * _shift_rows_fill(a, d, 1.0, period=SUBLANES)
        d *= 2
    carry = h_ref[...]
    groups = []
    for g in range(tm // SUBLANES):
        rows = slice(g * SUBLANES, (g + 1) * SUBLANES)
        hg = b[rows] + a[rows] * carry
        carry = hg[SUBLANES - 1:]
        groups.append(hg)
    h_ref[...] = carry
    h = jnp.concatenate(groups, axis=0)
    return (h * _silu(cg)).astype(jnp.bfloat16)


def _block_diag(w):
    hh, n, m = w.shape
    eye = jnp.eye(hh, dtype=w.dtype)
    return jnp.einsum('hij,hg->higj', w, eye).reshape(hh * n, hh * m)


N_STATE = S5_GROUPS * S5_STATE
S5_TM = 512
S5_CHUNKS = BR // LANES
S5_CHUNK_STATES = N_STATE // S5_CHUNKS
SLABS_PER_CHUNK = S5_CHUNK_STATES // LANES
N_SLABS = N_STATE // LANES
SLAB_PITCH = S5_TM + SUBLANES
S5_UNROLL = 8


def _s5_kernel(u_ref, g_ref, bmat_ref, ar_ref, ai_ref, cmat_ref, d_ref, wglu_ref, bglu_ref,
               y_ref, xs_ref, state_ref):
    @pl.when(pl.program_id(0) == 0)
    def _():
        state_ref[...] = jnp.zeros_like(state_ref)

    u = u_ref[...]
    tm = u.shape[0]
    ub = u.astype(jnp.bfloat16)
    for j in range(S5_CHUNKS):
        bu = jnp.dot(ub[:, j * LANES:(j + 1) * LANES], bmat_ref[j], preferred_element_type=jnp.float32)
        for part in range(2):
            for cc in range(SLABS_PER_CHUNK):
                slab = part * N_SLABS + j * SLABS_PER_CHUNK + cc
                col = part * S5_CHUNK_STATES + cc * LANES
                xs_ref[pl.ds(slab * SLAB_PITCH, tm), :] = bu[:, col:col + LANES]

    n_vreg = N_SLABS // SUBLANES
    a_re = [ar_ref[v * SUBLANES:(v + 1) * SUBLANES, :] for v in range(n_vreg)]
    a_im = [ai_ref[v * SUBLANES:(v + 1) * SUBLANES, :] for v in range(n_vreg)]

    def rows(part, v, t):
        return pl.ds((part * N_SLABS + v * SUBLANES) * SLAB_PITCH + t, SUBLANES, stride=SLAB_PITCH)

    def step(t, x):
        new = []
        for v in range(n_vreg):
            xr, xi = x[2 * v], x[2 * v + 1]
            nr = (a_re[v] * xr + xs_ref[rows(0, v, t), :]) - a_im[v] * xi
            ni = (a_re[v] * xi + xs_ref[rows(1, v, t), :]) + a_im[v] * xr
            xs_ref[rows(0, v, t), :] = nr
            xs_ref[rows(1, v, t), :] = ni
            new += [nr, ni]
        return tuple(new)

    x0 = tuple(state_ref[i] for i in range(2 * n_vreg))
    x1 = lax.fori_loop(0, tm, step, x0, unroll=S5_UNROLL)
    for i in range(2 * n_vreg):
        state_ref[i] = x1[i]

    ys = []
    for j in range(S5_CHUNKS):
        xcat = jnp.concatenate(
            [xs_ref[pl.ds((part * N_SLABS + j * SLABS_PER_CHUNK + cc) * SLAB_PITCH, tm), :]
             for part in range(2) for cc in range(SLABS_PER_CHUNK)], axis=1)
        ys.append(jnp.dot(xcat.astype(jnp.bfloat16), cmat_ref[j], preferred_element_type=jnp.float32))
    y = jnp.concatenate(ys, axis=1) + d_ref[...] * u
    y = 0.5 * y * (1.0 + jnp.tanh(math.sqrt(2.0 / math.pi) * (y + 0.044715 * (y * y * y))))
    gate = jnp.dot(y.astype(jnp.bfloat16), wglu_ref[...], preferred_element_type=jnp.float32) + bglu_ref[...]
    y = y * _sigmoid(gate)
    y_ref[...] = (y * _silu(g_ref[...])).astype(y_ref.dtype)


def _chunk_block_diag(w):
    per = w.shape[0] // S5_CHUNKS
    return jnp.stack([_block_diag(w[j * per:(j + 1) * per]) for j in range(S5_CHUNKS)])


def _s5_discretize(lam_re, lam_im, log_dt, b_re, b_im, c_re, c_im):
    dt = jnp.exp(log_dt)[:, None]
    mag = jnp.exp(lam_re * dt)
    ab_re = mag * jnp.cos(lam_im * dt)
    ab_im = mag * jnp.sin(lam_im * dt)
    den = lam_re * lam_re + lam_im * lam_im
    f_re = ((ab_re - 1.0) * lam_re + ab_im * lam_im) / den
    f_im = (ab_im * lam_re - (ab_re - 1.0) * lam_im) / den
    bb_re = f_re[..., None] * b_re - f_im[..., None] * b_im
    bb_im = f_re[..., None] * b_im + f_im[..., None] * b_re
    bmat = jnp.concatenate([_chunk_block_diag(jnp.swapaxes(bb_re, 1, 2)),
                            _chunk_block_diag(jnp.swapaxes(bb_im, 1, 2))], axis=2)
    cmat = jnp.concatenate([_chunk_block_diag(jnp.swapaxes(c_re, 1, 2)),
                            -_chunk_block_diag(jnp.swapaxes(c_im, 1, 2))], axis=1)
    return (bmat.astype(jnp.bfloat16), cmat.astype(jnp.bfloat16),
            ab_re.reshape(N_SLABS, LANES), ab_im.reshape(N_SLABS, LANES))


def _s5(proj, bmat, a_re, a_im, cmat, s5_d, w_glu_bf16, b_glu, layer):
    s = proj.shape[0]
    tm = S5_TM
    const = lambda a: pl.BlockSpec(a.shape, lambda i, nd=a.ndim: (0,) * nd)
    vec = lambda a: pl.BlockSpec((None,) + a.shape[1:], lambda i, nd=a.ndim: (layer,) + (0,) * (nd - 1))
    s5_d = s5_d.reshape(DEPTH, 1, BR)
    b_glu = b_glu.reshape(DEPTH, 1, BR)
    return pl.pallas_call(
        _s5_kernel,
        grid=(s // tm,),
        in_specs=[pl.BlockSpec((tm, BR), lambda i: (i, COL_D_U)),
                  pl.BlockSpec((tm, BR), lambda i: (i, COL_D_G)),
                  const(bmat), const(a_re), const(a_im), const(cmat),
                  vec(s5_d), vec(w_glu_bf16), vec(b_glu)],
        out_specs=pl.BlockSpec((tm, BR), lambda i: (i, 0)),
        out_shape=jax.ShapeDtypeStruct((s, BR), jnp.bfloat16),
        scratch_shapes=[pltpu.VMEM((2 * N_SLABS * SLAB_PITCH, LANES), jnp.float32),
                        pltpu.VMEM((2 * N_SLABS // SUBLANES, SUBLANES, LANES), jnp.float32)],
        compiler_params=_params("arbitrary"),
        name="s5",
    )(proj, proj, bmat, a_re, a_im, cmat, s5_d, w_glu_bf16, b_glu)


OUT_TM = 512
OUT_SUB = 256


def _gated_conv_tile(ab, ac, ax, ag, cw_ref, hist_ref):
    u = ac * ax
    d1, d2 = _delayed(u, hist_ref, 2)
    conv = d2 * cw_ref[0:1, :]
    conv = conv + d1 * cw_ref[1:2, :]
    conv = conv + u * cw_ref[2:3, :]
    return (ab * conv * _silu(ag)).astype(jnp.bfloat16)


def _out_kernel(x_ref, ab_ref, ac_ref, ax_ref, ag_ref, caw_ref,
                cx_ref, cg_ref, ccw_ref, ccb_ref, lw_ref, lb_ref, lam_ref,
                yb_ref, yd_ref, w_ref, gate_ref, g_ref, b_ref, o_ref,
                hist_a_ref, hist_c_ref, h_ref):
    @pl.when(pl.program_id(0) == 0)
    def _():
        hist_a_ref[...] = jnp.zeros_like(hist_a_ref)
        hist_c_ref[...] = jnp.zeros_like(hist_c_ref)
        h_ref[...] = jnp.zeros_like(h_ref)

    for sub in range(x_ref.shape[0] // OUT_SUB):
        rows = slice(sub * OUT_SUB, (sub + 1) * OUT_SUB)
        ya = _gated_conv_tile(ab_ref[rows, :], ac_ref[rows, :], ax_ref[rows, :], ag_ref[rows, :],
                              caw_ref, hist_a_ref)
        yc = _rglru_tile(cx_ref[rows, :], cg_ref[rows, :], ccw_ref, ccb_ref, lw_ref, lb_ref, lam_ref,
                         hist_c_ref, h_ref)
        y = jnp.dot(ya, w_ref[0 * BR:1 * BR, :], preferred_element_type=jnp.float32)
        y = y + jnp.dot(yb_ref[rows, :], w_ref[1 * BR:2 * BR, :], preferred_element_type=jnp.float32)
        y = y + jnp.dot(yc, w_ref[2 * BR:3 * BR, :], preferred_element_type=jnp.float32)
        y = y + jnp.dot(yd_ref[rows, :], w_ref[3 * BR:4 * BR, :], preferred_element_type=jnp.float32)
        z = ALPHA * x_ref[rows, :] + (1.0 + gate_ref[...]) * y
        mu = jnp.mean(z, axis=-1, keepdims=True)
        zc = z - mu
        var = jnp.mean(zc * zc, axis=-1, keepdims=True)
        o_ref[rows, :] = zc * lax.rsqrt(var + LN_EPS) * g_ref[...] + b_ref[...]


def _out(x, proj, conv_a, conv_c, conv_c_b, lru_w, lru_b, lru_lambda, yb, yd, w_out_bf16, ada, ln_g, ln_b, layer):
    s, d = x.shape
    tm = OUT_TM
    col = lambda c: pl.BlockSpec((tm, BR), lambda i, c=c: (i, c))
    branch = pl.BlockSpec((tm, BR), lambda i: (i, 0))
    vec = lambda a: pl.BlockSpec((None,) + a.shape[1:], lambda i, nd=a.ndim: (layer,) + (0,) * (nd - 1))
    conv_c_b = conv_c_b.reshape(DEPTH, 1, BR)
    lam = lru_lambda.reshape(DEPTH, 1, BR)
    ln_g = ln_g.reshape(DEPTH, 1, d)
    ln_b = ln_b.reshape(DEPTH, 1, d)
    return pl.pallas_call(
        _out_kernel,
        grid=(s // tm,),
        in_specs=[pl.BlockSpec((tm, d), lambda i: (i, 0)),
                  col(COL_A_B), col(COL_A_C), col(COL_A_X), col(COL_A_G), vec(conv_a),
                  col(COL_C_X), col(COL_C_G), vec(conv_c), vec(conv_c_b), vec(lru_w), vec(lru_b), vec(lam),
                  branch, branch,
                  pl.BlockSpec((None, 4 * BR, d), lambda i: (layer, 0, 0), pipeline_mode=pl.Buffered(1)),
                  pl.BlockSpec((1, d), lambda i: (0, 2)),
                  vec(ln_g), vec(ln_b)],
        out_specs=pl.BlockSpec((tm, d), lambda i: (i, 0)),
        out_shape=jax.ShapeDtypeStruct((s, d), jnp.float32),
        scratch_shapes=[pltpu.VMEM((SUBLANES, BR), jnp.float32), pltpu.VMEM((SUBLANES, BR), jnp.float32),
                        pltpu.VMEM((1, BR), jnp.float32)],
        compiler_params=_params("arbitrary"),
        name="out_proj_ln",
    )(x, proj, proj, proj, proj, conv_a, proj, proj, conv_c, conv_c_b, lru_w, lru_b, lam,
      yb, yd, w_out_bf16, ada, ln_g, ln_b)


def kernel(x, c, rel_bias, w_ada, b_ada, w_in, conv_a, conv_c, conv_c_b, lru_wa, lru_ba, lru_wx, lru_bx, lru_lambda, s5_lam_re, s5_lam_im, s5_log_dt, s5_b_re, s5_b_im, s5_c_re, s5_c_im, s5_d, s5_w_glu, s5_b_glu, w_out, ln_g, ln_b):
    bsz, s, d = x.shape
    assert bsz == 1 and w_in.shape == (DEPTH, d, N_IN)
    xs = x.reshape(s, d)
    c_col = c.reshape(d, 1)
    b_ada3 = b_ada.reshape(DEPTH, 1, 3 * d)
    w_out_bf16 = w_out.astype(jnp.bfloat16)
    w_glu_bf16 = s5_w_glu.astype(jnp.bfloat16)
    lru_w = jnp.stack([jnp.concatenate([_block_diag(lru_wa[l]), _block_diag(lru_wx[l])], axis=1)
                       for l in range(DEPTH)]).astype(jnp.bfloat16)
    lru_b = jnp.concatenate([lru_ba, lru_bx], axis=1).reshape(DEPTH, 1, 2 * BR)

    for l in range(DEPTH):
        ada = _ada(c_col, w_ada, b_ada3, l)
        proj = _proj(xs, ada, w_in, l)
        yb = _attn(proj, rel_bias)
        bmat, cmat, a_re, a_im = _s5_discretize(
            s5_lam_re[l], s5_lam_im[l], s5_log_dt[l], s5_b_re[l], s5_b_im[l], s5_c_re[l], s5_c_im[l])
        yd = _s5(proj, bmat, a_re, a_im, cmat, s5_d, w_glu_bf16, s5_b_glu, l)
        xs = _out(xs, proj, conv_a, conv_c, conv_c_b, lru_w, lru_b, lru_lambda, yb, yd, w_out_bf16, ada,
                  ln_g, ln_b, l)
    return xs.reshape(bsz, s, d)
```

```python
import functools
import math

import numpy as np
import jax
import jax.numpy as jnp
from jax import lax
from jax.experimental import pallas as pl
from jax.experimental.pallas import tpu as pltpu

BR = 512
N_IN = 12 * BR
ATT_HEADS = 8
ATT_HEAD_DIM = 64
BLK = 128
SPAN = 128
DILATIONS = (1, 4, 16)
REL_BUCKETS = 32
REL_MAX_DIST = 2048
LRU_HEADS = 8
LRU_C = 8.0
S5_CH = 16
S5_GROUPS = 32
S5_STATE = 64
DEPTH = 2
ALPHA = (2 * DEPTH) ** 0.25
LN_EPS = 1e-5

SUBLANES = 8
LANES = 128
VMEM_LIMIT = 56 * 1024 * 1024

COL_A_B, COL_A_C, COL_A_X, COL_A_G = 0, 1, 2, 3
COL_Q, COL_K, COL_V, COL_B_G = 4, 5, 6, 7
COL_C_X, COL_C_G = 8, 9
COL_D_U, COL_D_G = 10, 11


def _silu(x):
    return x * _sigmoid(x)


_sigmoid = jax.nn.sigmoid


def _log1p(x):
    w = 1.0 + x
    return jnp.where(w == 1.0, x, x * jnp.log(w) / (w - 1.0))


def _params(*sem):
    return pltpu.CompilerParams(dimension_semantics=sem, vmem_limit_bytes=VMEM_LIMIT)


def _delayed(x, prev_ref, max_delay):
    assert 0 < max_delay < SUBLANES
    prev = prev_ref[...]
    row = lax.broadcasted_iota(jnp.int32, prev.shape, 0)
    taps = []
    for d in range(1, max_delay + 1):
        rolled = pltpu.roll(x, d, 0)
        top = jnp.where(row < d, pltpu.roll(prev, d, 0), rolled[:SUBLANES])
        taps.append(jnp.concatenate([top, rolled[SUBLANES:]], axis=0))
    prev_ref[...] = x[x.shape[0] - SUBLANES:]
    return taps


def _shift_rows_fill(x, d, fill, period):
    assert 0 < d < period <= SUBLANES
    rolled = pltpu.roll(x, d, 0)
    row = lax.broadcasted_iota(jnp.int32, x.shape, 0) & (period - 1)
    return jnp.where(row < d, jnp.asarray(fill, x.dtype), rolled)


def _ada_kernel(c_ref, w_ref, b_ref, o_ref):
    cond = _silu(c_ref[...])
    bias = b_ref[pl.ds(pl.program_id(0), 1), :]
    o_ref[...] = jnp.sum(cond * w_ref[...], axis=0, keepdims=True) + bias


def _ada(c_col, w_ada, b_ada):
    d = c_col.shape[0]
    depth, _, n = w_ada.shape
    tn = 512
    return pl.pallas_call(
        _ada_kernel,
        grid=(depth, n // tn),
        in_specs=[
            pl.BlockSpec((d, 1), lambda l, j: (0, 0)),
            pl.BlockSpec((None, d, tn), lambda l, j: (l, 0, j)),
            pl.BlockSpec((depth, tn), lambda l, j: (0, j)),
        ],
        out_specs=pl.BlockSpec((None, 1, tn), lambda l, j: (l, 0, j)),
        out_shape=jax.ShapeDtypeStruct((depth, 1, n), jnp.float32),
        compiler_params=_params("arbitrary", "arbitrary"),
        name="ada",
    )(c_col, w_ada, b_ada)


def _proj_kernel(x_ref, shift_ref, scale_ref, w_ref, o_ref, h_ref):
    @pl.when(pl.program_id(1) == 0)
    def _():
        h = x_ref[...] * (1.0 + scale_ref[...]) + shift_ref[...]
        h_ref[...] = h.astype(jnp.bfloat16)

    o_ref[...] = jnp.dot(h_ref[...], w_ref[...].astype(jnp.bfloat16), preferred_element_type=jnp.float32)


def _proj(x, ada, w_in, layer):
    s, d = x.shape
    n = w_in.shape[2]
    tm, tn = 1024, 1024
    return pl.pallas_call(
        _proj_kernel,
        grid=(s // tm, n // tn),
        in_specs=[
            pl.BlockSpec((tm, d), lambda i, j: (i, 0)),
            pl.BlockSpec((None, 1, d), lambda i, j: (layer, 0, 0)),
            pl.BlockSpec((None, 1, d), lambda i, j: (layer, 0, 1)),
            pl.BlockSpec((None, d, tn), lambda i, j: (layer, 0, j)),
        ],
        out_specs=pl.BlockSpec((tm, tn), lambda i, j: (i, j)),
        out_shape=jax.ShapeDtypeStruct((s, n), jnp.float32),
        scratch_shapes=[pltpu.VMEM((tm, d), jnp.bfloat16)],
        compiler_params=_params("arbitrary", "arbitrary"),
        name="proj",
    )(x, ada, ada, w_in)


ATT_TILE = max(DILATIONS) * BLK
ATT_BLOCKS = ATT_TILE // BLK
N_PAIRS = ATT_HEADS // 2
DEINT = 4
PLANE = ATT_TILE // DEINT
QROWS = BLK // DEINT
MIX_ROWS = 256
ATT_UNROLL = 16
MASKED = -1e30


def _t5_bucket_tables():
    assert DILATIONS == (1, 4, 16) and DEINT == 4
    i = np.arange(BLK)[:, None]
    j = np.arange(2 * BLK)[None, :]
    delta = i + BLK - j
    valid = (delta >= 0) & (delta <= SPAN)
    max_exact = REL_BUCKETS // 2
    tables = []
    for dil in DILATIONS:
        dist = np.clip(delta, 0, SPAN) * dil
        nf = np.maximum(dist, 1).astype(np.float32)
        large = max_exact + (np.log(nf / np.float32(max_exact)) / np.float32(math.log(REL_MAX_DIST / max_exact))
                             * np.float32(REL_BUCKETS - max_exact)).astype(np.int32)
        bucket = np.where(dist < max_exact, dist, np.minimum(large, REL_BUCKETS - 1))
        table = np.stack([np.where(valid & (j >= BLK), bucket, -1), np.where(valid, bucket, -1)])
        if dil == 1:
            rows = np.array([DEINT * a + r for r in range(DEINT) for a in range(QROWS)])
            cols = np.array([blk * BLK + DEINT * a + r
                             for r in range(DEINT) for blk in range(2) for a in range(QROWS)])
            table = table[:, rows][:, :, cols]
        tables.append(table)
    return jnp.asarray(np.stack(tables), jnp.int32)


def _attend(q, k, v, bias, low):
    q = (q * (ATT_HEAD_DIM ** -0.5)).astype(jnp.bfloat16)
    zero = jnp.zeros_like(q)
    q2 = jnp.concatenate([jnp.where(low, q, zero), jnp.where(low, zero, q)], axis=0)
    sc = lax.dot_general(q2, k.astype(jnp.bfloat16), (((1,), (1,)), ((), ())),
                         preferred_element_type=jnp.float32) + bias
    m = jnp.max(sc, axis=-1, keepdims=True)
    p = jnp.exp(sc - m)
    l = jnp.sum(p, axis=-1, keepdims=True)
    pv = jnp.dot(p.astype(jnp.bfloat16), v.astype(jnp.bfloat16), preferred_element_type=jnp.float32)
    o = jnp.where(low, pv[:BLK], pv[BLK:])
    l2 = jnp.where(low, l[:BLK], l[BLK:])
    m2 = jnp.where(low, m[:BLK], m[BLK:])
    return o / l2, m2 + jnp.log(l2)


def _attn_kernel(rb_ref, bucket_ref, q_ref, k_ref, v_ref, g_ref, y_ref,
                 q4_ref, k4_ref, v4_ref, o4_ref, lse4_ref, ynat_ref, bias_ref):
    pair = pl.program_id(0)
    t = pl.program_id(1)
    lane = lax.broadcasted_iota(jnp.int32, (BLK, LANES), 1)
    low = lane < ATT_HEAD_DIM
    n_pat = len(DILATIONS)

    @pl.when(t == 0)
    def _():
        k4_ref[:, 0:PLANE, :] = jnp.zeros((DEINT, PLANE, LANES), jnp.float32)
        v4_ref[:, 0:PLANE, :] = jnp.zeros((DEINT, PLANE, LANES), jnp.float32)
        for g in range(n_pat):
            bucket = bucket_ref[g, 1]
            no_prev = bucket_ref[g, 0] < 0
            hits = [bucket == b for b in range(REL_BUCKETS)]
            for half in range(2):
                acc = jnp.full(bucket.shape, MASKED, jnp.float32)
                for b in range(REL_BUCKETS):
                    acc = jnp.where(hits[b], rb_ref[b, 2 * pair + half], acc)
                bias_ref[g, 1, half * BLK:(half + 1) * BLK, :] = acc
                bias_ref[g, 0, half * BLK:(half + 1) * BLK, :] = jnp.where(no_prev, MASKED, acc)

    for r in range(DEINT):
        q4_ref[r] = q_ref[pl.ds(r, PLANE, stride=DEINT), :]
        k4_ref[r, PLANE:2 * PLANE, :] = k_ref[pl.ds(r, PLANE, stride=DEINT), :]
        v4_ref[r, PLANE:2 * PLANE, :] = v_ref[pl.ds(r, PLANE, stride=DEINT), :]

    def d1_qk(n):
        a0 = pl.multiple_of(n * QROWS, QROWS)
        q = jnp.concatenate([q4_ref[r, pl.ds(a0, QROWS), :] for r in range(DEINT)], axis=0)
        k = jnp.concatenate([k4_ref[r, pl.ds(PLANE - QROWS + a0, 2 * QROWS), :] for r in range(DEINT)], axis=0)
        return q, k, (t > 0) | (n > 0)

    def d1_v(n):
        a0 = pl.multiple_of(n * QROWS, QROWS)
        return jnp.concatenate([v4_ref[r, pl.ds(PLANE - QROWS + a0, 2 * QROWS), :] for r in range(DEINT)], axis=0)

    def d1_store(n, o, lse):
        a0 = pl.multiple_of(n * QROWS, QROWS)
        for r in range(DEINT):
            o4_ref[0, r, pl.ds(a0, QROWS), :] = o[r * QROWS:(r + 1) * QROWS]
            lse4_ref[0, r, pl.ds(a0, QROWS), :] = lse[r * QROWS:(r + 1) * QROWS]

    def d4_qk(b):
        r, n = b // DEINT, b % DEINT
        a0 = pl.multiple_of(n * BLK, BLK)
        return (q4_ref[r, pl.ds(a0, BLK), :], k4_ref[r, pl.ds(PLANE - BLK + a0, 2 * BLK), :],
                (t > 0) | (n > 0))

    def d4_v(b):
        r, n = b // DEINT, b % DEINT
        return v4_ref[r, pl.ds(PLANE - BLK + pl.multiple_of(n * BLK, BLK), 2 * BLK), :]

    def d4_store(b, o, lse):
        r, n = b // DEINT, b % DEINT
        a0 = pl.multiple_of(n * BLK, BLK)
        o4_ref[1, r, pl.ds(a0, BLK), :] = o
        lse4_ref[1, r, pl.ds(a0, BLK), :] = lse

    def d16_qk(b):
        lo, hi = b // DEINT, b % DEINT
        return (q4_ref[lo, pl.ds(hi, BLK, stride=DEINT), :],
                k4_ref[lo, pl.ds(hi, 2 * BLK, stride=DEINT), :], t > 0)

    def d16_v(b):
        lo, hi = b // DEINT, b % DEINT
        return v4_ref[lo, pl.ds(hi, 2 * BLK, stride=DEINT), :]

    def d16_store(b, o, lse):
        lo, hi = b // DEINT, b % DEINT
        o4_ref[2, lo, pl.ds(hi, BLK, stride=DEINT), :] = o
        lse4_ref[2, lo, pl.ds(hi, BLK, stride=DEINT), :] = lse

    for g, (qk, vload, store) in enumerate(((d1_qk, d1_v, d1_store), (d4_qk, d4_v, d4_store),
                                            (d16_qk, d16_v, d16_store))):
        def unit(u, carry, g=g, qk=qk, vload=vload, store=store):
            q, k, has_prev = qk(u)
            o, lse = _attend(q, k, vload(u), bias_ref[g, jnp.where(has_prev, 1, 0)], low)
            store(u, o, lse)
            return carry

        lax.fori_loop(0, ATT_BLOCKS, unit, 0, unroll=ATT_UNROLL)

    for r in range(DEINT):
        def mix(c, carry, r=r):
            a0 = pl.multiple_of(c * MIX_ROWS, MIX_ROWS)
            nat = pl.ds(DEINT * a0 + r, MIX_ROWS, stride=DEINT)
            a = [lse4_ref[g, r, pl.ds(a0, MIX_ROWS), :] for g in range(n_pat)]
            m = functools.reduce(jnp.maximum, a)
            e = [jnp.exp(x - m) for x in a]
            num = sum(e[g] * o4_ref[g, r, pl.ds(a0, MIX_ROWS), :] for g in range(n_pat))
            ynat_ref[nat, :] = num / sum(e) * _silu(g_ref[nat, :])
            return carry

        lax.fori_loop(0, PLANE // MIX_ROWS, mix, 0)

    y_ref[...] = ynat_ref[...].astype(y_ref.dtype)
    k4_ref[:, 0:PLANE, :] = k4_ref[:, PLANE:2 * PLANE, :]
    v4_ref[:, 0:PLANE, :] = v4_ref[:, PLANE:2 * PLANE, :]


def _attn(proj, rel_bias):
    s = proj.shape[0]
    assert s % ATT_TILE == 0
    lanes_per_col = BR // LANES
    col = lambda c: pl.BlockSpec((ATT_TILE, LANES), lambda p, t, c=c: (t, c * lanes_per_col + p))
    n_pat = len(DILATIONS)
    return pl.pallas_call(
        _attn_kernel,
        grid=(N_PAIRS, s // ATT_TILE),
        in_specs=[
            pl.BlockSpec(memory_space=pltpu.SMEM),
            pl.BlockSpec((n_pat, 2, BLK, 2 * BLK), lambda p, t: (0, 0, 0, 0)),
            col(COL_Q), col(COL_K), col(COL_V), col(COL_B_G),
        ],
        out_specs=pl.BlockSpec((ATT_TILE, LANES), lambda p, t: (t, p)),
        out_shape=jax.ShapeDtypeStruct((s, BR), jnp.bfloat16),
        scratch_shapes=[pltpu.VMEM((DEINT, PLANE, LANES), jnp.float32),
                        pltpu.VMEM((DEINT, 2 * PLANE, LANES), jnp.float32),
                        pltpu.VMEM((DEINT, 2 * PLANE, LANES), jnp.float32),
                        pltpu.VMEM((n_pat, DEINT, PLANE, LANES), jnp.float32),
                        pltpu.VMEM((n_pat, DEINT, PLANE, LANES), jnp.float32),
                        pltpu.VMEM((ATT_TILE, LANES), jnp.float32),
                        pltpu.VMEM((n_pat, 2, 2 * BLK, 2 * BLK), jnp.float32)],
        compiler_params=_params("arbitrary", "arbitrary"),
        name="dilated_attn",
    )(rel_bias, _t5_bucket_tables(), proj, proj, proj, proj)


def _rglru_tile(cx, cg, cw_ref, cb_ref, w_ref, ba_ref, bx_ref, lam_ref, hist_ref, h_ref):
    tm = cx.shape[0]
    d1, d2, d3 = _delayed(cx, hist_ref, 3)
    xc = d3 * cw_ref[0:1, :]
    xc = xc + d2 * cw_ref[1:2, :]
    xc = xc + d1 * cw_ref[2:3, :]
    xc = xc + cx * cw_ref[3:4, :]
    xc = xc + cb_ref[...]

    z = jnp.dot(xc.astype(jnp.bfloat16), w_ref[...], preferred_element_type=jnp.float32)
    r = _sigmoid(z[:, :BR] + ba_ref[...])
    ig = _sigmoid(z[:, BR:] + bx_ref[...])
    neg_lam = -lam_ref[...]
    softplus = jnp.maximum(neg_lam, 0.0) + _log1p(jnp.exp(-jnp.abs(neg_lam)))
    log_a = -LRU_C * r * softplus
    a = jnp.exp(log_a)
    b = jnp.sqrt(jnp.tanh(-log_a) * (a * a + 1.0)) * ig * xc

    d = 1
    while d < SUBLANES:
        b = b + a * _shift_rows_fill(b, d, 0.0, period=SUBLANES)
        a = a * _shift_rows_fill(a, d, 1.0, period=SUBLANES)
        d *= 2
    carry = h_ref[...]
    groups = []
    for g in range(tm // SUBLANES):
        rows = slice(g * SUBLANES, (g + 1) * SUBLANES)
        hg = b[rows] + a[rows] * carry
        carry = hg[SUBLANES - 1:]
        groups.append(hg)
    h_ref[...] = carry
    h = jnp.concatenate(groups, axis=0)
    return (h * _silu(cg)).astype(jnp.bfloat16)


N_STATE = S5_GROUPS * S5_STATE
S5_TM = 512
S5_CHUNKS = BR // LANES
S5_CHUNK_STATES = N_STATE // S5_CHUNKS
SLABS_PER_CHUNK = S5_CHUNK_STATES // LANES
N_SLABS = N_STATE // LANES
SLAB_PITCH = S5_TM + SUBLANES
S5_UNROLL = 8


def _s5_kernel(u_ref, g_ref, bmat_ref, ar_ref, ai_ref, cmat_ref, d_ref, wglu_ref, bglu_ref,
               y_ref, xs_ref, state_ref, *, layer):
    d_ref = d_ref.at[pl.ds(layer, 1)]
    bglu_ref = bglu_ref.at[pl.ds(layer, 1)]

    @pl.when(pl.program_id(0) == 0)
    def _():
        state_ref[...] = jnp.zeros_like(state_ref)

    u = u_ref[...]
    tm = u.shape[0]
    ub = u.astype(jnp.bfloat16)
    for j in range(S5_CHUNKS):
        bu = jnp.dot(ub[:, j * LANES:(j + 1) * LANES], bmat_ref[j], preferred_element_type=jnp.float32)
        for part in range(2):
            for cc in range(SLABS_PER_CHUNK):
                slab = part * N_SLABS + j * SLABS_PER_CHUNK + cc
                col = part * S5_CHUNK_STATES + cc * LANES
                xs_ref[pl.ds(slab * SLAB_PITCH, tm), :] = bu[:, col:col + LANES]

    n_vreg = N_SLABS // SUBLANES
    a_re = [ar_ref[v * SUBLANES:(v + 1) * SUBLANES, :] for v in range(n_vreg)]
    a_im = [ai_ref[v * SUBLANES:(v + 1) * SUBLANES, :] for v in range(n_vreg)]

    def rows(part, v, t):
        return pl.ds((part * N_SLABS + v * SUBLANES) * SLAB_PITCH + t, SUBLANES, stride=SLAB_PITCH)

    def step(t, x):
        new = []
        for v in range(n_vreg):
            xr, xi = x[2 * v], x[2 * v + 1]
            nr = (a_re[v] * xr + xs_ref[rows(0, v, t), :]) - a_im[v] * xi
            ni = (a_re[v] * xi + xs_ref[rows(1, v, t), :]) + a_im[v] * xr
            xs_ref[rows(0, v, t), :] = nr
            xs_ref[rows(1, v, t), :] = ni
            new += [nr, ni]
        return tuple(new)

    x0 = tuple(state_ref[i] for i in range(2 * n_vreg))
    x1 = lax.fori_loop(0, tm, step, x0, unroll=S5_UNROLL)
    for i in range(2 * n_vreg):
        state_ref[i] = x1[i]

    ys = []
    for j in range(S5_CHUNKS):
        xcat = jnp.concatenate(
            [xs_ref[pl.ds((part * N_SLABS + j * SLABS_PER_CHUNK + cc) * SLAB_PITCH, tm), :]
             for part in range(2) for cc in range(SLABS_PER_CHUNK)], axis=1)
        ys.append(jnp.dot(xcat.astype(jnp.bfloat16), cmat_ref[j], preferred_element_type=jnp.float32))
    y = jnp.concatenate(ys, axis=1) + d_ref[...] * u
    y = 0.5 * y * (1.0 + jnp.tanh(math.sqrt(2.0 / math.pi) * (y + 0.044715 * (y * y * y))))
    gate = jnp.dot(y.astype(jnp.bfloat16), wglu_ref[...], preferred_element_type=jnp.float32) + bglu_ref[...]
    y = y * _sigmoid(gate)
    y_ref[...] = (y * _silu(g_ref[...])).astype(y_ref.dtype)


def _s5_discretize(lam_re, lam_im, log_dt, b_re, b_im, c_re, c_im):
    depth = lam_re.shape[0]
    per = S5_GROUPS // S5_CHUNKS
    dt = jnp.exp(log_dt)[..., None]
    mag = jnp.exp(lam_re * dt)
    ab_re = mag * jnp.cos(lam_im * dt)
    ab_im = mag * jnp.sin(lam_im * dt)
    den = lam_re * lam_re + lam_im * lam_im
    f_re = ((ab_re - 1.0) * lam_re + ab_im * lam_im) / den
    f_im = (ab_im * lam_re - (ab_re - 1.0) * lam_im) / den
    bb_re = f_re[..., None] * b_re - f_im[..., None] * b_im
    bb_im = f_re[..., None] * b_im + f_im[..., None] * b_re
    eye = jnp.eye(per, dtype=jnp.float32)
    bb = jnp.stack([bb_re, bb_im], axis=1).reshape(depth, 2, S5_CHUNKS, per, S5_STATE, S5_CH)
    bmat = jnp.einsum('lkjgpc,gh->ljgckhp', bb, eye).reshape(depth, S5_CHUNKS, LANES, 2 * S5_CHUNK_STATES)
    cc = jnp.stack([c_re, -c_im], axis=1).reshape(depth, 2, S5_CHUNKS, per, S5_CH, S5_STATE)
    cmat = jnp.einsum('lkjgcp,gh->ljkhpgc', cc, eye).reshape(depth, S5_CHUNKS, 2 * S5_CHUNK_STATES, LANES)
    return (bmat.astype(jnp.bfloat16), cmat.astype(jnp.bfloat16),
            ab_re.reshape(depth, N_SLABS, LANES), ab_im.reshape(depth, N_SLABS, LANES))


def _s5(proj, bmat, a_re, a_im, cmat, s5_d, w_glu_bf16, b_glu, layer):
    s = proj.shape[0]
    tm = S5_TM
    full = lambda a: pl.BlockSpec(a.shape, lambda i, nd=a.ndim: (0,) * nd)
    vec = lambda a: pl.BlockSpec((None,) + a.shape[1:], lambda i, nd=a.ndim: (layer,) + (0,) * (nd - 1))
    return pl.pallas_call(
        functools.partial(_s5_kernel, layer=layer),
        grid=(s // tm,),
        in_specs=[pl.BlockSpec((tm, BR), lambda i: (i, COL_D_U)),
                  pl.BlockSpec((tm, BR), lambda i: (i, COL_D_G)),
                  vec(bmat), vec(a_re), vec(a_im), vec(cmat),
                  full(s5_d), vec(w_glu_bf16), full(b_glu)],
        out_specs=pl.BlockSpec((tm, BR), lambda i: (i, 0)),
        out_shape=jax.ShapeDtypeStruct((s, BR), jnp.bfloat16),
        scratch_shapes=[pltpu.VMEM((2 * N_SLABS * SLAB_PITCH, LANES), jnp.float32),
                        pltpu.VMEM((2 * N_SLABS // SUBLANES, SUBLANES, LANES), jnp.float32)],
        compiler_params=_params("arbitrary"),
        name="s5",
    )(proj, proj, bmat, a_re, a_im, cmat, s5_d, w_glu_bf16, b_glu)


OUT_TM = 512
OUT_SUB = 256


def _gated_conv_tile(ab, ac, ax, ag, cw_ref, hist_ref):
    u = ac * ax
    d1, d2 = _delayed(u, hist_ref, 2)
    conv = d2 * cw_ref[0:1, :]
    conv = conv + d1 * cw_ref[1:2, :]
    conv = conv + u * cw_ref[2:3, :]
    return (ab * conv * _silu(ag)).astype(jnp.bfloat16)


def _out_kernel(x_ref, ab_ref, ac_ref, ax_ref, ag_ref, caw_ref,
                cx_ref, cg_ref, ccw_ref, ccb_ref, lw_ref, lba_ref, lbx_ref, lam_ref,
                yb_ref, yd_ref, w_ref, gate_ref, g_ref, b_ref, o_ref,
                hist_a_ref, hist_c_ref, h_ref, *, layer):
    ccb_ref, lba_ref, lbx_ref, lam_ref, g_ref, b_ref = (
        r.at[pl.ds(layer, 1)] for r in (ccb_ref, lba_ref, lbx_ref, lam_ref, g_ref, b_ref))

    @pl.when(pl.program_id(0) == 0)
    def _():
        hist_a_ref[...] = jnp.zeros_like(hist_a_ref)
        hist_c_ref[...] = jnp.zeros_like(hist_c_ref)
        h_ref[...] = jnp.zeros_like(h_ref)

    for sub in range(x_ref.shape[0] // OUT_SUB):
        rows = slice(sub * OUT_SUB, (sub + 1) * OUT_SUB)
        ya = _gated_conv_tile(ab_ref[rows, :], ac_ref[rows, :], ax_ref[rows, :], ag_ref[rows, :],
                              caw_ref, hist_a_ref)
        yc = _rglru_tile(cx_ref[rows, :], cg_ref[rows, :], ccw_ref, ccb_ref, lw_ref, lba_ref, lbx_ref,
                         lam_ref, hist_c_ref, h_ref)
        y = jnp.dot(ya, w_ref[0 * BR:1 * BR, :], preferred_element_type=jnp.float32)
        y = y + jnp.dot(yb_ref[rows, :], w_ref[1 * BR:2 * BR, :], preferred_element_type=jnp.float32)
        y = y + jnp.dot(yc, w_ref[2 * BR:3 * BR, :], preferred_element_type=jnp.float32)
        y = y + jnp.dot(yd_ref[rows, :], w_ref[3 * BR:4 * BR, :], preferred_element_type=jnp.float32)
        z = ALPHA * x_ref[rows, :] + (1.0 + gate_ref[...]) * y
        mu = jnp.mean(z, axis=-1, keepdims=True)
        zc = z - mu
        var = jnp.mean(zc * zc, axis=-1, keepdims=True)
        o_ref[rows, :] = zc * lax.rsqrt(var + LN_EPS) * g_ref[...] + b_ref[...]


def _out(x, proj, conv_a, conv_c, conv_c_b, lru_w, lru_ba, lru_bx, lru_lambda, yb, yd, w_out_bf16, ada,
         ln_g, ln_b, layer):
    s, d = x.shape
    tm = OUT_TM
    col = lambda c: pl.BlockSpec((tm, BR), lambda i, c=c: (i, c))
    branch = pl.BlockSpec((tm, BR), lambda i: (i, 0))
    full = lambda a: pl.BlockSpec(a.shape, lambda i, nd=a.ndim: (0,) * nd)
    vec = lambda a: pl.BlockSpec((None,) + a.shape[1:], lambda i, nd=a.ndim: (layer,) + (0,) * (nd - 1))
    return pl.pallas_call(
        functools.partial(_out_kernel, layer=layer),
        grid=(s // tm,),
        in_specs=[pl.BlockSpec((tm, d), lambda i: (i, 0)),
                  col(COL_A_B), col(COL_A_C), col(COL_A_X), col(COL_A_G), vec(conv_a),
                  col(COL_C_X), col(COL_C_G), vec(conv_c), full(conv_c_b), vec(lru_w),
                  full(lru_ba), full(lru_bx), full(lru_lambda),
                  branch, branch,
                  pl.BlockSpec((None, 4 * BR, d), lambda i: (layer, 0, 0), pipeline_mode=pl.Buffered(1)),
                  pl.BlockSpec((None, 1, d), lambda i: (layer, 0, 2)),
                  full(ln_g), full(ln_b)],
        out_specs=pl.BlockSpec((tm, d), lambda i: (i, 0)),
        out_shape=jax.ShapeDtypeStruct((s, d), jnp.float32),
        scratch_shapes=[pltpu.VMEM((SUBLANES, BR), jnp.float32), pltpu.VMEM((SUBLANES, BR), jnp.float32),
                        pltpu.VMEM((1, BR), jnp.float32)],
        compiler_params=_params("arbitrary"),
        name="out_proj_ln",
    )(x, proj, proj, proj, proj, conv_a, proj, proj, conv_c, conv_c_b, lru_w, lru_ba, lru_bx, lru_lambda,
      yb, yd, w_out_bf16, ada, ln_g, ln_b)


def kernel(x, c, rel_bias, w_ada, b_ada, w_in, conv_a, conv_c, conv_c_b, lru_wa, lru_ba, lru_wx, lru_bx, lru_lambda, s5_lam_re, s5_lam_im, s5_log_dt, s5_b_re, s5_b_im, s5_c_re, s5_c_im, s5_d, s5_w_glu, s5_b_glu, w_out, ln_g, ln_b):
    bsz, s, d = x.shape
    assert bsz == 1 and w_in.shape == (DEPTH, d, N_IN)
    xs = x.reshape(s, d)
    w_out_bf16 = w_out.astype(jnp.bfloat16)
    w_glu_bf16 = s5_w_glu.astype(jnp.bfloat16)
    eye = jnp.eye(LRU_HEADS, dtype=jnp.float32)
    lru_w = jnp.einsum('lkhij,hg->lhikgj', jnp.stack([lru_wa, lru_wx], axis=1), eye)
    lru_w = lru_w.reshape(DEPTH, BR, 2 * BR).astype(jnp.bfloat16)
    bmat, cmat, a_re, a_im = _s5_discretize(s5_lam_re, s5_lam_im, s5_log_dt, s5_b_re, s5_b_im, s5_c_re, s5_c_im)
    ada = _ada(c.reshape(d, 1), w_ada, b_ada)

    for l in range(DEPTH):
        proj = _proj(xs, ada, w_in, l)
        yb = _attn(proj, rel_bias)
        yd = _s5(proj, bmat, a_re, a_im, cmat, s5_d, w_glu_bf16, s5_b_glu, l)
        xs = _out(xs, proj, conv_a, conv_c, conv_c_b, lru_w, lru_ba, lru_bx, lru_lambda, yb, yd, w_out_bf16, ada,
                  ln_g, ln_b, l)
    return xs.reshape(bsz, s, d)
```

```python
import functools
import math

import numpy as np
import jax
import jax.numpy as jnp
from jax import lax
from jax.experimental import pallas as pl
from jax.experimental.pallas import tpu as pltpu

BR = 512
N_IN = 12 * BR
ATT_HEADS = 8
ATT_HEAD_DIM = 64
BLK = 128
SPAN = 128
DILATIONS = (1, 4, 16)
REL_BUCKETS = 32
REL_MAX_DIST = 2048
LRU_HEADS = 8
LRU_C = 8.0
S5_CH = 16
S5_GROUPS = 32
S5_STATE = 64
DEPTH = 2
ALPHA = (2 * DEPTH) ** 0.25
LN_EPS = 1e-5

SUBLANES = 8
LANES = 128
VMEM_LIMIT = 56 * 1024 * 1024

COL_A_B, COL_A_C, COL_A_X, COL_A_G = 0, 1, 2, 3
COL_Q, COL_K, COL_V, COL_B_G = 4, 5, 6, 7
COL_C_X, COL_C_G = 8, 9
COL_D_U, COL_D_G = 10, 11


def _silu(x):
    return x * _sigmoid(x)


_sigmoid = jax.nn.sigmoid


def _log1p(x):
    w = 1.0 + x
    return jnp.where(w == 1.0, x, x * jnp.log(w) / (w - 1.0))


def _params(*sem):
    return pltpu.CompilerParams(dimension_semantics=sem, vmem_limit_bytes=VMEM_LIMIT)


def _delayed(x, prev_ref, max_delay):
    assert 0 < max_delay < SUBLANES
    prev = prev_ref[...]
    row = lax.broadcasted_iota(jnp.int32, prev.shape, 0)
    taps = []
    for d in range(1, max_delay + 1):
        rolled = pltpu.roll(x, d, 0)
        top = jnp.where(row < d, pltpu.roll(prev, d, 0), rolled[:SUBLANES])
        taps.append(jnp.concatenate([top, rolled[SUBLANES:]], axis=0))
    prev_ref[...] = x[x.shape[0] - SUBLANES:]
    return taps


def _shift_rows_fill(x, d, fill, period):
    assert 0 < d < period <= SUBLANES
    rolled = pltpu.roll(x, d, 0)
    row = lax.broadcasted_iota(jnp.int32, x.shape, 0) & (period - 1)
    return jnp.where(row < d, jnp.asarray(fill, x.dtype), rolled)


def _ada_kernel(c_ref, w_ref, b_ref, o_ref):
    cond = _silu(c_ref[...])
    bias = b_ref[pl.ds(pl.program_id(0), 1), :]
    o_ref[...] = jnp.sum(cond * w_ref[...], axis=0, keepdims=True) + bias


def _ada(c_col, w_ada, b_ada):
    d = c_col.shape[0]
    depth, _, n = w_ada.shape
    tn = 512
    return pl.pallas_call(
        _ada_kernel,
        grid=(depth, n // tn),
        in_specs=[
            pl.BlockSpec((d, 1), lambda l, j: (0, 0)),
            pl.BlockSpec((None, d, tn), lambda l, j: (l, 0, j)),
            pl.BlockSpec((depth, tn), lambda l, j: (0, j)),
        ],
        out_specs=pl.BlockSpec((None, 1, tn), lambda l, j: (l, 0, j)),
        out_shape=jax.ShapeDtypeStruct((depth, 1, n), jnp.float32),
        compiler_params=_params("arbitrary", "arbitrary"),
        name="ada",
    )(c_col, w_ada, b_ada)


def _proj_kernel(x_ref, shift_ref, scale_ref, w_ref, o_ref, h_ref):
    @pl.when(pl.program_id(1) == 0)
    def _():
        h = x_ref[...] * (1.0 + scale_ref[...]) + shift_ref[...]
        h_ref[...] = h.astype(jnp.bfloat16)

    o_ref[...] = jnp.dot(h_ref[...], w_ref[...].astype(jnp.bfloat16), preferred_element_type=jnp.float32)


def _proj(x, ada, w_in, layer):
    s, d = x.shape
    n = w_in.shape[2]
    tm, tn = 1024, 1024
    return pl.pallas_call(
        _proj_kernel,
        grid=(s // tm, n // tn),
        in_specs=[
            pl.BlockSpec((tm, d), lambda i, j: (i, 0)),
            pl.BlockSpec((None, 1, d), lambda i, j: (layer, 0, 0)),
            pl.BlockSpec((None, 1, d), lambda i, j: (layer, 0, 1)),
            pl.BlockSpec((None, d, tn), lambda i, j: (layer, 0, j)),
        ],
        out_specs=pl.BlockSpec((tm, tn), lambda i, j: (i, j)),
        out_shape=jax.ShapeDtypeStruct((s, n), jnp.float32),
        scratch_shapes=[pltpu.VMEM((tm, d), jnp.bfloat16)],
        compiler_params=_params("arbitrary", "arbitrary"),
        name="proj",
    )(x, ada, ada, w_in)


ATT_TILE = max(DILATIONS) * BLK
ATT_BLOCKS = ATT_TILE // BLK
N_PAIRS = ATT_HEADS // 2
DEINT = 4
PLANE = ATT_TILE // DEINT
QROWS = BLK // DEINT
MIX_ROWS = 256
ATT_UNROLL = 16
MASKED = -1e30


def _t5_bucket_tables():
    assert DILATIONS == (1, 4, 16) and DEINT == 4
    i = np.arange(BLK)[:, None]
    j = np.arange(2 * BLK)[None, :]
    delta = i + BLK - j
    valid = (delta >= 0) & (delta <= SPAN)
    max_exact = REL_BUCKETS // 2
    tables = []
    for dil in DILATIONS:
        dist = np.clip(delta, 0, SPAN) * dil
        nf = np.maximum(dist, 1).astype(np.float32)
        large = max_exact + (np.log(nf / np.float32(max_exact)) / np.float32(math.log(REL_MAX_DIST / max_exact))
                             * np.float32(REL_BUCKETS - max_exact)).astype(np.int32)
        bucket = np.where(dist < max_exact, dist, np.minimum(large, REL_BUCKETS - 1))
        table = np.stack([np.where(valid & (j >= BLK), bucket, -1), np.where(valid, bucket, -1)])
        if dil == 1:
            rows = np.array([DEINT * a + r for r in range(DEINT) for a in range(QROWS)])
            cols = np.array([blk * BLK + DEINT * a + r
                             for r in range(DEINT) for blk in range(2) for a in range(QROWS)])
            table = table[:, rows][:, :, cols]
        tables.append(table)
    return jnp.asarray(np.stack(tables), jnp.int32)


def _attend(q, k, v, bias, low):
    q = (q * (ATT_HEAD_DIM ** -0.5)).astype(jnp.bfloat16)
    zero = jnp.zeros_like(q)
    q2 = jnp.concatenate([jnp.where(low, q, zero), jnp.where(low, zero, q)], axis=0)
    sc = lax.dot_general(q2, k.astype(jnp.bfloat16), (((1,), (1,)), ((), ())),
                         preferred_element_type=jnp.float32) + bias
    m = jnp.max(sc, axis=-1, keepdims=True)
    p = jnp.exp(sc - m)
    l = jnp.sum(p, axis=-1, keepdims=True)
    pv = jnp.dot(p.astype(jnp.bfloat16), v.astype(jnp.bfloat16), preferred_element_type=jnp.float32)
    o = jnp.where(low, pv[:BLK], pv[BLK:])
    l2 = jnp.where(low, l[:BLK], l[BLK:])
    m2 = jnp.where(low, m[:BLK], m[BLK:])
    return o / l2, m2 + jnp.log(l2)


def _attn_kernel(rb_ref, bucket_ref, q_ref, k_ref, v_ref, g_ref, y_ref,
                 q4_ref, k4_ref, v4_ref, o4_ref, lse4_ref, ynat_ref, bias_ref):
    pair = pl.program_id(0)
    t = pl.program_id(1)
    lane = lax.broadcasted_iota(jnp.int32, (BLK, LANES), 1)
    low = lane < ATT_HEAD_DIM
    n_pat = len(DILATIONS)

    @pl.when(t == 0)
    def _():
        k4_ref[:, 0:PLANE, :] = jnp.zeros((DEINT, PLANE, LANES), jnp.float32)
        v4_ref[:, 0:PLANE, :] = jnp.zeros((DEINT, PLANE, LANES), jnp.float32)
        for g in range(n_pat):
            bucket = bucket_ref[g, 1]
            no_prev = bucket_ref[g, 0] < 0
            hits = [bucket == b for b in range(REL_BUCKETS)]
            for half in range(2):
                acc = jnp.full(bucket.shape, MASKED, jnp.float32)
                for b in range(REL_BUCKETS):
                    acc = jnp.where(hits[b], rb_ref[b, 2 * pair + half], acc)
                bias_ref[g, 1, half * BLK:(half + 1) * BLK, :] = acc
                bias_ref[g, 0, half * BLK:(half + 1) * BLK, :] = jnp.where(no_prev, MASKED, acc)

    for r in range(DEINT):
        q4_ref[r] = q_ref[pl.ds(r, PLANE, stride=DEINT), :]
        k4_ref[r, PLANE:2 * PLANE, :] = k_ref[pl.ds(r, PLANE, stride=DEINT), :]
        v4_ref[r, PLANE:2 * PLANE, :] = v_ref[pl.ds(r, PLANE, stride=DEINT), :]

    def d1_qk(n):
        a0 = pl.multiple_of(n * QROWS, QROWS)
        q = jnp.concatenate([q4_ref[r, pl.ds(a0, QROWS), :] for r in range(DEINT)], axis=0)
        k = jnp.concatenate([k4_ref[r, pl.ds(PLANE - QROWS + a0, 2 * QROWS), :] for r in range(DEINT)], axis=0)
        return q, k, (t > 0) | (n > 0)

    def d1_v(n):
        a0 = pl.multiple_of(n * QROWS, QROWS)
        return jnp.concatenate([v4_ref[r, pl.ds(PLANE - QROWS + a0, 2 * QROWS), :] for r in range(DEINT)], axis=0)

    def d1_store(n, o, lse):
        a0 = pl.multiple_of(n * QROWS, QROWS)
        for r in range(DEINT):
            o4_ref[0, r, pl.ds(a0, QROWS), :] = o[r * QROWS:(r + 1) * QROWS]
            lse4_ref[0, r, pl.ds(a0, QROWS), :] = lse[r * QROWS:(r + 1) * QROWS]

    def d4_qk(b):
        r, n = b // DEINT, b % DEINT
        a0 = pl.multiple_of(n * BLK, BLK)
        return (q4_ref[r, pl.ds(a0, BLK), :], k4_ref[r, pl.ds(PLANE - BLK + a0, 2 * BLK), :],
                (t > 0) | (n > 0))

    def d4_v(b):
        r, n = b // DEINT, b % DEINT
        return v4_ref[r, pl.ds(PLANE - BLK + pl.multiple_of(n * BLK, BLK), 2 * BLK), :]

    def d4_store(b, o, lse):
        r, n = b // DEINT, b % DEINT
        a0 = pl.multiple_of(n * BLK, BLK)
        o4_ref[1, r, pl.ds(a0, BLK), :] = o
        lse4_ref[1, r, pl.ds(a0, BLK), :] = lse

    def d16_qk(b):
        lo, hi = b // DEINT, b % DEINT
        return (q4_ref[lo, pl.ds(hi, BLK, stride=DEINT), :],
                k4_ref[lo, pl.ds(hi, 2 * BLK, stride=DEINT), :], t > 0)

    def d16_v(b):
        lo, hi = b // DEINT, b % DEINT
        return v4_ref[lo, pl.ds(hi, 2 * BLK, stride=DEINT), :]

    def d16_store(b, o, lse):
        lo, hi = b // DEINT, b % DEINT
        o4_ref[2, lo, pl.ds(hi, BLK, stride=DEINT), :] = o
        lse4_ref[2, lo, pl.ds(hi, BLK, stride=DEINT), :] = lse

    for g, (qk, vload, store) in enumerate(((d1_qk, d1_v, d1_store), (d4_qk, d4_v, d4_store),
                                            (d16_qk, d16_v, d16_store))):
        def unit(u, carry, g=g, qk=qk, vload=vload, store=store):
            q, k, has_prev = qk(u)
            o, lse = _attend(q, k, vload(u), bias_ref[g, jnp.where(has_prev, 1, 0)], low)
            store(u, o, lse)
            return carry

        lax.fori_loop(0, ATT_BLOCKS, unit, 0, unroll=ATT_UNROLL)

    for r in range(DEINT):
        def mix(c, carry, r=r):
            a0 = pl.multiple_of(c * MIX_ROWS, MIX_ROWS)
            nat = pl.ds(DEINT * a0 + r, MIX_ROWS, stride=DEINT)
            a = [lse4_ref[g, r, pl.ds(a0, MIX_ROWS), :] for g in range(n_pat)]
            m = functools.reduce(jnp.maximum, a)
            e = [jnp.exp(x - m) for x in a]
            num = sum(e[g] * o4_ref[g, r, pl.ds(a0, MIX_ROWS), :] for g in range(n_pat))
            ynat_ref[nat, :] = num / sum(e) * _silu(g_ref[nat, :])
            return carry

        lax.fori_loop(0, PLANE // MIX_ROWS, mix, 0)

    y_ref[...] = ynat_ref[...].astype(y_ref.dtype)
    k4_ref[:, 0:PLANE, :] = k4_ref[:, PLANE:2 * PLANE, :]
    v4_ref[:, 0:PLANE, :] = v4_ref[:, PLANE:2 * PLANE, :]


def _attn(proj, rel_bias):
    s = proj.shape[0]
    assert s % ATT_TILE == 0
    lanes_per_col = BR // LANES
    col = lambda c: pl.BlockSpec((ATT_TILE, LANES), lambda p, t, c=c: (t, c * lanes_per_col + p))
    n_pat = len(DILATIONS)
    return pl.pallas_call(
        _attn_kernel,
        grid=(N_PAIRS, s // ATT_TILE),
        in_specs=[
            pl.BlockSpec(memory_space=pltpu.SMEM),
            pl.BlockSpec((n_pat, 2, BLK, 2 * BLK), lambda p, t: (0, 0, 0, 0)),
            col(COL_Q), col(COL_K), col(COL_V), col(COL_B_G),
        ],
        out_specs=pl.BlockSpec((ATT_TILE, LANES), lambda p, t: (t, p)),
        out_shape=jax.ShapeDtypeStruct((s, BR), jnp.bfloat16),
        scratch_shapes=[pltpu.VMEM((DEINT, PLANE, LANES), jnp.float32),
                        pltpu.VMEM((DEINT, 2 * PLANE, LANES), jnp.float32),
                        pltpu.VMEM((DEINT, 2 * PLANE, LANES), jnp.float32),
                        pltpu.VMEM((n_pat, DEINT, PLANE, LANES), jnp.float32),
                        pltpu.VMEM((n_pat, DEINT, PLANE, LANES), jnp.float32),
                        pltpu.VMEM((ATT_TILE, LANES), jnp.float32),
                        pltpu.VMEM((n_pat, 2, 2 * BLK, 2 * BLK), jnp.float32)],
        compiler_params=_params("arbitrary", "arbitrary"),
        name="dilated_attn",
    )(rel_bias, _t5_bucket_tables(), proj, proj, proj, proj)


def _rglru_tile(cx, cg, cw_ref, cb_ref, w_ref, ba_ref, bx_ref, lam_ref, hist_ref, h_ref):
    tm = cx.shape[0]
    d1, d2, d3 = _delayed(cx, hist_ref, 3)
    xc = d3 * cw_ref[0:1, :]
    xc = xc + d2 * cw_ref[1:2, :]
    xc = xc + d1 * cw_ref[2:3, :]
    xc = xc + cx * cw_ref[3:4, :]
    xc = xc + cb_ref[...]

    z = jnp.dot(xc.astype(jnp.bfloat16), w_ref[...], preferred_element_type=jnp.float32)
    r = _sigmoid(z[:, :BR] + ba_ref[...])
    ig = _sigmoid(z[:, BR:] + bx_ref[...])
    neg_lam = -lam_ref[...]
    softplus = jnp.maximum(neg_lam, 0.0) + _log1p(jnp.exp(-jnp.abs(neg_lam)))
    log_a = -LRU_C * r * softplus
    a = jnp.exp(log_a)
    b = jnp.sqrt(jnp.tanh(-log_a) * (a * a + 1.0)) * ig * xc

    d = 1
    while d < SUBLANES:
        b = b + a * _shift_rows_fill(b, d, 0.0, period=SUBLANES)
        a = a * _shift_rows_fill(a, d, 1.0, period=SUBLANES)
        d *= 2
    carry = h_ref[...]
    groups = []
    for g in range(tm // SUBLANES):
        rows = slice(g * SUBLANES, (g + 1) * SUBLANES)
        hg = b[rows] + a[rows] * carry
        carry = hg[SUBLANES - 1:]
        groups.append(hg)
    h_ref[...] = carry
    h = jnp.concatenate(groups, axis=0)
    return (h * _silu(cg)).astype(jnp.bfloat16)


N_STATE = S5_GROUPS * S5_STATE
S5_TM = 512
S5_CHUNKS = BR // LANES
S5_CHUNK_STATES = N_STATE // S5_CHUNKS
SLABS_PER_CHUNK = S5_CHUNK_STATES // LANES
N_SLABS = N_STATE // LANES
SLAB_PITCH = S5_TM + SUBLANES
S5_UNROLL = 8


def _s5_kernel(ub_ref, uc_ref, gc_ref, bmat_ref, ar_ref, ai_ref, cmat_ref, d_ref, wglu_ref, bglu_ref,
               y_ref, bu0_ref, bu1_ref, xs0_ref, xs1_ref, state_ref, *, layer):
    d_ref = d_ref.at[pl.ds(layer, 1)]
    bglu_ref = bglu_ref.at[pl.ds(layer, 1)]
    step = pl.program_id(0)
    tm = S5_TM
    n_vreg = N_SLABS // SUBLANES

    @pl.when(step == 0)
    def _():
        state_ref[...] = jnp.zeros_like(state_ref)
        for ref in (bu0_ref, bu1_ref, xs0_ref, xs1_ref):
            ref[...] = jnp.zeros_like(ref)

    def slab(part, index):
        return pl.ds((part * N_SLABS + index) * SLAB_PITCH, tm)

    def pipeline(bu_w_ref, bu_r_ref, xs_w_ref, xs_r_ref):
        ub = ub_ref[...].astype(jnp.bfloat16)
        u = uc_ref[...]
        a_re = [ar_ref[v * SUBLANES:(v + 1) * SUBLANES, :] for v in range(n_vreg)]
        a_im = [ai_ref[v * SUBLANES:(v + 1) * SUBLANES, :] for v in range(n_vreg)]
        x = [state_ref[k] for k in range(2 * n_vreg)]
        ys = []

        def project(j):
            bu = jnp.dot(ub[:, j * LANES:(j + 1) * LANES], bmat_ref[j], preferred_element_type=jnp.float32)
            for part in range(2):
                for cc in range(SLABS_PER_CHUNK):
                    col = part * S5_CHUNK_STATES + cc * LANES
                    bu_w_ref[slab(part, j * SLABS_PER_CHUNK + cc), :] = bu[:, col:col + LANES]

        def scan(t0, t1):
            for t in range(t0, t1):
                for v in range(n_vreg):
                    rows_re = pl.ds(v * SUBLANES * SLAB_PITCH + t, SUBLANES, stride=SLAB_PITCH)
                    rows_im = pl.ds((N_SLABS + v * SUBLANES) * SLAB_PITCH + t, SUBLANES, stride=SLAB_PITCH)
                    xr, xi = x[2 * v], x[2 * v + 1]
                    nr = (a_re[v] * xr + bu_r_ref[rows_re, :]) - a_im[v] * xi
                    ni = (a_re[v] * xi + bu_r_ref[rows_im, :]) + a_im[v] * xr
                    xs_w_ref[rows_re, :] = nr
                    xs_w_ref[rows_im, :] = ni
                    x[2 * v], x[2 * v + 1] = nr, ni

        def readout(j):
            xcat = jnp.concatenate([xs_r_ref[slab(part, j * SLABS_PER_CHUNK + cc), :]
                                    for part in range(2) for cc in range(SLABS_PER_CHUNK)], axis=1)
            ys.append(jnp.dot(xcat.astype(jnp.bfloat16), cmat_ref[j], preferred_element_type=jnp.float32))

        jobs = [functools.partial(project, j) for j in range(S5_CHUNKS)]
        jobs += [functools.partial(readout, j) for j in range(S5_CHUNKS)]
        seg = tm // len(jobs)
        for k, job in enumerate(jobs):
            scan(k * seg, (k + 1) * seg)
            job()
        for k in range(2 * n_vreg):
            state_ref[k] = x[k]

        y = jnp.concatenate(ys, axis=1) + d_ref[...] * u
        y = 0.5 * y * (1.0 + jnp.tanh(math.sqrt(2.0 / math.pi) * (y + 0.044715 * (y * y * y))))
        gate = jnp.dot(y.astype(jnp.bfloat16), wglu_ref[...], preferred_element_type=jnp.float32) + bglu_ref[...]
        y = y * _sigmoid(gate)
        y_ref[...] = (y * _silu(gc_ref[...])).astype(y_ref.dtype)

    @pl.when(step % 2 == 0)
    def _():
        pipeline(bu0_ref, bu1_ref, xs0_ref, xs1_ref)

    @pl.when(step % 2 == 1)
    def _():
        pipeline(bu1_ref, bu0_ref, xs1_ref, xs0_ref)


def _s5_discretize(lam_re, lam_im, log_dt, b_re, b_im, c_re, c_im):
    depth = lam_re.shape[0]
    per = S5_GROUPS // S5_CHUNKS
    dt = jnp.exp(log_dt)[..., None]
    mag = jnp.exp(lam_re * dt)
    ab_re = mag * jnp.cos(lam_im * dt)
    ab_im = mag * jnp.sin(lam_im * dt)
    den = lam_re * lam_re + lam_im * lam_im
    f_re = ((ab_re - 1.0) * lam_re + ab_im * lam_im) / den
    f_im = (ab_im * lam_re - (ab_re - 1.0) * lam_im) / den
    bb_re = f_re[..., None] * b_re - f_im[..., None] * b_im
    bb_im = f_re[..., None] * b_im + f_im[..., None] * b_re
    eye = jnp.eye(per, dtype=jnp.float32)
    bb = jnp.stack([bb_re, bb_im], axis=1).reshape(depth, 2, S5_CHUNKS, per, S5_STATE, S5_CH)
    bmat = jnp.einsum('lkjgpc,gh->ljgckhp', bb, eye).reshape(depth, S5_CHUNKS, LANES, 2 * S5_CHUNK_STATES)
    cc = jnp.stack([c_re, -c_im], axis=1).reshape(depth, 2, S5_CHUNKS, per, S5_CH, S5_STATE)
    cmat = jnp.einsum('lkjgcp,gh->ljkhpgc', cc, eye).reshape(depth, S5_CHUNKS, 2 * S5_CHUNK_STATES, LANES)
    return (bmat.astype(jnp.bfloat16), cmat.astype(jnp.bfloat16),
            ab_re.reshape(depth, N_SLABS, LANES), ab_im.reshape(depth, N_SLABS, LANES))


def _s5(proj, bmat, a_re, a_im, cmat, s5_d, w_glu_bf16, b_glu, layer):
    s = proj.shape[0]
    tm = S5_TM
    full = lambda a: pl.BlockSpec(a.shape, lambda i, nd=a.ndim: (0,) * nd)
    vec = lambda a: pl.BlockSpec((None,) + a.shape[1:], lambda i, nd=a.ndim: (layer,) + (0,) * (nd - 1))
    n = s // tm
    lag0 = lambda i: jnp.minimum(i, n - 1)
    lag2 = lambda i: jnp.clip(i - 2, 0, n - 1)
    slabs = pltpu.VMEM((2 * N_SLABS * SLAB_PITCH, LANES), jnp.float32)
    return pl.pallas_call(
        functools.partial(_s5_kernel, layer=layer),
        grid=(n + 2,),
        in_specs=[pl.BlockSpec((tm, BR), lambda i: (lag0(i), COL_D_U)),
                  pl.BlockSpec((tm, BR), lambda i: (lag2(i), COL_D_U)),
                  pl.BlockSpec((tm, BR), lambda i: (lag2(i), COL_D_G)),
                  vec(bmat), vec(a_re), vec(a_im), vec(cmat),
                  full(s5_d), vec(w_glu_bf16), full(b_glu)],
        out_specs=pl.BlockSpec((tm, BR), lambda i: (lag2(i), 0)),
        out_shape=jax.ShapeDtypeStruct((s, BR), jnp.bfloat16),
        scratch_shapes=[slabs, slabs, slabs, slabs,
                        pltpu.VMEM((2 * N_SLABS // SUBLANES, SUBLANES, LANES), jnp.float32)],
        compiler_params=_params("arbitrary"),
        name="s5",
    )(proj, proj, proj, bmat, a_re, a_im, cmat, s5_d, w_glu_bf16, b_glu)


OUT_TM = 512
OUT_SUB = 256


def _gated_conv_tile(ab, ac, ax, ag, cw_ref, hist_ref):
    u = ac * ax
    d1, d2 = _delayed(u, hist_ref, 2)
    conv = d2 * cw_ref[0:1, :]
    conv = conv + d1 * cw_ref[1:2, :]
    conv = conv + u * cw_ref[2:3, :]
    return (ab * conv * _silu(ag)).astype(jnp.bfloat16)


def _out_kernel(x_ref, ab_ref, ac_ref, ax_ref, ag_ref, caw_ref,
                cx_ref, cg_ref, ccw_ref, ccb_ref, lw_ref, lba_ref, lbx_ref, lam_ref,
                yb_ref, yd_ref, w_ref, gate_ref, g_ref, b_ref, o_ref,
                hist_a_ref, hist_c_ref, h_ref, *, layer):
    ccb_ref, lba_ref, lbx_ref, lam_ref, g_ref, b_ref = (
        r.at[pl.ds(layer, 1)] for r in (ccb_ref, lba_ref, lbx_ref, lam_ref, g_ref, b_ref))

    @pl.when(pl.program_id(0) == 0)
    def _():
        hist_a_ref[...] = jnp.zeros_like(hist_a_ref)
        hist_c_ref[...] = jnp.zeros_like(hist_c_ref)
        h_ref[...] = jnp.zeros_like(h_ref)

    for sub in range(x_ref.shape[0] // OUT_SUB):
        rows = slice(sub * OUT_SUB, (sub + 1) * OUT_SUB)
        ya = _gated_conv_tile(ab_ref[rows, :], ac_ref[rows, :], ax_ref[rows, :], ag_ref[rows, :],
                              caw_ref, hist_a_ref)
        yc = _rglru_tile(cx_ref[rows, :], cg_ref[rows, :], ccw_ref, ccb_ref, lw_ref, lba_ref, lbx_ref,
                         lam_ref, hist_c_ref, h_ref)
        y = jnp.dot(ya, w_ref[0 * BR:1 * BR, :], preferred_element_type=jnp.float32)
        y = y + jnp.dot(yb_ref[rows, :], w_ref[1 * BR:2 * BR, :], preferred_element_type=jnp.float32)
        y = y + jnp.dot(yc, w_ref[2 * BR:3 * BR, :], preferred_element_type=jnp.float32)
        y = y + jnp.dot(yd_ref[rows, :], w_ref[3 * BR:4 * BR, :], preferred_element_type=jnp.float32)
        z = ALPHA * x_ref[rows, :] + (1.0 + gate_ref[...]) * y
        mu = jnp.mean(z, axis=-1, keepdims=True)
        zc = z - mu
        var = jnp.mean(zc * zc, axis=-1, keepdims=True)
        o_ref[rows, :] = zc * lax.rsqrt(var + LN_EPS) * g_ref[...] + b_ref[...]


def _out(x, proj, conv_a, conv_c, conv_c_b, lru_w, lru_ba, lru_bx, lru_lambda, yb, yd, w_out_bf16, ada,
         ln_g, ln_b, layer):
    s, d = x.shape
    tm = OUT_TM
    col = lambda c: pl.BlockSpec((tm, BR), lambda i, c=c: (i, c))
    branch = pl.BlockSpec((tm, BR), lambda i: (i, 0))
    full = lambda a: pl.BlockSpec(a.shape, lambda i, nd=a.ndim: (0,) * nd)
    vec = lambda a: pl.BlockSpec((None,) + a.shape[1:], lambda i, nd=a.ndim: (layer,) + (0,) * (nd - 1))
    return pl.pallas_call(
        functools.partial(_out_kernel, layer=layer),
        grid=(s // tm,),
        in_specs=[pl.BlockSpec((tm, d), lambda i: (i, 0)),
                  col(COL_A_B), col(COL_A_C), col(COL_A_X), col(COL_A_G), vec(conv_a),
                  col(COL_C_X), col(COL_C_G), vec(conv_c), full(conv_c_b), vec(lru_w),
                  full(lru_ba), full(lru_bx), full(lru_lambda),
                  branch, branch,
                  pl.BlockSpec((None, 4 * BR, d), lambda i: (layer, 0, 0), pipeline_mode=pl.Buffered(1)),
                  pl.BlockSpec((None, 1, d), lambda i: (layer, 0, 2)),
                  full(ln_g), full(ln_b)],
        out_specs=pl.BlockSpec((tm, d), lambda i: (i, 0)),
        out_shape=jax.ShapeDtypeStruct((s, d), jnp.float32),
        scratch_shapes=[pltpu.VMEM((SUBLANES, BR), jnp.float32), pltpu.VMEM((SUBLANES, BR), jnp.float32),
                        pltpu.VMEM((1, BR), jnp.float32)],
        compiler_params=_params("arbitrary"),
        name="out_proj_ln",
    )(x, proj, proj, proj, proj, conv_a, proj, proj, conv_c, conv_c_b, lru_w, lru_ba, lru_bx, lru_lambda,
      yb, yd, w_out_bf16, ada, ln_g, ln_b)


def kernel(x, c, rel_bias, w_ada, b_ada, w_in, conv_a, conv_c, conv_c_b, lru_wa, lru_ba, lru_wx, lru_bx, lru_lambda, s5_lam_re, s5_lam_im, s5_log_dt, s5_b_re, s5_b_im, s5_c_re, s5_c_im, s5_d, s5_w_glu, s5_b_glu, w_out, ln_g, ln_b):
    bsz, s, d = x.shape
    assert bsz == 1 and w_in.shape == (DEPTH, d, N_IN)
    xs = x.reshape(s, d)
    w_out_bf16 = w_out.astype(jnp.bfloat16)
    w_glu_bf16 = s5_w_glu.astype(jnp.bfloat16)
    eye = jnp.eye(LRU_HEADS, dtype=jnp.float32)
    lru_w = jnp.einsum('lkhij,hg->lhikgj', jnp.stack([lru_wa, lru_wx], axis=1), eye)
    lru_w = lru_w.reshape(DEPTH, BR, 2 * BR).astype(jnp.bfloat16)
    bmat, cmat, a_re, a_im = _s5_discretize(s5_lam_re, s5_lam_im, s5_log_dt, s5_b_re, s5_b_im, s5_c_re, s5_c_im)
    ada = _ada(c.reshape(d, 1), w_ada, b_ada)

    for l in range(DEPTH):
        proj = _proj(xs, ada, w_in, l)
        yb = _attn(proj, rel_bias)
        yd = _s5(proj, bmat, a_re, a_im, cmat, s5_d, w_glu_bf16, s5_b_glu, l)
        xs = _out(xs, proj, conv_a, conv_c, conv_c_b, lru_w, lru_ba, lru_bx, lru_lambda, yb, yd, w_out_bf16, ada,
                  ln_g, ln_b, l)
    return xs.reshape(bsz, s, d)
```

```python
import functools
import math

import numpy as np
import jax
import jax.numpy as jnp
from jax import lax
from jax.experimental import pallas as pl
from jax.experimental.pallas import tpu as pltpu

BR = 512
N_IN = 12 * BR
ATT_HEADS = 8
ATT_HEAD_DIM = 64
BLK = 128
SPAN = 128
DILATIONS = (1, 4, 16)
REL_BUCKETS = 32
REL_MAX_DIST = 2048
LRU_HEADS = 8
LRU_C = 8.0
S5_CH = 16
S5_GROUPS = 32
S5_STATE = 64
DEPTH = 2
ALPHA = (2 * DEPTH) ** 0.25
LN_EPS = 1e-5

SUBLANES = 8
LANES = 128
VMEM_LIMIT = 56 * 1024 * 1024

W_COLS_AC = ((0, 4 * BR), (8 * BR, 10 * BR))
COL_Q, COL_K, COL_V, COL_B_G, COL_D_U, COL_D_G = range(6)


def _silu(x):
    return x * _sigmoid(x)


_sigmoid = jax.nn.sigmoid


def _log1p(x):
    w = 1.0 + x
    return jnp.where(w == 1.0, x, x * jnp.log(w) / (w - 1.0))


def _params(*sem):
    return pltpu.CompilerParams(dimension_semantics=sem, vmem_limit_bytes=VMEM_LIMIT)


def _delayed(x, prev_ref, max_delay):
    assert 0 < max_delay < SUBLANES
    prev = prev_ref[...]
    row = lax.broadcasted_iota(jnp.int32, prev.shape, 0)
    taps = []
    for d in range(1, max_delay + 1):
        rolled = pltpu.roll(x, d, 0)
        top = jnp.where(row < d, pltpu.roll(prev, d, 0), rolled[:SUBLANES])
        taps.append(jnp.concatenate([top, rolled[SUBLANES:]], axis=0))
    prev_ref[...] = x[x.shape[0] - SUBLANES:]
    return taps


def _shift_rows_fill(x, d, fill, period):
    assert 0 < d < period <= SUBLANES
    rolled = pltpu.roll(x, d, 0)
    row = lax.broadcasted_iota(jnp.int32, x.shape, 0) & (period - 1)
    return jnp.where(row < d, jnp.asarray(fill, x.dtype), rolled)


def _ada_kernel(c_ref, w_ref, b_ref, o_ref):
    cond = _silu(c_ref[...])
    bias = b_ref[pl.ds(pl.program_id(0), 1), :]
    o_ref[...] = jnp.sum(cond * w_ref[...], axis=0, keepdims=True) + bias


def _ada(c_col, w_ada, b_ada):
    d = c_col.shape[0]
    depth, _, n = w_ada.shape
    tn = 512
    return pl.pallas_call(
        _ada_kernel,
        grid=(depth, n // tn),
        in_specs=[
            pl.BlockSpec((d, 1), lambda l, j: (0, 0)),
            pl.BlockSpec((None, d, tn), lambda l, j: (l, 0, j)),
            pl.BlockSpec((depth, tn), lambda l, j: (0, j)),
        ],
        out_specs=pl.BlockSpec((None, 1, tn), lambda l, j: (l, 0, j)),
        out_shape=jax.ShapeDtypeStruct((depth, 1, n), jnp.float32),
        compiler_params=_params("arbitrary", "arbitrary"),
        name="ada",
    )(c_col, w_ada, b_ada)


def _proj_kernel(x_ref, shift_ref, scale_ref, w_ref, o_ref, h_ref):
    @pl.when(pl.program_id(1) == 0)
    def _():
        h = x_ref[...] * (1.0 + scale_ref[...]) + shift_ref[...]
        h_ref[...] = h.astype(jnp.bfloat16)

    o_ref[...] = jnp.dot(h_ref[...], w_ref[...].astype(jnp.bfloat16), preferred_element_type=jnp.float32)


def _proj(x, ada, w_in, layer):
    s, d = x.shape
    tm, tn = 1024, 2 * BR
    n_tiles = 3
    return pl.pallas_call(
        _proj_kernel,
        grid=(s // tm, n_tiles),
        in_specs=[
            pl.BlockSpec((tm, d), lambda i, j: (i, 0)),
            pl.BlockSpec((None, 1, d), lambda i, j: (layer, 0, 0)),
            pl.BlockSpec((None, 1, d), lambda i, j: (layer, 0, 1)),
            pl.BlockSpec((None, d, tn), lambda i, j: (layer, 0, j + 2 + j // 2)),
        ],
        out_specs=pl.BlockSpec((tm, tn), lambda i, j: (i, j)),
        out_shape=jax.ShapeDtypeStruct((s, n_tiles * tn), jnp.float32),
        scratch_shapes=[pltpu.VMEM((tm, d), jnp.bfloat16)],
        compiler_params=_params("arbitrary", "arbitrary"),
        name="proj",
    )(x, ada, ada, w_in)


ATT_TILE = max(DILATIONS) * BLK
ATT_BLOCKS = ATT_TILE // BLK
N_PAIRS = ATT_HEADS // 2
DEINT = 4
PLANE = ATT_TILE // DEINT
QROWS = BLK // DEINT
MIX_ROWS = 256
ATT_UNROLL = 16
MASKED = -1e30


def _t5_bucket_tables():
    assert DILATIONS == (1, 4, 16) and DEINT == 4
    i = np.arange(BLK)[:, None]
    j = np.arange(2 * BLK)[None, :]
    delta = i + BLK - j
    valid = (delta >= 0) & (delta <= SPAN)
    max_exact = REL_BUCKETS // 2
    tables = []
    for dil in DILATIONS:
        dist = np.clip(delta, 0, SPAN) * dil
        nf = np.maximum(dist, 1).astype(np.float32)
        large = max_exact + (np.log(nf / np.float32(max_exact)) / np.float32(math.log(REL_MAX_DIST / max_exact))
                             * np.float32(REL_BUCKETS - max_exact)).astype(np.int32)
        bucket = np.where(dist < max_exact, dist, np.minimum(large, REL_BUCKETS - 1))
        table = np.stack([np.where(valid & (j >= BLK), bucket, -1), np.where(valid, bucket, -1)])
        if dil == 1:
            rows = np.array([DEINT * a + r for r in range(DEINT) for a in range(QROWS)])
            cols = np.array([blk * BLK + DEINT * a + r
                             for r in range(DEINT) for blk in range(2) for a in range(QROWS)])
            table = table[:, rows][:, :, cols]
        tables.append(table)
    return jnp.asarray(np.stack(tables), jnp.int32)


def _attend(q, k, v, bias, low):
    q = (q * (ATT_HEAD_DIM ** -0.5)).astype(jnp.bfloat16)
    zero = jnp.zeros_like(q)
    q2 = jnp.concatenate([jnp.where(low, q, zero), jnp.where(low, zero, q)], axis=0)
    sc = lax.dot_general(q2, k.astype(jnp.bfloat16), (((1,), (1,)), ((), ())),
                         preferred_element_type=jnp.float32) + bias
    m = jnp.max(sc, axis=-1, keepdims=True)
    p = jnp.exp(sc - m)
    l = jnp.sum(p, axis=-1, keepdims=True)
    pv = jnp.dot(p.astype(jnp.bfloat16), v.astype(jnp.bfloat16), preferred_element_type=jnp.float32)
    o = jnp.where(low, pv[:BLK], pv[BLK:])
    l2 = jnp.where(low, l[:BLK], l[BLK:])
    m2 = jnp.where(low, m[:BLK], m[BLK:])
    return o / l2, m2 + jnp.log(l2)


def _attn_kernel(rb_ref, bucket_ref, q_ref, k_ref, v_ref, g_ref, y_ref,
                 q4_ref, k4_ref, v4_ref, o4_ref, lse4_ref, ynat_ref, bias_ref):
    pair = pl.program_id(0)
    t = pl.program_id(1)
    lane = lax.broadcasted_iota(jnp.int32, (BLK, LANES), 1)
    low = lane < ATT_HEAD_DIM
    n_pat = len(DILATIONS)

    @pl.when(t == 0)
    def _():
        k4_ref[:, 0:PLANE, :] = jnp.zeros((DEINT, PLANE, LANES), jnp.float32)
        v4_ref[:, 0:PLANE, :] = jnp.zeros((DEINT, PLANE, LANES), jnp.float32)
        for g in range(n_pat):
            bucket = bucket_ref[g, 1]
            no_prev = bucket_ref[g, 0] < 0
            hits = [bucket == b for b in range(REL_BUCKETS)]
            for half in range(2):
                acc = jnp.full(bucket.shape, MASKED, jnp.float32)
                for b in range(REL_BUCKETS):
                    acc = jnp.where(hits[b], rb_ref[b, 2 * pair + half], acc)
                bias_ref[g, 1, half * BLK:(half + 1) * BLK, :] = acc
                bias_ref[g, 0, half * BLK:(half + 1) * BLK, :] = jnp.where(no_prev, MASKED, acc)

    for r in range(DEINT):
        q4_ref[r] = q_ref[pl.ds(r, PLANE, stride=DEINT), :]
        k4_ref[r, PLANE:2 * PLANE, :] = k_ref[pl.ds(r, PLANE, stride=DEINT), :]
        v4_ref[r, PLANE:2 * PLANE, :] = v_ref[pl.ds(r, PLANE, stride=DEINT), :]

    def d1_qk(n):
        a0 = pl.multiple_of(n * QROWS, QROWS)
        q = jnp.concatenate([q4_ref[r, pl.ds(a0, QROWS), :] for r in range(DEINT)], axis=0)
        k = jnp.concatenate([k4_ref[r, pl.ds(PLANE - QROWS + a0, 2 * QROWS), :] for r in range(DEINT)], axis=0)
        return q, k, (t > 0) | (n > 0)

    def d1_v(n):
        a0 = pl.multiple_of(n * QROWS, QROWS)
        return jnp.concatenate([v4_ref[r, pl.ds(PLANE - QROWS + a0, 2 * QROWS), :] for r in range(DEINT)], axis=0)

    def d1_store(n, o, lse):
        a0 = pl.multiple_of(n * QROWS, QROWS)
        for r in range(DEINT):
            o4_ref[0, r, pl.ds(a0, QROWS), :] = o[r * QROWS:(r + 1) * QROWS]
            lse4_ref[0, r, pl.ds(a0, QROWS), :] = lse[r * QROWS:(r + 1) * QROWS]

    def d4_qk(b):
        r, n = b // DEINT, b % DEINT
        a0 = pl.multiple_of(n * BLK, BLK)
        return (q4_ref[r, pl.ds(a0, BLK), :], k4_ref[r, pl.ds(PLANE - BLK + a0, 2 * BLK), :],
                (t > 0) | (n > 0))

    def d4_v(b):
        r, n = b // DEINT, b % DEINT
        return v4_ref[r, pl.ds(PLANE - BLK + pl.multiple_of(n * BLK, BLK), 2 * BLK), :]

    def d4_store(b, o, lse):
        r, n = b // DEINT, b % DEINT
        a0 = pl.multiple_of(n * BLK, BLK)
        o4_ref[1, r, pl.ds(a0, BLK), :] = o
        lse4_ref[1, r, pl.ds(a0, BLK), :] = lse

    def d16_qk(b):
        lo, hi = b // DEINT, b % DEINT
        return (q4_ref[lo, pl.ds(hi, BLK, stride=DEINT), :],
                k4_ref[lo, pl.ds(hi, 2 * BLK, stride=DEINT), :], t > 0)

    def d16_v(b):
        lo, hi = b // DEINT, b % DEINT
        return v4_ref[lo, pl.ds(hi, 2 * BLK, stride=DEINT), :]

    def d16_store(b, o, lse):
        lo, hi = b // DEINT, b % DEINT
        o4_ref[2, lo, pl.ds(hi, BLK, stride=DEINT), :] = o
        lse4_ref[2, lo, pl.ds(hi, BLK, stride=DEINT), :] = lse

    for g, (qk, vload, store) in enumerate(((d1_qk, d1_v, d1_store), (d4_qk, d4_v, d4_store),
                                            (d16_qk, d16_v, d16_store))):
        def unit(u, carry, g=g, qk=qk, vload=vload, store=store):
            q, k, has_prev = qk(u)
            o, lse = _attend(q, k, vload(u), bias_ref[g, jnp.where(has_prev, 1, 0)], low)
            store(u, o, lse)
            return carry

        lax.fori_loop(0, ATT_BLOCKS, unit, 0, unroll=ATT_UNROLL)

    for r in range(DEINT):
        def mix(c, carry, r=r):
            a0 = pl.multiple_of(c * MIX_ROWS, MIX_ROWS)
            nat = pl.ds(DEINT * a0 + r, MIX_ROWS, stride=DEINT)
            a = [lse4_ref[g, r, pl.ds(a0, MIX_ROWS), :] for g in range(n_pat)]
            m = functools.reduce(jnp.maximum, a)
            e = [jnp.exp(x - m) for x in a]
            num = sum(e[g] * o4_ref[g, r, pl.ds(a0, MIX_ROWS), :] for g in range(n_pat))
            ynat_ref[nat, :] = num / sum(e) * _silu(g_ref[nat, :])
            return carry

        lax.fori_loop(0, PLANE // MIX_ROWS, mix, 0)

    y_ref[...] = ynat_ref[...].astype(y_ref.dtype)
    k4_ref[:, 0:PLANE, :] = k4_ref[:, PLANE:2 * PLANE, :]
    v4_ref[:, 0:PLANE, :] = v4_ref[:, PLANE:2 * PLANE, :]


def _attn(proj, rel_bias):
    s = proj.shape[0]
    assert s % ATT_TILE == 0
    lanes_per_col = BR // LANES
    col = lambda c: pl.BlockSpec((ATT_TILE, LANES), lambda p, t, c=c: (t, c * lanes_per_col + p))
    n_pat = len(DILATIONS)
    return pl.pallas_call(
        _attn_kernel,
        grid=(N_PAIRS, s // ATT_TILE),
        in_specs=[
            pl.BlockSpec(memory_space=pltpu.SMEM),
            pl.BlockSpec((n_pat, 2, BLK, 2 * BLK), lambda p, t: (0, 0, 0, 0)),
            col(COL_Q), col(COL_K), col(COL_V), col(COL_B_G),
        ],
        out_specs=pl.BlockSpec((ATT_TILE, LANES), lambda p, t: (t, p)),
        out_shape=jax.ShapeDtypeStruct((s, BR), jnp.bfloat16),
        scratch_shapes=[pltpu.VMEM((DEINT, PLANE, LANES), jnp.float32),
                        pltpu.VMEM((DEINT, 2 * PLANE, LANES), jnp.float32),
                        pltpu.VMEM((DEINT, 2 * PLANE, LANES), jnp.float32),
                        pltpu.VMEM((n_pat, DEINT, PLANE, LANES), jnp.float32),
                        pltpu.VMEM((n_pat, DEINT, PLANE, LANES), jnp.float32),
                        pltpu.VMEM((ATT_TILE, LANES), jnp.float32),
                        pltpu.VMEM((n_pat, 2, 2 * BLK, 2 * BLK), jnp.float32)],
        compiler_params=_params("arbitrary", "arbitrary"),
        name="dilated_attn",
    )(rel_bias, _t5_bucket_tables(), proj, proj, proj, proj)


def _rglru_tile(cx, cg, cw_ref, cb_ref, w_ref, ba_ref, bx_ref, lam_ref, hist_ref, h_ref):
    tm = cx.shape[0]
    d1, d2, d3 = _delayed(cx, hist_ref, 3)
    xc = d3 * cw_ref[0:1, :]
    xc = xc + d2 * cw_ref[1:2, :]
    xc = xc + d1 * cw_ref[2:3, :]
    xc = xc + cx * cw_ref[3:4, :]
    xc = xc + cb_ref[...]

    z = jnp.dot(xc.astype(jnp.bfloat16), w_ref[...], preferred_element_type=jnp.float32)
    r = _sigmoid(z[:, :BR] + ba_ref[...])
    ig = _sigmoid(z[:, BR:] + bx_ref[...])
    neg_lam = -lam_ref[...]
    softplus = jnp.maximum(neg_lam, 0.0) + _log1p(jnp.exp(-jnp.abs(neg_lam)))
    log_a = -LRU_C * r * softplus
    a = jnp.exp(log_a)
    b = jnp.sqrt(jnp.tanh(-log_a) * (a * a + 1.0)) * ig * xc

    d = 1
    while d < SUBLANES:
        b = b + a * _shift_rows_fill(b, d, 0.0, period=SUBLANES)
        a = a * _shift_rows_fill(a, d, 1.0, period=SUBLANES)
        d *= 2
    carry = h_ref[...]
    groups = []
    for g in range(tm // SUBLANES):
        rows = slice(g * SUBLANES, (g + 1) * SUBLANES)
        hg = b[rows] + a[rows] * carry
        carry = hg[SUBLANES - 1:]
        groups.append(hg)
    h_ref[...] = carry
    h = jnp.concatenate(groups, axis=0)
    return (h * _silu(cg)).astype(jnp.bfloat16)


N_STATE = S5_GROUPS * S5_STATE
S5_TM = 512
S5_CHUNKS = BR // LANES
S5_CHUNK_STATES = N_STATE // S5_CHUNKS
SLABS_PER_CHUNK = S5_CHUNK_STATES // LANES
N_SLABS = N_STATE // LANES
SLAB_PITCH = S5_TM + SUBLANES
S5_UNROLL = 8


def _s5_kernel(u_ref, g_ref, bmat_ref, ar_ref, ai_ref, cmat_ref, d_ref, wglu_ref, bglu_ref,
               y_ref, xs_ref, state_ref, *, layer):
    d_ref = d_ref.at[pl.ds(layer, 1)]
    bglu_ref = bglu_ref.at[pl.ds(layer, 1)]

    @pl.when(pl.program_id(0) == 0)
    def _():
        state_ref[...] = jnp.zeros_like(state_ref)

    u = u_ref[...]
    tm = u.shape[0]
    ub = u.astype(jnp.bfloat16)
    for j in range(S5_CHUNKS):
        bu = jnp.dot(ub[:, j * LANES:(j + 1) * LANES], bmat_ref[j], preferred_element_type=jnp.float32)
        for part in range(2):
            for cc in range(SLABS_PER_CHUNK):
                slab = part * N_SLABS + j * SLABS_PER_CHUNK + cc
                col = part * S5_CHUNK_STATES + cc * LANES
                xs_ref[pl.ds(slab * SLAB_PITCH, tm), :] = bu[:, col:col + LANES]

    n_vreg = N_SLABS // SUBLANES
    a_re = [ar_ref[v * SUBLANES:(v + 1) * SUBLANES, :] for v in range(n_vreg)]
    a_im = [ai_ref[v * SUBLANES:(v + 1) * SUBLANES, :] for v in range(n_vreg)]

    def rows(part, v, t):
        return pl.ds((part * N_SLABS + v * SUBLANES) * SLAB_PITCH + t, SUBLANES, stride=SLAB_PITCH)

    def step(t, x):
        new = []
        for v in range(n_vreg):
            xr, xi = x[2 * v], x[2 * v + 1]
            nr = (a_re[v] * xr + xs_ref[rows(0, v, t), :]) - a_im[v] * xi
            ni = (a_re[v] * xi + xs_ref[rows(1, v, t), :]) + a_im[v] * xr
            xs_ref[rows(0, v, t), :] = nr
            xs_ref[rows(1, v, t), :] = ni
            new += [nr, ni]
        return tuple(new)

    x0 = tuple(state_ref[i] for i in range(2 * n_vreg))
    x1 = lax.fori_loop(0, tm, step, x0, unroll=S5_UNROLL)
    for i in range(2 * n_vreg):
        state_ref[i] = x1[i]

    ys = []
    for j in range(S5_CHUNKS):
        xcat = jnp.concatenate(
            [xs_ref[pl.ds((part * N_SLABS + j * SLABS_PER_CHUNK + cc) * SLAB_PITCH, tm), :]
             for part in range(2) for cc in range(SLABS_PER_CHUNK)], axis=1)
        ys.append(jnp.dot(xcat.astype(jnp.bfloat16), cmat_ref[j], preferred_element_type=jnp.float32))
    y = jnp.concatenate(ys, axis=1) + d_ref[...] * u
    y = 0.5 * y * (1.0 + jnp.tanh(math.sqrt(2.0 / math.pi) * (y + 0.044715 * (y * y * y))))
    gate = jnp.dot(y.astype(jnp.bfloat16), wglu_ref[...], preferred_element_type=jnp.float32) + bglu_ref[...]
    y = y * _sigmoid(gate)
    y_ref[...] = (y * _silu(g_ref[...])).astype(y_ref.dtype)


def _s5_discretize(lam_re, lam_im, log_dt, b_re, b_im, c_re, c_im):
    depth = lam_re.shape[0]
    per = S5_GROUPS // S5_CHUNKS
    dt = jnp.exp(log_dt)[..., None]
    mag = jnp.exp(lam_re * dt)
    ab_re = mag * jnp.cos(lam_im * dt)
    ab_im = mag * jnp.sin(lam_im * dt)
    den = lam_re * lam_re + lam_im * lam_im
    f_re = ((ab_re - 1.0) * lam_re + ab_im * lam_im) / den
    f_im = (ab_im * lam_re - (ab_re - 1.0) * lam_im) / den
    bb_re = f_re[..., None] * b_re - f_im[..., None] * b_im
    bb_im = f_re[..., None] * b_im + f_im[..., None] * b_re
    eye = jnp.eye(per, dtype=jnp.float32)
    bb = jnp.stack([bb_re, bb_im], axis=1).reshape(depth, 2, S5_CHUNKS, per, S5_STATE, S5_CH)
    bmat = jnp.einsum('lkjgpc,gh->ljgckhp', bb, eye).reshape(depth, S5_CHUNKS, LANES, 2 * S5_CHUNK_STATES)
    cc = jnp.stack([c_re, -c_im], axis=1).reshape(depth, 2, S5_CHUNKS, per, S5_CH, S5_STATE)
    cmat = jnp.einsum('lkjgcp,gh->ljkhpgc', cc, eye).reshape(depth, S5_CHUNKS, 2 * S5_CHUNK_STATES, LANES)
    return (bmat.astype(jnp.bfloat16), cmat.astype(jnp.bfloat16),
            ab_re.reshape(depth, N_SLABS, LANES), ab_im.reshape(depth, N_SLABS, LANES))


def _s5(proj, bmat, a_re, a_im, cmat, s5_d, w_glu_bf16, b_glu, layer):
    s = proj.shape[0]
    tm = S5_TM
    full = lambda a: pl.BlockSpec(a.shape, lambda i, nd=a.ndim: (0,) * nd)
    vec = lambda a: pl.BlockSpec((None,) + a.shape[1:], lambda i, nd=a.ndim: (layer,) + (0,) * (nd - 1))
    return pl.pallas_call(
        functools.partial(_s5_kernel, layer=layer),
        grid=(s // tm,),
        in_specs=[pl.BlockSpec((tm, BR), lambda i: (i, COL_D_U)),
                  pl.BlockSpec((tm, BR), lambda i: (i, COL_D_G)),
                  vec(bmat), vec(a_re), vec(a_im), vec(cmat),
                  full(s5_d), vec(w_glu_bf16), full(b_glu)],
        out_specs=pl.BlockSpec((tm, BR), lambda i: (i, 0)),
        out_shape=jax.ShapeDtypeStruct((s, BR), jnp.bfloat16),
        scratch_shapes=[pltpu.VMEM((2 * N_SLABS * SLAB_PITCH, LANES), jnp.float32),
                        pltpu.VMEM((2 * N_SLABS // SUBLANES, SUBLANES, LANES), jnp.float32)],
        compiler_params=_params("arbitrary"),
        name="s5",
    )(proj, proj, bmat, a_re, a_im, cmat, s5_d, w_glu_bf16, b_glu)


OUT_TM = 512
OUT_SUB = 256


def _gated_conv_tile(ab, ac, ax, ag, cw_ref, hist_ref):
    u = ac * ax
    d1, d2 = _delayed(u, hist_ref, 2)
    conv = d2 * cw_ref[0:1, :]
    conv = conv + d1 * cw_ref[1:2, :]
    conv = conv + u * cw_ref[2:3, :]
    return (ab * conv * _silu(ag)).astype(jnp.bfloat16)


def _out_kernel(x_ref, shift_ref, scale_ref, wac_ref, caw_ref, ccw_ref, ccb_ref, lw_ref, lba_ref, lbx_ref, lam_ref,
                yb_ref, yd_ref, w_ref, gate_ref, g_ref, b_ref, o_ref,
                hist_a_ref, hist_c_ref, h_ref, *, layer):
    ccb_ref, lba_ref, lbx_ref, lam_ref, g_ref, b_ref = (
        r.at[pl.ds(layer, 1)] for r in (ccb_ref, lba_ref, lbx_ref, lam_ref, g_ref, b_ref))

    @pl.when(pl.program_id(0) == 0)
    def _():
        hist_a_ref[...] = jnp.zeros_like(hist_a_ref)
        hist_c_ref[...] = jnp.zeros_like(hist_c_ref)
        h_ref[...] = jnp.zeros_like(h_ref)

    def project(rows):
        x = x_ref[rows, :]
        h = (x * (1.0 + scale_ref[...]) + shift_ref[...]).astype(jnp.bfloat16)
        return jnp.dot(h, wac_ref[...], preferred_element_type=jnp.float32)

    def branches(pa):
        ab, ac, ax, ag, cx, cg = (pa[:, k * BR:(k + 1) * BR] for k in range(6))
        ya = _gated_conv_tile(ab, ac, ax, ag, caw_ref, hist_a_ref)
        yc = _rglru_tile(cx, cg, ccw_ref, ccb_ref, lw_ref, lba_ref, lbx_ref, lam_ref, hist_c_ref, h_ref)
        return ya, yc

    def mix(rows, ya, yc):
        y = jnp.dot(ya, w_ref[0 * BR:1 * BR, :], preferred_element_type=jnp.float32)
        y = y + jnp.dot(yb_ref[rows, :], w_ref[1 * BR:2 * BR, :], preferred_element_type=jnp.float32)
        y = y + jnp.dot(yc, w_ref[2 * BR:3 * BR, :], preferred_element_type=jnp.float32)
        y = y + jnp.dot(yd_ref[rows, :], w_ref[3 * BR:4 * BR, :], preferred_element_type=jnp.float32)
        return ALPHA * x_ref[rows, :] + (1.0 + gate_ref[...]) * y

    def norm(rows, z):
        mu = jnp.mean(z, axis=-1, keepdims=True)
        zc = z - mu
        var = jnp.mean(zc * zc, axis=-1, keepdims=True)
        o_ref[rows, :] = zc * lax.rsqrt(var + LN_EPS) * g_ref[...] + b_ref[...]

    subs = [slice(k * OUT_SUB, (k + 1) * OUT_SUB) for k in range(x_ref.shape[0] // OUT_SUB)]
    pa, yac, z = {}, {}, {}
    for step in range(len(subs) + 3):
        if step < len(subs):
            pa[step] = project(subs[step])
        if 0 <= step - 1 < len(subs):
            yac[step - 1] = branches(pa.pop(step - 1))
        if 0 <= step - 3 < len(subs):
            norm(subs[step - 3], z.pop(step - 3))
        if 0 <= step - 2 < len(subs):
            z[step - 2] = mix(subs[step - 2], *yac.pop(step - 2))


def _out(x, w_ac_bf16, conv_a, conv_c, conv_c_b, lru_w, lru_ba, lru_bx, lru_lambda, yb, yd, w_out_bf16, ada,
         ln_g, ln_b, layer):
    s, d = x.shape
    tm = OUT_TM
    branch = pl.BlockSpec((tm, BR), lambda i: (i, 0))
    full = lambda a: pl.BlockSpec(a.shape, lambda i, nd=a.ndim: (0,) * nd)
    vec = lambda a: pl.BlockSpec((None,) + a.shape[1:], lambda i, nd=a.ndim: (layer,) + (0,) * (nd - 1))
    resident = lambda a: pl.BlockSpec((None,) + a.shape[1:], lambda i: (layer, 0, 0),
                                      pipeline_mode=pl.Buffered(1))
    ada_third = lambda k: pl.BlockSpec((None, 1, d), lambda i: (layer, 0, k))
    return pl.pallas_call(
        functools.partial(_out_kernel, layer=layer),
        grid=(s // tm,),
        in_specs=[pl.BlockSpec((tm, d), lambda i: (i, 0)),
                  ada_third(0), ada_third(1),
                  resident(w_ac_bf16), vec(conv_a), vec(conv_c), full(conv_c_b), vec(lru_w),
                  full(lru_ba), full(lru_bx), full(lru_lambda),
                  branch, branch,
                  resident(w_out_bf16),
                  ada_third(2),
                  full(ln_g), full(ln_b)],
        out_specs=pl.BlockSpec((tm, d), lambda i: (i, 0)),
        out_shape=jax.ShapeDtypeStruct((s, d), jnp.float32),
        scratch_shapes=[pltpu.VMEM((SUBLANES, BR), jnp.float32), pltpu.VMEM((SUBLANES, BR), jnp.float32),
                        pltpu.VMEM((1, BR), jnp.float32)],
        compiler_params=_params("arbitrary"),
        name="out_proj_ln",
    )(x, ada, ada, w_ac_bf16, conv_a, conv_c, conv_c_b, lru_w, lru_ba, lru_bx, lru_lambda,
      yb, yd, w_out_bf16, ada, ln_g, ln_b)


def kernel(x, c, rel_bias, w_ada, b_ada, w_in, conv_a, conv_c, conv_c_b, lru_wa, lru_ba, lru_wx, lru_bx, lru_lambda, s5_lam_re, s5_lam_im, s5_log_dt, s5_b_re, s5_b_im, s5_c_re, s5_c_im, s5_d, s5_w_glu, s5_b_glu, w_out, ln_g, ln_b):
    bsz, s, d = x.shape
    assert bsz == 1 and w_in.shape == (DEPTH, d, N_IN)
    xs = x.reshape(s, d)
    w_out_bf16 = w_out.astype(jnp.bfloat16)
    w_ac_bf16 = jnp.concatenate([w_in[:, :, lo:hi] for lo, hi in W_COLS_AC], axis=2).astype(jnp.bfloat16)
    w_glu_bf16 = s5_w_glu.astype(jnp.bfloat16)
    eye = jnp.eye(LRU_HEADS, dtype=jnp.float32)
    lru_w = jnp.einsum('lkhij,hg->lhikgj', jnp.stack([lru_wa, lru_wx], axis=1), eye)
    lru_w = lru_w.reshape(DEPTH, BR, 2 * BR).astype(jnp.bfloat16)
    bmat, cmat, a_re, a_im = _s5_discretize(s5_lam_re, s5_lam_im, s5_log_dt, s5_b_re, s5_b_im, s5_c_re, s5_c_im)
    ada = _ada(c.reshape(d, 1), w_ada, b_ada)

    for l in range(DEPTH):
        proj = _proj(xs, ada, w_in, l)
        yb = _attn(proj, rel_bias)
        yd = _s5(proj, bmat, a_re, a_im, cmat, s5_d, w_glu_bf16, s5_b_glu, l)
        xs = _out(xs, w_ac_bf16, conv_a, conv_c, conv_c_b, lru_w, lru_ba, lru_bx, lru_lambda, yb, yd, w_out_bf16,
                  ada, ln_g, ln_b, l)
    return xs.reshape(bsz, s, d)
```

```python
import functools
import math

import numpy as np
import jax
import jax.numpy as jnp
from jax import lax
from jax.experimental import pallas as pl
from jax.experimental.pallas import tpu as pltpu

BR = 512
N_IN = 12 * BR
ATT_HEADS = 8
ATT_HEAD_DIM = 64
BLK = 128
SPAN = 128
DILATIONS = (1, 4, 16)
REL_BUCKETS = 32
REL_MAX_DIST = 2048
LRU_HEADS = 8
LRU_C = 8.0
S5_CH = 16
S5_GROUPS = 32
S5_STATE = 64
DEPTH = 2
ALPHA = (2 * DEPTH) ** 0.25
LN_EPS = 1e-5

SUBLANES = 8
LANES = 128
VMEM_LIMIT = 56 * 1024 * 1024

COL_A_B, COL_A_C, COL_A_X, COL_A_G = 0, 1, 2, 3
COL_Q, COL_K, COL_V, COL_B_G = 4, 5, 6, 7
COL_C_X, COL_C_G = 8, 9
COL_D_U, COL_D_G = 10, 11


def _silu(x):
    return x * _sigmoid(x)


_sigmoid = jax.nn.sigmoid


def _log1p(x):
    w = 1.0 + x
    return jnp.where(w == 1.0, x, x * jnp.log(w) / (w - 1.0))


def _params(*sem):
    return pltpu.CompilerParams(dimension_semantics=sem, vmem_limit_bytes=VMEM_LIMIT)


def _delayed(x, prev_ref, max_delay):
    assert 0 < max_delay < SUBLANES
    prev = prev_ref[...]
    row = lax.broadcasted_iota(jnp.int32, prev.shape, 0)
    taps = []
    for d in range(1, max_delay + 1):
        rolled = pltpu.roll(x, d, 0)
        top = jnp.where(row < d, pltpu.roll(prev, d, 0), rolled[:SUBLANES])
        taps.append(jnp.concatenate([top, rolled[SUBLANES:]], axis=0))
    prev_ref[...] = x[x.shape[0] - SUBLANES:]
    return taps


def _shift_rows_fill(x, d, fill, period):
    assert 0 < d < period <= SUBLANES
    rolled = pltpu.roll(x, d, 0)
    row = lax.broadcasted_iota(jnp.int32, x.shape, 0) & (period - 1)
    return jnp.where(row < d, jnp.asarray(fill, x.dtype), rolled)


def _ada_kernel(c_ref, w_ref, b_ref, o_ref):
    cond = _silu(c_ref[...])
    bias = b_ref[pl.ds(pl.program_id(0), 1), :]
    o_ref[...] = jnp.sum(cond * w_ref[...], axis=0, keepdims=True) + bias


def _ada(c_col, w_ada, b_ada):
    d = c_col.shape[0]
    depth, _, n = w_ada.shape
    tn = 512
    return pl.pallas_call(
        _ada_kernel,
        grid=(depth, n // tn),
        in_specs=[
            pl.BlockSpec((d, 1), lambda l, j: (0, 0)),
            pl.BlockSpec((None, d, tn), lambda l, j: (l, 0, j)),
            pl.BlockSpec((depth, tn), lambda l, j: (0, j)),
        ],
        out_specs=pl.BlockSpec((None, 1, tn), lambda l, j: (l, 0, j)),
        out_shape=jax.ShapeDtypeStruct((depth, 1, n), jnp.float32),
        compiler_params=_params("arbitrary", "arbitrary"),
        name="ada",
    )(c_col, w_ada, b_ada)


def _proj_kernel(x_ref, shift_ref, scale_ref, w_ref, o_ref, h_ref):
    @pl.when(pl.program_id(1) == 0)
    def _():
        h = x_ref[...] * (1.0 + scale_ref[...]) + shift_ref[...]
        h_ref[...] = h.astype(jnp.bfloat16)

    o_ref[...] = jnp.dot(h_ref[...], w_ref[...].astype(jnp.bfloat16), preferred_element_type=jnp.float32)


def _proj(x, ada, w_in, layer):
    s, d = x.shape
    n = w_in.shape[2]
    tm, tn = 2048, 512
    return pl.pallas_call(
        _proj_kernel,
        grid=(s // tm, n // tn),
        in_specs=[
            pl.BlockSpec((tm, d), lambda i, j: (i, 0), pipeline_mode=pl.Buffered(1)),
            pl.BlockSpec((None, 1, d), lambda i, j: (layer, 0, 0)),
            pl.BlockSpec((None, 1, d), lambda i, j: (layer, 0, 1)),
            pl.BlockSpec((None, d, tn), lambda i, j: (layer, 0, j)),
        ],
        out_specs=pl.BlockSpec((tm, tn), lambda i, j: (i, j)),
        out_shape=jax.ShapeDtypeStruct((s, n), jnp.float32),
        scratch_shapes=[pltpu.VMEM((tm, d), jnp.bfloat16)],
        compiler_params=_params("arbitrary", "arbitrary"),
        name="proj",
    )(x, ada, ada, w_in)


ATT_TILE = max(DILATIONS) * BLK
ATT_BLOCKS = ATT_TILE // BLK
N_PAIRS = ATT_HEADS // 2
DEINT = 4
PLANE = ATT_TILE // DEINT
QROWS = BLK // DEINT
MIX_ROWS = 256
ATT_UNROLL = 16
MASKED = -1e30


def _t5_bucket_tables():
    assert DILATIONS == (1, 4, 16) and DEINT == 4
    i = np.arange(BLK)[:, None]
    j = np.arange(2 * BLK)[None, :]
    delta = i + BLK - j
    valid = (delta >= 0) & (delta <= SPAN)
    max_exact = REL_BUCKETS // 2
    tables = []
    for dil in DILATIONS:
        dist = np.clip(delta, 0, SPAN) * dil
        nf = np.maximum(dist, 1).astype(np.float32)
        large = max_exact + (np.log(nf / np.float32(max_exact)) / np.float32(math.log(REL_MAX_DIST / max_exact))
                             * np.float32(REL_BUCKETS - max_exact)).astype(np.int32)
        bucket = np.where(dist < max_exact, dist, np.minimum(large, REL_BUCKETS - 1))
        table = np.stack([np.where(valid & (j >= BLK), bucket, -1), np.where(valid, bucket, -1)])
        if dil == 1:
            rows = np.array([DEINT * a + r for r in range(DEINT) for a in range(QROWS)])
            cols = np.array([blk * BLK + DEINT * a + r
                             for r in range(DEINT) for blk in range(2) for a in range(QROWS)])
            table = table[:, rows][:, :, cols]
        tables.append(table)
    return jnp.asarray(np.stack(tables), jnp.int32)


def _attend(q, k, v, bias, low):
    q = (q * (ATT_HEAD_DIM ** -0.5)).astype(jnp.bfloat16)
    zero = jnp.zeros_like(q)
    q2 = jnp.concatenate([jnp.where(low, q, zero), jnp.where(low, zero, q)], axis=0)
    sc = lax.dot_general(q2, k.astype(jnp.bfloat16), (((1,), (1,)), ((), ())),
                         preferred_element_type=jnp.float32) + bias
    m = jnp.max(sc, axis=-1, keepdims=True)
    p = jnp.exp(sc - m)
    l = jnp.sum(p, axis=-1, keepdims=True)
    pv = jnp.dot(p.astype(jnp.bfloat16), v.astype(jnp.bfloat16), preferred_element_type=jnp.float32)
    o = jnp.where(low, pv[:BLK], pv[BLK:])
    l2 = jnp.where(low, l[:BLK], l[BLK:])
    m2 = jnp.where(low, m[:BLK], m[BLK:])
    return o / l2, m2 + jnp.log(l2)


def _attn_kernel(rb_ref, bucket_ref, q_ref, k_ref, v_ref, g_ref, y_ref,
                 q4_ref, k4_ref, v4_ref, o4_ref, lse4_ref, ynat_ref, bias_ref):
    pair = pl.program_id(0)
    t = pl.program_id(1)
    lane = lax.broadcasted_iota(jnp.int32, (BLK, LANES), 1)
    low = lane < ATT_HEAD_DIM
    n_pat = len(DILATIONS)

    @pl.when(t == 0)
    def _():
        k4_ref[:, 0:PLANE, :] = jnp.zeros((DEINT, PLANE, LANES), jnp.float32)
        v4_ref[:, 0:PLANE, :] = jnp.zeros((DEINT, PLANE, LANES), jnp.float32)
        for g in range(n_pat):
            bucket = bucket_ref[g, 1]
            no_prev = bucket_ref[g, 0] < 0
            hits = [bucket == b for b in range(REL_BUCKETS)]
            for half in range(2):
                acc = jnp.full(bucket.shape, MASKED, jnp.float32)
                for b in range(REL_BUCKETS):
                    acc = jnp.where(hits[b], rb_ref[b, 2 * pair + half], acc)
                bias_ref[g, 1, half * BLK:(half + 1) * BLK, :] = acc
                bias_ref[g, 0, half * BLK:(half + 1) * BLK, :] = jnp.where(no_prev, MASKED, acc)

    for r in range(DEINT):
        q4_ref[r] = q_ref[pl.ds(r, PLANE, stride=DEINT), :]
        k4_ref[r, PLANE:2 * PLANE, :] = k_ref[pl.ds(r, PLANE, stride=DEINT), :]
        v4_ref[r, PLANE:2 * PLANE, :] = v_ref[pl.ds(r, PLANE, stride=DEINT), :]

    def d1_qk(n):
        a0 = pl.multiple_of(n * QROWS, QROWS)
        q = jnp.concatenate([q4_ref[r, pl.ds(a0, QROWS), :] for r in range(DEINT)], axis=0)
        k = jnp.concatenate([k4_ref[r, pl.ds(PLANE - QROWS + a0, 2 * QROWS), :] for r in range(DEINT)], axis=0)
        return q, k, (t > 0) | (n > 0)

    def d1_v(n):
        a0 = pl.multiple_of(n * QROWS, QROWS)
        return jnp.concatenate([v4_ref[r, pl.ds(PLANE - QROWS + a0, 2 * QROWS), :] for r in range(DEINT)], axis=0)

    def d1_store(n, o, lse):
        a0 = pl.multiple_of(n * QROWS, QROWS)
        for r in range(DEINT):
            o4_ref[0, r, pl.ds(a0, QROWS), :] = o[r * QROWS:(r + 1) * QROWS]
            lse4_ref[0, r, pl.ds(a0, QROWS), :] = lse[r * QROWS:(r + 1) * QROWS]

    def d4_qk(b):
        r, n = b // DEINT, b % DEINT
        a0 = pl.multiple_of(n * BLK, BLK)
        return (q4_ref[r, pl.ds(a0, BLK), :], k4_ref[r, pl.ds(PLANE - BLK + a0, 2 * BLK), :],
                (t > 0) | (n > 0))

    def d4_v(b):
        r, n = b // DEINT, b % DEINT
        return v4_ref[r, pl.ds(PLANE - BLK + pl.multiple_of(n * BLK, BLK), 2 * BLK), :]

    def d4_store(b, o, lse):
        r, n = b // DEINT, b % DEINT
        a0 = pl.multiple_of(n * BLK, BLK)
        o4_ref[1, r, pl.ds(a0, BLK), :] = o
        lse4_ref[1, r, pl.ds(a0, BLK), :] = lse

    def d16_qk(b):
        lo, hi = b // DEINT, b % DEINT
        return (q4_ref[lo, pl.ds(hi, BLK, stride=DEINT), :],
                k4_ref[lo, pl.ds(hi, 2 * BLK, stride=DEINT), :], t > 0)

    def d16_v(b):
        lo, hi = b // DEINT, b % DEINT
        return v4_ref[lo, pl.ds(hi, 2 * BLK, stride=DEINT), :]

    def d16_store(b, o, lse):
        lo, hi = b // DEINT, b % DEINT
        o4_ref[2, lo, pl.ds(hi, BLK, stride=DEINT), :] = o
        lse4_ref[2, lo, pl.ds(hi, BLK, stride=DEINT), :] = lse

    for g, (qk, vload, store) in enumerate(((d1_qk, d1_v, d1_store), (d4_qk, d4_v, d4_store),
                                            (d16_qk, d16_v, d16_store))):
        def unit(u, carry, g=g, qk=qk, vload=vload, store=store):
            q, k, has_prev = qk(u)
            o, lse = _attend(q, k, vload(u), bias_ref[g, jnp.where(has_prev, 1, 0)], low)
            store(u, o, lse)
            return carry

        lax.fori_loop(0, ATT_BLOCKS, unit, 0, unroll=ATT_UNROLL)

    for r in range(DEINT):
        def mix(c, carry, r=r):
            a0 = pl.multiple_of(c * MIX_ROWS, MIX_ROWS)
            nat = pl.ds(DEINT * a0 + r, MIX_ROWS, stride=DEINT)
            a = [lse4_ref[g, r, pl.ds(a0, MIX_ROWS), :] for g in range(n_pat)]
            m = functools.reduce(jnp.maximum, a)
            e = [jnp.exp(x - m) for x in a]
            num = sum(e[g] * o4_ref[g, r, pl.ds(a0, MIX_ROWS), :] for g in range(n_pat))
            ynat_ref[nat, :] = num / sum(e) * _silu(g_ref[nat, :])
            return carry

        lax.fori_loop(0, PLANE // MIX_ROWS, mix, 0)

    y_ref[...] = ynat_ref[...].astype(y_ref.dtype)
    k4_ref[:, 0:PLANE, :] = k4_ref[:, PLANE:2 * PLANE, :]
    v4_ref[:, 0:PLANE, :] = v4_ref[:, PLANE:2 * PLANE, :]


def _attn(proj, rel_bias):
    s = proj.shape[0]
    assert s % ATT_TILE == 0
    lanes_per_col = BR // LANES
    col = lambda c: pl.BlockSpec((ATT_TILE, LANES), lambda p, t, c=c: (t, c * lanes_per_col + p))
    n_pat = len(DILATIONS)
    return pl.pallas_call(
        _attn_kernel,
        grid=(N_PAIRS, s // ATT_TILE),
        in_specs=[
            pl.BlockSpec(memory_space=pltpu.SMEM),
            pl.BlockSpec((n_pat, 2, BLK, 2 * BLK), lambda p, t: (0, 0, 0, 0)),
            col(COL_Q), col(COL_K), col(COL_V), col(COL_B_G),
        ],
        out_specs=pl.BlockSpec((ATT_TILE, LANES), lambda p, t: (t, p)),
        out_shape=jax.ShapeDtypeStruct((s, BR), jnp.bfloat16),
        scratch_shapes=[pltpu.VMEM((DEINT, PLANE, LANES), jnp.float32),
                        pltpu.VMEM((DEINT, 2 * PLANE, LANES), jnp.float32),
                        pltpu.VMEM((DEINT, 2 * PLANE, LANES), jnp.float32),
                        pltpu.VMEM((n_pat, DEINT, PLANE, LANES), jnp.float32),
                        pltpu.VMEM((n_pat, DEINT, PLANE, LANES), jnp.float32),
                        pltpu.VMEM((ATT_TILE, LANES), jnp.float32),
                        pltpu.VMEM((n_pat, 2, 2 * BLK, 2 * BLK), jnp.float32)],
        compiler_params=_params("arbitrary", "arbitrary"),
        name="dilated_attn",
    )(rel_bias, _t5_bucket_tables(), proj, proj, proj, proj)


def _rglru_tile(cx, cg, cw_ref, cb_ref, w_ref, ba_ref, bx_ref, lam_ref, hist_ref, h_ref):
    tm = cx.shape[0]
    d1, d2, d3 = _delayed(cx, hist_ref, 3)
    xc = d3 * cw_ref[0:1, :]
    xc = xc + d2 * cw_ref[1:2, :]
    xc = xc + d1 * cw_ref[2:3, :]
    xc = xc + cx * cw_ref[3:4, :]
    xc = xc + cb_ref[...]

    z = jnp.dot(xc.astype(jnp.bfloat16), w_ref[...], preferred_element_type=jnp.float32)
    r = _sigmoid(z[:, :BR] + ba_ref[...])
    ig = _sigmoid(z[:, BR:] + bx_ref[...])
    neg_lam = -lam_ref[...]
    softplus = jnp.maximum(neg_lam, 0.0) + _log1p(jnp.exp(-jnp.abs(neg_lam)))
    log_a = -LRU_C * r * softplus
    a = jnp.exp(log_a)
    b = jnp.sqrt(jnp.tanh(-log_a) * (a * a + 1.0)) * ig * xc

    d = 1
    while d < SUBLANES:
        b = b + a * _shift_rows_fill(b, d, 0.0, period=SUBLANES)
        a = a * _shift_rows_fill(a, d, 1.0, period=SUBLANES)
        d *= 2
    carry = h_ref[...]
    groups = []
    for g in range(tm // SUBLANES):
        rows = slice(g * SUBLANES, (g + 1) * SUBLANES)
        hg = b[rows] + a[rows] * carry
        carry = hg[SUBLANES - 1:]
        groups.append(hg)
    h_ref[...] = carry
    h = jnp.concatenate(groups, axis=0)
    return (h * _silu(cg)).astype(jnp.bfloat16)


N_STATE = S5_GROUPS * S5_STATE
S5_TM = 512
S5_CHUNKS = BR // LANES
S5_CHUNK_STATES = N_STATE // S5_CHUNKS
SLABS_PER_CHUNK = S5_CHUNK_STATES // LANES
N_SLABS = N_STATE // LANES
SLAB_PITCH = S5_TM + SUBLANES
S5_UNROLL = 8


def _s5_kernel(u_ref, g_ref, bmat_ref, ar_ref, ai_ref, cmat_ref, d_ref, wglu_ref, bglu_ref,
               y_ref, xs_ref, state_ref, *, layer):
    d_ref = d_ref.at[pl.ds(layer, 1)]
    bglu_ref = bglu_ref.at[pl.ds(layer, 1)]

    @pl.when(pl.program_id(0) == 0)
    def _():
        state_ref[...] = jnp.zeros_like(state_ref)

    u = u_ref[...]
    tm = u.shape[0]
    ub = u.astype(jnp.bfloat16)
    for j in range(S5_CHUNKS):
        bu = jnp.dot(ub[:, j * LANES:(j + 1) * LANES], bmat_ref[j], preferred_element_type=jnp.float32)
        for part in range(2):
            for cc in range(SLABS_PER_CHUNK):
                slab = part * N_SLABS + j * SLABS_PER_CHUNK + cc
                col = part * S5_CHUNK_STATES + cc * LANES
                xs_ref[pl.ds(slab * SLAB_PITCH, tm), :] = bu[:, col:col + LANES]

    n_vreg = N_SLABS // SUBLANES
    a_re = [ar_ref[v * SUBLANES:(v + 1) * SUBLANES, :] for v in range(n_vreg)]
    a_im = [ai_ref[v * SUBLANES:(v + 1) * SUBLANES, :] for v in range(n_vreg)]

    def rows(part, v, t):
        return pl.ds((part * N_SLABS + v * SUBLANES) * SLAB_PITCH + t, SUBLANES, stride=SLAB_PITCH)

    def step(t, x):
        new = []
        for v in range(n_vreg):
            xr, xi = x[2 * v], x[2 * v + 1]
            nr = (a_re[v] * xr + xs_ref[rows(0, v, t), :]) - a_im[v] * xi
            ni = (a_re[v] * xi + xs_ref[rows(1, v, t), :]) + a_im[v] * xr
            xs_ref[rows(0, v, t), :] = nr
            xs_ref[rows(1, v, t), :] = ni
            new += [nr, ni]
        return tuple(new)

    x0 = tuple(state_ref[i] for i in range(2 * n_vreg))
    x1 = lax.fori_loop(0, tm, step, x0, unroll=S5_UNROLL)
    for i in range(2 * n_vreg):
        state_ref[i] = x1[i]

    ys = []
    for j in range(S5_CHUNKS):
        xcat = jnp.concatenate(
            [xs_ref[pl.ds((part * N_SLABS + j * SLABS_PER_CHUNK + cc) * SLAB_PITCH, tm), :]
             for part in range(2) for cc in range(SLABS_PER_CHUNK)], axis=1)
        ys.append(jnp.dot(xcat.astype(jnp.bfloat16), cmat_ref[j], preferred_element_type=jnp.float32))
    y = jnp.concatenate(ys, axis=1) + d_ref[...] * u
    y = 0.5 * y * (1.0 + jnp.tanh(math.sqrt(2.0 / math.pi) * (y + 0.044715 * (y * y * y))))
    gate = jnp.dot(y.astype(jnp.bfloat16), wglu_ref[...], preferred_element_type=jnp.float32) + bglu_ref[...]
    y = y * _sigmoid(gate)
    y_ref[...] = (y * _silu(g_ref[...])).astype(y_ref.dtype)


def _s5_discretize(lam_re, lam_im, log_dt, b_re, b_im, c_re, c_im):
    depth = lam_re.shape[0]
    per = S5_GROUPS // S5_CHUNKS
    dt = jnp.exp(log_dt)[..., None]
    mag = jnp.exp(lam_re * dt)
    ab_re = mag * jnp.cos(lam_im * dt)
    ab_im = mag * jnp.sin(lam_im * dt)
    den = lam_re * lam_re + lam_im * lam_im
    f_re = ((ab_re - 1.0) * lam_re + ab_im * lam_im) / den
    f_im = (ab_im * lam_re - (ab_re - 1.0) * lam_im) / den
    bb_re = f_re[..., None] * b_re - f_im[..., None] * b_im
    bb_im = f_re[..., None] * b_im + f_im[..., None] * b_re
    eye = jnp.eye(per, dtype=jnp.float32)
    bb = jnp.stack([bb_re, bb_im], axis=1).reshape(depth, 2, S5_CHUNKS, per, S5_STATE, S5_CH)
    bmat = jnp.einsum('lkjgpc,gh->ljgckhp', bb, eye).reshape(depth, S5_CHUNKS, LANES, 2 * S5_CHUNK_STATES)
    cc = jnp.stack([c_re, -c_im], axis=1).reshape(depth, 2, S5_CHUNKS, per, S5_CH, S5_STATE)
    cmat = jnp.einsum('lkjgcp,gh->ljkhpgc', cc, eye).reshape(depth, S5_CHUNKS, 2 * S5_CHUNK_STATES, LANES)
    return (bmat.astype(jnp.bfloat16), cmat.astype(jnp.bfloat16),
            ab_re.reshape(depth, N_SLABS, LANES), ab_im.reshape(depth, N_SLABS, LANES))


def _s5(proj, bmat, a_re, a_im, cmat, s5_d, w_glu_bf16, b_glu, layer):
    s = proj.shape[0]
    tm = S5_TM
    full = lambda a: pl.BlockSpec(a.shape, lambda i, nd=a.ndim: (0,) * nd)
    vec = lambda a: pl.BlockSpec((None,) + a.shape[1:], lambda i, nd=a.ndim: (layer,) + (0,) * (nd - 1))
    return pl.pallas_call(
        functools.partial(_s5_kernel, layer=layer),
        grid=(s // tm,),
        in_specs=[pl.BlockSpec((tm, BR), lambda i: (i, COL_D_U)),
                  pl.BlockSpec((tm, BR), lambda i: (i, COL_D_G)),
                  vec(bmat), vec(a_re), vec(a_im), vec(cmat),
                  full(s5_d), vec(w_glu_bf16), full(b_glu)],
        out_specs=pl.BlockSpec((tm, BR), lambda i: (i, 0)),
        out_shape=jax.ShapeDtypeStruct((s, BR), jnp.bfloat16),
        scratch_shapes=[pltpu.VMEM((2 * N_SLABS * SLAB_PITCH, LANES), jnp.float32),
                        pltpu.VMEM((2 * N_SLABS // SUBLANES, SUBLANES, LANES), jnp.float32)],
        compiler_params=_params("arbitrary"),
        name="s5",
    )(proj, proj, bmat, a_re, a_im, cmat, s5_d, w_glu_bf16, b_glu)


OUT_TM = 512
OUT_SUB = 256


def _gated_conv_tile(ab, ac, ax, ag, cw_ref, hist_ref):
    u = ac * ax
    d1, d2 = _delayed(u, hist_ref, 2)
    conv = d2 * cw_ref[0:1, :]
    conv = conv + d1 * cw_ref[1:2, :]
    conv = conv + u * cw_ref[2:3, :]
    return (ab * conv * _silu(ag)).astype(jnp.bfloat16)


def _out_kernel(x_ref, ab_ref, ac_ref, ax_ref, ag_ref, caw_ref,
                cx_ref, cg_ref, ccw_ref, ccb_ref, lw_ref, lba_ref, lbx_ref, lam_ref,
                yb_ref, yd_ref, w_ref, gate_ref, g_ref, b_ref, o_ref,
                hist_a_ref, hist_c_ref, h_ref, *, layer):
    ccb_ref, lba_ref, lbx_ref, lam_ref, g_ref, b_ref = (
        r.at[pl.ds(layer, 1)] for r in (ccb_ref, lba_ref, lbx_ref, lam_ref, g_ref, b_ref))

    @pl.when(pl.program_id(0) == 0)
    def _():
        hist_a_ref[...] = jnp.zeros_like(hist_a_ref)
        hist_c_ref[...] = jnp.zeros_like(hist_c_ref)
        h_ref[...] = jnp.zeros_like(h_ref)

    for sub in range(x_ref.shape[0] // OUT_SUB):
        rows = slice(sub * OUT_SUB, (sub + 1) * OUT_SUB)
        ya = _gated_conv_tile(ab_ref[rows, :], ac_ref[rows, :], ax_ref[rows, :], ag_ref[rows, :],
                              caw_ref, hist_a_ref)
        yc = _rglru_tile(cx_ref[rows, :], cg_ref[rows, :], ccw_ref, ccb_ref, lw_ref, lba_ref, lbx_ref,
                         lam_ref, hist_c_ref, h_ref)
        y = jnp.dot(ya, w_ref[0 * BR:1 * BR, :], preferred_element_type=jnp.float32)
        y = y + jnp.dot(yb_ref[rows, :], w_ref[1 * BR:2 * BR, :], preferred_element_type=jnp.float32)
        y = y + jnp.dot(yc, w_ref[2 * BR:3 * BR, :], preferred_element_type=jnp.float32)
        y = y + jnp.dot(yd_ref[rows, :], w_ref[3 * BR:4 * BR, :], preferred_element_type=jnp.float32)
        z = ALPHA * x_ref[rows, :] + (1.0 + gate_ref[...]) * y
        mu = jnp.mean(z, axis=-1, keepdims=True)
        zc = z - mu
        var = jnp.mean(zc * zc, axis=-1, keepdims=True)
        o_ref[rows, :] = zc * lax.rsqrt(var + LN_EPS) * g_ref[...] + b_ref[...]


def _out(x, proj, conv_a, conv_c, conv_c_b, lru_w, lru_ba, lru_bx, lru_lambda, yb, yd, w_out_bf16, ada,
         ln_g, ln_b, layer):
    s, d = x.shape
    tm = OUT_TM
    col = lambda c: pl.BlockSpec((tm, BR), lambda i, c=c: (i, c))
    branch = pl.BlockSpec((tm, BR), lambda i: (i, 0))
    full = lambda a: pl.BlockSpec(a.shape, lambda i, nd=a.ndim: (0,) * nd)
    vec = lambda a: pl.BlockSpec((None,) + a.shape[1:], lambda i, nd=a.ndim: (layer,) + (0,) * (nd - 1))
    return pl.pallas_call(
        functools.partial(_out_kernel, layer=layer),
        grid=(s // tm,),
        in_specs=[pl.BlockSpec((tm, d), lambda i: (i, 0)),
                  col(COL_A_B), col(COL_A_C), col(COL_A_X), col(COL_A_G), vec(conv_a),
                  col(COL_C_X), col(COL_C_G), vec(conv_c), full(conv_c_b), vec(lru_w),
                  full(lru_ba), full(lru_bx), full(lru_lambda),
                  branch, branch,
                  pl.BlockSpec((None, 4 * BR, d), lambda i: (layer, 0, 0), pipeline_mode=pl.Buffered(1)),
                  pl.BlockSpec((None, 1, d), lambda i: (layer, 0, 2)),
                  full(ln_g), full(ln_b)],
        out_specs=pl.BlockSpec((tm, d), lambda i: (i, 0)),
        out_shape=jax.ShapeDtypeStruct((s, d), jnp.float32),
        scratch_shapes=[pltpu.VMEM((SUBLANES, BR), jnp.float32), pltpu.VMEM((SUBLANES, BR), jnp.float32),
                        pltpu.VMEM((1, BR), jnp.float32)],
        compiler_params=_params("arbitrary"),
        name="out_proj_ln",
    )(x, proj, proj, proj, proj, conv_a, proj, proj, conv_c, conv_c_b, lru_w, lru_ba, lru_bx, lru_lambda,
      yb, yd, w_out_bf16, ada, ln_g, ln_b)


def kernel(x, c, rel_bias, w_ada, b_ada, w_in, conv_a, conv_c, conv_c_b, lru_wa, lru_ba, lru_wx, lru_bx, lru_lambda, s5_lam_re, s5_lam_im, s5_log_dt, s5_b_re, s5_b_im, s5_c_re, s5_c_im, s5_d, s5_w_glu, s5_b_glu, w_out, ln_g, ln_b):
    bsz, s, d = x.shape
    assert bsz == 1 and w_in.shape == (DEPTH, d, N_IN)
    xs = x.reshape(s, d)
    w_out_bf16 = w_out.astype(jnp.bfloat16)
    w_glu_bf16 = s5_w_glu.astype(jnp.bfloat16)
    eye = jnp.eye(LRU_HEADS, dtype=jnp.float32)
    lru_w = jnp.einsum('lkhij,hg->lhikgj', jnp.stack([lru_wa, lru_wx], axis=1), eye)
    lru_w = lru_w.reshape(DEPTH, BR, 2 * BR).astype(jnp.bfloat16)
    bmat, cmat, a_re, a_im = _s5_discretize(s5_lam_re, s5_lam_im, s5_log_dt, s5_b_re, s5_b_im, s5_c_re, s5_c_im)
    ada = _ada(c.reshape(d, 1), w_ada, b_ada)

    for l in range(DEPTH):
        proj = _proj(xs, ada, w_in, l)
        yb = _attn(proj, rel_bias)
        yd = _s5(proj, bmat, a_re, a_im, cmat, s5_d, w_glu_bf16, s5_b_glu, l)
        xs = _out(xs, proj, conv_a, conv_c, conv_c_b, lru_w, lru_ba, lru_bx, lru_lambda, yb, yd, w_out_bf16, ada,
                  ln_g, ln_b, l)
    return xs.reshape(bsz, s, d)
```

```python
import functools
import math

import numpy as np
import jax
import jax.numpy as jnp
from jax import lax
from jax.experimental import pallas as pl
from jax.experimental.pallas import tpu as pltpu

BR = 512
N_IN = 12 * BR
ATT_HEADS = 8
ATT_HEAD_DIM = 64
BLK = 128
SPAN = 128
DILATIONS = (1, 4, 16)
REL_BUCKETS = 32
REL_MAX_DIST = 2048
LRU_HEADS = 8
LRU_C = 8.0
S5_CH = 16
S5_GROUPS = 32
S5_STATE = 64
DEPTH = 2
ALPHA = (2 * DEPTH) ** 0.25
LN_EPS = 1e-5

SUBLANES = 8
LANES = 128
VMEM_LIMIT = 56 * 1024 * 1024

COL_A_B, COL_A_C, COL_A_X, COL_A_G = 0, 1, 2, 3
COL_Q, COL_K, COL_V, COL_B_G = 4, 5, 6, 7
COL_C_X, COL_C_G = 8, 9
COL_D_U, COL_D_G = 10, 11


def _silu(x):
    return x * _sigmoid(x)


_sigmoid = jax.nn.sigmoid


def _log1p(x):
    w = 1.0 + x
    return jnp.where(w == 1.0, x, x * jnp.log(w) / (w - 1.0))


def _params(*sem):
    return pltpu.CompilerParams(dimension_semantics=sem, vmem_limit_bytes=VMEM_LIMIT)


def _delayed(x, prev_ref, max_delay):
    assert 0 < max_delay < SUBLANES
    prev = prev_ref[...]
    row = lax.broadcasted_iota(jnp.int32, prev.shape, 0)
    taps = []
    for d in range(1, max_delay + 1):
        rolled = pltpu.roll(x, d, 0)
        top = jnp.where(row < d, pltpu.roll(prev, d, 0), rolled[:SUBLANES])
        taps.append(jnp.concatenate([top, rolled[SUBLANES:]], axis=0))
    prev_ref[...] = x[x.shape[0] - SUBLANES:]
    return taps


def _shift_rows_fill(x, d, fill, period):
    assert 0 < d < period <= SUBLANES
    rolled = pltpu.roll(x, d, 0)
    row = lax.broadcasted_iota(jnp.int32, x.shape, 0) & (period - 1)
    return jnp.where(row < d, jnp.asarray(fill, x.dtype), rolled)


def _ada_kernel(c_ref, w_ref, b_ref, o_ref):
    cond = _silu(c_ref[...])
    bias = b_ref[pl.ds(pl.program_id(0), 1), :]
    o_ref[...] = jnp.sum(cond * w_ref[...], axis=0, keepdims=True) + bias


def _ada(c_col, w_ada, b_ada):
    d = c_col.shape[0]
    depth, _, n = w_ada.shape
    tn = 512
    return pl.pallas_call(
        _ada_kernel,
        grid=(depth, n // tn),
        in_specs=[
            pl.BlockSpec((d, 1), lambda l, j: (0, 0)),
            pl.BlockSpec((None, d, tn), lambda l, j: (l, 0, j)),
            pl.BlockSpec((depth, tn), lambda l, j: (0, j)),
        ],
        out_specs=pl.BlockSpec((None, 1, tn), lambda l, j: (l, 0, j)),
        out_shape=jax.ShapeDtypeStruct((depth, 1, n), jnp.float32),
        compiler_params=_params("arbitrary", "arbitrary"),
        name="ada",
    )(c_col, w_ada, b_ada)


def _proj_kernel(x_ref, shift_ref, scale_ref, w_ref, o_ref, h_ref):
    @pl.when(pl.program_id(1) == 0)
    def _():
        h = x_ref[...] * (1.0 + scale_ref[...]) + shift_ref[...]
        h_ref[...] = h.astype(jnp.bfloat16)

    o_ref[...] = jnp.dot(h_ref[...], w_ref[...].astype(jnp.bfloat16), preferred_element_type=jnp.float32)


def _proj(x, ada, w_in, layer):
    s, d = x.shape
    n = w_in.shape[2]
    tm, tn = 2048, 512
    return pl.pallas_call(
        _proj_kernel,
        grid=(s // tm, n // tn),
        in_specs=[
            pl.BlockSpec((tm, d), lambda i, j: (i, 0), pipeline_mode=pl.Buffered(1)),
            pl.BlockSpec((None, 1, d), lambda i, j: (layer, 0, 0)),
            pl.BlockSpec((None, 1, d), lambda i, j: (layer, 0, 1)),
            pl.BlockSpec((None, d, tn), lambda i, j: (layer, 0, j)),
        ],
        out_specs=pl.BlockSpec((tm, tn), lambda i, j: (i, j)),
        out_shape=jax.ShapeDtypeStruct((s, n), jnp.float32),
        scratch_shapes=[pltpu.VMEM((tm, d), jnp.bfloat16)],
        compiler_params=_params("arbitrary", "arbitrary"),
        name="proj",
    )(x, ada, ada, w_in)


ATT_TILE = max(DILATIONS) * BLK
ATT_BLOCKS = ATT_TILE // BLK
N_PAIRS = ATT_HEADS // 2
DEINT = 4
PLANE = ATT_TILE // DEINT
QROWS = BLK // DEINT
MIX_ROWS = 256
ATT_UNROLL = 16
MASKED = -1e30


def _t5_bucket_tables():
    assert DILATIONS == (1, 4, 16) and DEINT == 4
    i = np.arange(BLK)[:, None]
    j = np.arange(2 * BLK)[None, :]
    delta = i + BLK - j
    valid = (delta >= 0) & (delta <= SPAN)
    max_exact = REL_BUCKETS // 2
    tables = []
    for dil in DILATIONS:
        dist = np.clip(delta, 0, SPAN) * dil
        nf = np.maximum(dist, 1).astype(np.float32)
        large = max_exact + (np.log(nf / np.float32(max_exact)) / np.float32(math.log(REL_MAX_DIST / max_exact))
                             * np.float32(REL_BUCKETS - max_exact)).astype(np.int32)
        bucket = np.where(dist < max_exact, dist, np.minimum(large, REL_BUCKETS - 1))
        table = np.stack([np.where(valid & (j >= BLK), bucket, -1), np.where(valid, bucket, -1)])
        if dil == 1:
            rows = np.array([DEINT * a + r for r in range(DEINT) for a in range(QROWS)])
            cols = np.array([blk * BLK + DEINT * a + r
                             for r in range(DEINT) for blk in range(2) for a in range(QROWS)])
            table = table[:, rows][:, :, cols]
        tables.append(table)
    return jnp.asarray(np.stack(tables), jnp.int32)


def _attend(q, k, v, bias, low):
    q = (q * (ATT_HEAD_DIM ** -0.5)).astype(jnp.bfloat16)
    zero = jnp.zeros_like(q)
    q2 = jnp.concatenate([jnp.where(low, q, zero), jnp.where(low, zero, q)], axis=0)
    sc = lax.dot_general(q2, k.astype(jnp.bfloat16), (((1,), (1,)), ((), ())),
                         preferred_element_type=jnp.float32) + bias
    m = jnp.max(sc, axis=-1, keepdims=True)
    p = jnp.exp(sc - m)
    l = jnp.sum(p, axis=-1, keepdims=True)
    pv = jnp.dot(p.astype(jnp.bfloat16), v.astype(jnp.bfloat16), preferred_element_type=jnp.float32)
    return (jnp.where(low, pv[:BLK], pv[BLK:]), jnp.where(low, m[:BLK], m[BLK:]),
            jnp.where(low, l[:BLK], l[BLK:]))


def _attn_kernel(rb_ref, bucket_ref, q_ref, k_ref, v_ref, g_ref, y_ref,
                 q4_ref, k4_ref, v4_ref, o4_ref, m4_ref, l4_ref, ynat_ref, bias_ref):
    pair = pl.program_id(0)
    t = pl.program_id(1)
    lane = lax.broadcasted_iota(jnp.int32, (BLK, LANES), 1)
    low = lane < ATT_HEAD_DIM
    n_pat = len(DILATIONS)

    @pl.when(t == 0)
    def _():
        k4_ref[:, 0:PLANE, :] = jnp.zeros((DEINT, PLANE, LANES), jnp.float32)
        v4_ref[:, 0:PLANE, :] = jnp.zeros((DEINT, PLANE, LANES), jnp.float32)
        for g in range(n_pat):
            bucket = bucket_ref[g, 1]
            no_prev = bucket_ref[g, 0] < 0
            hits = [bucket == b for b in range(REL_BUCKETS)]
            for half in range(2):
                acc = jnp.full(bucket.shape, MASKED, jnp.float32)
                for b in range(REL_BUCKETS):
                    acc = jnp.where(hits[b], rb_ref[b, 2 * pair + half], acc)
                bias_ref[g, 1, half * BLK:(half + 1) * BLK, :] = acc
                bias_ref[g, 0, half * BLK:(half + 1) * BLK, :] = jnp.where(no_prev, MASKED, acc)

    for r in range(DEINT):
        q4_ref[r] = q_ref[pl.ds(r, PLANE, stride=DEINT), :]
        k4_ref[r, PLANE:2 * PLANE, :] = k_ref[pl.ds(r, PLANE, stride=DEINT), :]
        v4_ref[r, PLANE:2 * PLANE, :] = v_ref[pl.ds(r, PLANE, stride=DEINT), :]

    def d1_qk(n):
        a0 = pl.multiple_of(n * QROWS, QROWS)
        q = jnp.concatenate([q4_ref[r, pl.ds(a0, QROWS), :] for r in range(DEINT)], axis=0)
        k = jnp.concatenate([k4_ref[r, pl.ds(PLANE - QROWS + a0, 2 * QROWS), :] for r in range(DEINT)], axis=0)
        return q, k, (t > 0) | (n > 0)

    def d1_v(n):
        a0 = pl.multiple_of(n * QROWS, QROWS)
        return jnp.concatenate([v4_ref[r, pl.ds(PLANE - QROWS + a0, 2 * QROWS), :] for r in range(DEINT)], axis=0)

    stats_refs = (o4_ref, m4_ref, l4_ref)

    def d1_store(n, stats):
        a0 = pl.multiple_of(n * QROWS, QROWS)
        for ref, val in zip(stats_refs, stats):
            for r in range(DEINT):
                ref[0, r, pl.ds(a0, QROWS), :] = val[r * QROWS:(r + 1) * QROWS]

    def d4_qk(b):
        r, n = b // DEINT, b % DEINT
        a0 = pl.multiple_of(n * BLK, BLK)
        return (q4_ref[r, pl.ds(a0, BLK), :], k4_ref[r, pl.ds(PLANE - BLK + a0, 2 * BLK), :],
                (t > 0) | (n > 0))

    def d4_v(b):
        r, n = b // DEINT, b % DEINT
        return v4_ref[r, pl.ds(PLANE - BLK + pl.multiple_of(n * BLK, BLK), 2 * BLK), :]

    def d4_store(b, stats):
        r, n = b // DEINT, b % DEINT
        a0 = pl.multiple_of(n * BLK, BLK)
        for ref, val in zip(stats_refs, stats):
            ref[1, r, pl.ds(a0, BLK), :] = val

    def d16_qk(b):
        lo, hi = b // DEINT, b % DEINT
        return (q4_ref[lo, pl.ds(hi, BLK, stride=DEINT), :],
                k4_ref[lo, pl.ds(hi, 2 * BLK, stride=DEINT), :], t > 0)

    def d16_v(b):
        lo, hi = b // DEINT, b % DEINT
        return v4_ref[lo, pl.ds(hi, 2 * BLK, stride=DEINT), :]

    def d16_store(b, stats):
        lo, hi = b // DEINT, b % DEINT
        for ref, val in zip(stats_refs, stats):
            ref[2, lo, pl.ds(hi, BLK, stride=DEINT), :] = val

    for g, (qk, vload, store) in enumerate(((d1_qk, d1_v, d1_store), (d4_qk, d4_v, d4_store),
                                            (d16_qk, d16_v, d16_store))):
        def unit(u, carry, g=g, qk=qk, vload=vload, store=store):
            q, k, has_prev = qk(u)
            store(u, _attend(q, k, vload(u), bias_ref[g, jnp.where(has_prev, 1, 0)], low))
            return carry

        lax.fori_loop(0, ATT_BLOCKS, unit, 0, unroll=ATT_UNROLL)

    for r in range(DEINT):
        def mix(c, carry, r=r):
            a0 = pl.multiple_of(c * MIX_ROWS, MIX_ROWS)
            nat = pl.ds(DEINT * a0 + r, MIX_ROWS, stride=DEINT)
            rows = pl.ds(a0, MIX_ROWS)
            ms = [m4_ref[g, r, rows, :] for g in range(n_pat)]
            m = functools.reduce(jnp.maximum, ms)
            e = [jnp.exp(x - m) for x in ms]
            num = sum(e[g] * o4_ref[g, r, rows, :] for g in range(n_pat))
            den = sum(e[g] * l4_ref[g, r, rows, :] for g in range(n_pat))
            ynat_ref[nat, :] = num / den * _silu(g_ref[nat, :])
            return carry

        lax.fori_loop(0, PLANE // MIX_ROWS, mix, 0)

    y_ref[...] = ynat_ref[...].astype(y_ref.dtype)
    k4_ref[:, 0:PLANE, :] = k4_ref[:, PLANE:2 * PLANE, :]
    v4_ref[:, 0:PLANE, :] = v4_ref[:, PLANE:2 * PLANE, :]


def _attn(proj, rel_bias):
    s = proj.shape[0]
    assert s % ATT_TILE == 0
    lanes_per_col = BR // LANES
    col = lambda c: pl.BlockSpec((ATT_TILE, LANES), lambda p, t, c=c: (t, c * lanes_per_col + p))
    n_pat = len(DILATIONS)
    return pl.pallas_call(
        _attn_kernel,
        grid=(N_PAIRS, s // ATT_TILE),
        in_specs=[
            pl.BlockSpec(memory_space=pltpu.SMEM),
            pl.BlockSpec((n_pat, 2, BLK, 2 * BLK), lambda p, t: (0, 0, 0, 0)),
            col(COL_Q), col(COL_K), col(COL_V), col(COL_B_G),
        ],
        out_specs=pl.BlockSpec((ATT_TILE, LANES), lambda p, t: (t, p)),
        out_shape=jax.ShapeDtypeStruct((s, BR), jnp.bfloat16),
        scratch_shapes=[pltpu.VMEM((DEINT, PLANE, LANES), jnp.float32),
                        pltpu.VMEM((DEINT, 2 * PLANE, LANES), jnp.float32),
                        pltpu.VMEM((DEINT, 2 * PLANE, LANES), jnp.float32),
                        pltpu.VMEM((n_pat, DEINT, PLANE, LANES), jnp.float32),
                        pltpu.VMEM((n_pat, DEINT, PLANE, LANES), jnp.float32),
                        pltpu.VMEM((n_pat, DEINT, PLANE, LANES), jnp.float32),
                        pltpu.VMEM((ATT_TILE, LANES), jnp.float32),
                        pltpu.VMEM((n_pat, 2, 2 * BLK, 2 * BLK), jnp.float32)],
        compiler_params=_params("arbitrary", "arbitrary"),
        name="dilated_attn",
    )(rel_bias, _t5_bucket_tables(), proj, proj, proj, proj)


def _rglru_tile(cx, cg, cw_ref, cb_ref, w_ref, ba_ref, bx_ref, lam_ref, hist_ref, h_ref):
    tm = cx.shape[0]
    d1, d2, d3 = _delayed(cx, hist_ref, 3)
    xc = d3 * cw_ref[0:1, :]
    xc = xc + d2 * cw_ref[1:2, :]
    xc = xc + d1 * cw_ref[2:3, :]
    xc = xc + cx * cw_ref[3:4, :]
    xc = xc + cb_ref[...]

    z = jnp.dot(xc.astype(jnp.bfloat16), w_ref[...], preferred_element_type=jnp.float32)
    r = _sigmoid(z[:, :BR] + ba_ref[...])
    ig = _sigmoid(z[:, BR:] + bx_ref[...])
    neg_lam = -lam_ref[...]
    softplus = jnp.maximum(neg_lam, 0.0) + _log1p(jnp.exp(-jnp.abs(neg_lam)))
    log_a = -LRU_C * r * softplus
    a = jnp.exp(log_a)
    b = jnp.sqrt(jnp.tanh(-log_a) * (a * a + 1.0)) * ig * xc

    d = 1
    while d < SUBLANES:
        b = b + a * _shift_rows_fill(b, d, 0.0, period=SUBLANES)
        a = a * _shift_rows_fill(a, d, 1.0, period=SUBLANES)
        d *= 2
    carry = h_ref[...]
    groups = []
    for g in range(tm // SUBLANES):
        rows = slice(g * SUBLANES, (g + 1) * SUBLANES)
        hg = b[rows] + a[rows] * carry
        carry = hg[SUBLANES - 1:]
        groups.append(hg)
    h_ref[...] = carry
    h = jnp.concatenate(groups, axis=0)
    return (h * _silu(cg)).astype(jnp.bfloat16)


N_STATE = S5_GROUPS * S5_STATE
S5_TM = 512
S5_CHUNKS = BR // LANES
S5_CHUNK_STATES = N_STATE // S5_CHUNKS
SLABS_PER_CHUNK = S5_CHUNK_STATES // LANES
N_SLABS = N_STATE // LANES
SLAB_PITCH = S5_TM + SUBLANES
S5_UNROLL = 8


def _s5_kernel(u_ref, g_ref, bmat_ref, ar_ref, ai_ref, cmat_ref, d_ref, wglu_ref, bglu_ref,
               y_ref, xs_ref, state_ref, *, layer):
    d_ref = d_ref.at[pl.ds(layer, 1)]
    bglu_ref = bglu_ref.at[pl.ds(layer, 1)]

    @pl.when(pl.program_id(0) == 0)
    def _():
        state_ref[...] = jnp.zeros_like(state_ref)

    u = u_ref[...]
    tm = u.shape[0]
    ub = u.astype(jnp.bfloat16)
    for j in range(S5_CHUNKS):
        bu = jnp.dot(ub[:, j * LANES:(j + 1) * LANES], bmat_ref[j], preferred_element_type=jnp.float32)
        for part in range(2):
            for cc in range(SLABS_PER_CHUNK):
                slab = part * N_SLABS + j * SLABS_PER_CHUNK + cc
                col = part * S5_CHUNK_STATES + cc * LANES
                xs_ref[pl.ds(slab * SLAB_PITCH, tm), :] = bu[:, col:col + LANES]

    n_vreg = N_SLABS // SUBLANES
    a_re = [ar_ref[v * SUBLANES:(v + 1) * SUBLANES, :] for v in range(n_vreg)]
    a_im = [ai_ref[v * SUBLANES:(v + 1) * SUBLANES, :] for v in range(n_vreg)]

    def rows(part, v, t):
        return pl.ds((part * N_SLABS + v * SUBLANES) * SLAB_PITCH + t, SUBLANES, stride=SLAB_PITCH)

    def step(t, x):
        new = []
        for v in range(n_vreg):
            xr, xi = x[2 * v], x[2 * v + 1]
            nr = (a_re[v] * xr + xs_ref[rows(0, v, t), :]) - a_im[v] * xi
            ni = (a_re[v] * xi + xs_ref[rows(1, v, t), :]) + a_im[v] * xr
            xs_ref[rows(0, v, t), :] = nr
            xs_ref[rows(1, v, t), :] = ni
            new += [nr, ni]
        return tuple(new)

    x0 = tuple(state_ref[i] for i in range(2 * n_vreg))
    x1 = lax.fori_loop(0, tm, step, x0, unroll=S5_UNROLL)
    for i in range(2 * n_vreg):
        state_ref[i] = x1[i]

    ys = []
    for j in range(S5_CHUNKS):
        xcat = jnp.concatenate(
            [xs_ref[pl.ds((part * N_SLABS + j * SLABS_PER_CHUNK + cc) * SLAB_PITCH, tm), :]
             for part in range(2) for cc in range(SLABS_PER_CHUNK)], axis=1)
        ys.append(jnp.dot(xcat.astype(jnp.bfloat16), cmat_ref[j], preferred_element_type=jnp.float32))
    y = jnp.concatenate(ys, axis=1) + d_ref[...] * u
    y = 0.5 * y * (1.0 + jnp.tanh(math.sqrt(2.0 / math.pi) * (y + 0.044715 * (y * y * y))))
    gate = jnp.dot(y.astype(jnp.bfloat16), wglu_ref[...], preferred_element_type=jnp.float32) + bglu_ref[...]
    y = y * _sigmoid(gate)
    y_ref[...] = (y * _silu(g_ref[...])).astype(y_ref.dtype)


def _s5_discretize(lam_re, lam_im, log_dt, b_re, b_im, c_re, c_im):
    depth = lam_re.shape[0]
    per = S5_GROUPS // S5_CHUNKS
    dt = jnp.exp(log_dt)[..., None]
    mag = jnp.exp(lam_re * dt)
    ab_re = mag * jnp.cos(lam_im * dt)
    ab_im = mag * jnp.sin(lam_im * dt)
    den = lam_re * lam_re + lam_im * lam_im
    f_re = ((ab_re - 1.0) * lam_re + ab_im * lam_im) / den
    f_im = (ab_im * lam_re - (ab_re - 1.0) * lam_im) / den
    bb_re = f_re[..., None] * b_re - f_im[..., None] * b_im
    bb_im = f_re[..., None] * b_im + f_im[..., None] * b_re
    eye = jnp.eye(per, dtype=jnp.float32)
    bb = jnp.stack([bb_re, bb_im], axis=1).reshape(depth, 2, S5_CHUNKS, per, S5_STATE, S5_CH)
    bmat = jnp.einsum('lkjgpc,gh->ljgckhp', bb, eye).reshape(depth, S5_CHUNKS, LANES, 2 * S5_CHUNK_STATES)
    cc = jnp.stack([c_re, -c_im], axis=1).reshape(depth, 2, S5_CHUNKS, per, S5_CH, S5_STATE)
    cmat = jnp.einsum('lkjgcp,gh->ljkhpgc', cc, eye).reshape(depth, S5_CHUNKS, 2 * S5_CHUNK_STATES, LANES)
    return (bmat.astype(jnp.bfloat16), cmat.astype(jnp.bfloat16),
            ab_re.reshape(depth, N_SLABS, LANES), ab_im.reshape(depth, N_SLABS, LANES))


def _s5(proj, bmat, a_re, a_im, cmat, s5_d, w_glu_bf16, b_glu, layer):
    s = proj.shape[0]
    tm = S5_TM
    full = lambda a: pl.BlockSpec(a.shape, lambda i, nd=a.ndim: (0,) * nd)
    vec = lambda a: pl.BlockSpec((None,) + a.shape[1:], lambda i, nd=a.ndim: (layer,) + (0,) * (nd - 1))
    return pl.pallas_call(
        functools.partial(_s5_kernel, layer=layer),
        grid=(s // tm,),
        in_specs=[pl.BlockSpec((tm, BR), lambda i: (i, COL_D_U)),
                  pl.BlockSpec((tm, BR), lambda i: (i, COL_D_G)),
                  vec(bmat), vec(a_re), vec(a_im), vec(cmat),
                  full(s5_d), vec(w_glu_bf16), full(b_glu)],
        out_specs=pl.BlockSpec((tm, BR), lambda i: (i, 0)),
        out_shape=jax.ShapeDtypeStruct((s, BR), jnp.bfloat16),
        scratch_shapes=[pltpu.VMEM((2 * N_SLABS * SLAB_PITCH, LANES), jnp.float32),
                        pltpu.VMEM((2 * N_SLABS // SUBLANES, SUBLANES, LANES), jnp.float32)],
        compiler_params=_params("arbitrary"),
        name="s5",
    )(proj, proj, bmat, a_re, a_im, cmat, s5_d, w_glu_bf16, b_glu)


OUT_TM = 512
OUT_SUB = 256


def _gated_conv_tile(ab, ac, ax, ag, cw_ref, hist_ref):
    u = ac * ax
    d1, d2 = _delayed(u, hist_ref, 2)
    conv = d2 * cw_ref[0:1, :]
    conv = conv + d1 * cw_ref[1:2, :]
    conv = conv + u * cw_ref[2:3, :]
    return (ab * conv * _silu(ag)).astype(jnp.bfloat16)


def _out_kernel(x_ref, ab_ref, ac_ref, ax_ref, ag_ref, caw_ref,
                cx_ref, cg_ref, ccw_ref, ccb_ref, lw_ref, lba_ref, lbx_ref, lam_ref,
                yb_ref, yd_ref, w_ref, gate_ref, g_ref, b_ref, o_ref,
                hist_a_ref, hist_c_ref, h_ref, *, layer):
    ccb_ref, lba_ref, lbx_ref, lam_ref, g_ref, b_ref = (
        r.at[pl.ds(layer, 1)] for r in (ccb_ref, lba_ref, lbx_ref, lam_ref, g_ref, b_ref))

    @pl.when(pl.program_id(0) == 0)
    def _():
        hist_a_ref[...] = jnp.zeros_like(hist_a_ref)
        hist_c_ref[...] = jnp.zeros_like(hist_c_ref)
        h_ref[...] = jnp.zeros_like(h_ref)

    for sub in range(x_ref.shape[0] // OUT_SUB):
        rows = slice(sub * OUT_SUB, (sub + 1) * OUT_SUB)
        ya = _gated_conv_tile(ab_ref[rows, :], ac_ref[rows, :], ax_ref[rows, :], ag_ref[rows, :],
                              caw_ref, hist_a_ref)
        yc = _rglru_tile(cx_ref[rows, :], cg_ref[rows, :], ccw_ref, ccb_ref, lw_ref, lba_ref, lbx_ref,
                         lam_ref, hist_c_ref, h_ref)
        y = jnp.dot(ya, w_ref[0 * BR:1 * BR, :], preferred_element_type=jnp.float32)
        y = y + jnp.dot(yb_ref[rows, :], w_ref[1 * BR:2 * BR, :], preferred_element_type=jnp.float32)
        y = y + jnp.dot(yc, w_ref[2 * BR:3 * BR, :], preferred_element_type=jnp.float32)
        y = y + jnp.dot(yd_ref[rows, :], w_ref[3 * BR:4 * BR, :], preferred_element_type=jnp.float32)
        z = ALPHA * x_ref[rows, :] + (1.0 + gate_ref[...]) * y
        mu = jnp.mean(z, axis=-1, keepdims=True)
        zc = z - mu
        var = jnp.mean(zc * zc, axis=-1, keepdims=True)
        o_ref[rows, :] = zc * lax.rsqrt(var + LN_EPS) * g_ref[...] + b_ref[...]


def _out(x, proj, conv_a, conv_c, conv_c_b, lru_w, lru_ba, lru_bx, lru_lambda, yb, yd, w_out_bf16, ada,
         ln_g, ln_b, layer):
    s, d = x.shape
    tm = OUT_TM
    col = lambda c: pl.BlockSpec((tm, BR), lambda i, c=c: (i, c))
    branch = pl.BlockSpec((tm, BR), lambda i: (i, 0))
    full = lambda a: pl.BlockSpec(a.shape, lambda i, nd=a.ndim: (0,) * nd)
    vec = lambda a: pl.BlockSpec((None,) + a.shape[1:], lambda i, nd=a.ndim: (layer,) + (0,) * (nd - 1))
    return pl.pallas_call(
        functools.partial(_out_kernel, layer=layer),
        grid=(s // tm,),
        in_specs=[pl.BlockSpec((tm, d), lambda i: (i, 0)),
                  col(COL_A_B), col(COL_A_C), col(COL_A_X), col(COL_A_G), vec(conv_a),
                  col(COL_C_X), col(COL_C_G), vec(conv_c), full(conv_c_b), vec(lru_w),
                  full(lru_ba), full(lru_bx), full(lru_lambda),
                  branch, branch,
                  pl.BlockSpec((None, 4 * BR, d), lambda i: (layer, 0, 0), pipeline_mode=pl.Buffered(1)),
                  pl.BlockSpec((None, 1, d), lambda i: (layer, 0, 2)),
                  full(ln_g), full(ln_b)],
        out_specs=pl.BlockSpec((tm, d), lambda i: (i, 0)),
        out_shape=jax.ShapeDtypeStruct((s, d), jnp.float32),
        scratch_shapes=[pltpu.VMEM((SUBLANES, BR), jnp.float32), pltpu.VMEM((SUBLANES, BR), jnp.float32),
                        pltpu.VMEM((1, BR), jnp.float32)],
        compiler_params=_params("arbitrary"),
        name="out_proj_ln",
    )(x, proj, proj, proj, proj, conv_a, proj, proj, conv_c, conv_c_b, lru_w, lru_ba, lru_bx, lru_lambda,
      yb, yd, w_out_bf16, ada, ln_g, ln_b)


def kernel(x, c, rel_bias, w_ada, b_ada, w_in, conv_a, conv_c, conv_c_b, lru_wa, lru_ba, lru_wx, lru_bx, lru_lambda, s5_lam_re, s5_lam_im, s5_log_dt, s5_b_re, s5_b_im, s5_c_re, s5_c_im, s5_d, s5_w_glu, s5_b_glu, w_out, ln_g, ln_b):
    bsz, s, d = x.shape
    assert bsz == 1 and w_in.shape == (DEPTH, d, N_IN)
    xs = x.reshape(s, d)
    w_out_bf16 = w_out.astype(jnp.bfloat16)
    w_glu_bf16 = s5_w_glu.astype(jnp.bfloat16)
    eye = jnp.eye(LRU_HEADS, dtype=jnp.float32)
    lru_w = jnp.einsum('lkhij,hg->lhikgj', jnp.stack([lru_wa, lru_wx], axis=1), eye)
    lru_w = lru_w.reshape(DEPTH, BR, 2 * BR).astype(jnp.bfloat16)
    bmat, cmat, a_re, a_im = _s5_discretize(s5_lam_re, s5_lam_im, s5_log_dt, s5_b_re, s5_b_im, s5_c_re, s5_c_im)
    ada = _ada(c.reshape(d, 1), w_ada, b_ada)

    for l in range(DEPTH):
        proj = _proj(xs, ada, w_in, l)
        yb = _attn(proj, rel_bias)
        yd = _s5(proj, bmat, a_re, a_im, cmat, s5_d, w_glu_bf16, s5_b_glu, l)
        xs = _out(xs, proj, conv_a, conv_c, conv_c_b, lru_w, lru_ba, lru_bx, lru_lambda, yb, yd, w_out_bf16, ada,
                  ln_g, ln_b, l)
    return xs.reshape(bsz, s, d)
```

```python
import functools
import math

import numpy as np
import jax
import jax.numpy as jnp
from jax import lax
from jax.experimental import pallas as pl
from jax.experimental.pallas import tpu as pltpu

BR = 512
N_IN = 12 * BR
ATT_HEADS = 8
ATT_HEAD_DIM = 64
BLK = 128
SPAN = 128
DILATIONS = (1, 4, 16)
REL_BUCKETS = 32
REL_MAX_DIST = 2048
LRU_HEADS = 8
LRU_C = 8.0
S5_CH = 16
S5_GROUPS = 32
S5_STATE = 64
DEPTH = 2
ALPHA = (2 * DEPTH) ** 0.25
LN_EPS = 1e-5

SUBLANES = 8
LANES = 128
VMEM_LIMIT = 56 * 1024 * 1024

COL_A_B, COL_A_C, COL_A_X, COL_A_G = 0, 1, 2, 3
COL_Q, COL_K, COL_V, COL_B_G = 4, 5, 6, 7
COL_C_X, COL_C_G = 8, 9
COL_D_U, COL_D_G = 10, 11


def _silu(x):
    return x * _sigmoid(x)


_sigmoid = jax.nn.sigmoid


def _log1p(x):
    w = 1.0 + x
    return jnp.where(w == 1.0, x, x * jnp.log(w) / (w - 1.0))


def _params(*sem):
    return pltpu.CompilerParams(dimension_semantics=sem, vmem_limit_bytes=VMEM_LIMIT)


def _delayed(x, prev_ref, max_delay):
    assert 0 < max_delay < SUBLANES
    prev = prev_ref[...]
    row = lax.broadcasted_iota(jnp.int32, prev.shape, 0)
    taps = []
    for d in range(1, max_delay + 1):
        rolled = pltpu.roll(x, d, 0)
        top = jnp.where(row < d, pltpu.roll(prev, d, 0), rolled[:SUBLANES])
        taps.append(jnp.concatenate([top, rolled[SUBLANES:]], axis=0))
    prev_ref[...] = x[x.shape[0] - SUBLANES:]
    return taps


def _shift_rows_fill(x, d, fill, period):
    assert 0 < d < period <= SUBLANES
    rolled = pltpu.roll(x, d, 0)
    row = lax.broadcasted_iota(jnp.int32, x.shape, 0) & (period - 1)
    return jnp.where(row < d, jnp.asarray(fill, x.dtype), rolled)


def _ada_kernel(c_ref, w_ref, b_ref, o_ref):
    cond = _silu(c_ref[...])
    bias = b_ref[pl.ds(pl.program_id(0), 1), :]
    o_ref[...] = jnp.sum(cond * w_ref[...], axis=0, keepdims=True) + bias


def _ada(c_col, w_ada, b_ada):
    d = c_col.shape[0]
    depth, _, n = w_ada.shape
    tn = 512
    return pl.pallas_call(
        _ada_kernel,
        grid=(depth, n // tn),
        in_specs=[
            pl.BlockSpec((d, 1), lambda l, j: (0, 0)),
            pl.BlockSpec((None, d, tn), lambda l, j: (l, 0, j)),
            pl.BlockSpec((depth, tn), lambda l, j: (0, j)),
        ],
        out_specs=pl.BlockSpec((None, 1, tn), lambda l, j: (l, 0, j)),
        out_shape=jax.ShapeDtypeStruct((depth, 1, n), jnp.float32),
        compiler_params=_params("arbitrary", "arbitrary"),
        name="ada",
    )(c_col, w_ada, b_ada)


def _proj_kernel(x_ref, shift_ref, scale_ref, w_ref, o_ref, h_ref):
    @pl.when(pl.program_id(1) == 0)
    def _():
        h = x_ref[...] * (1.0 + scale_ref[...]) + shift_ref[...]
        h_ref[...] = h.astype(jnp.bfloat16)

    o_ref[...] = jnp.dot(h_ref[...], w_ref[...].astype(jnp.bfloat16), preferred_element_type=jnp.float32)


def _proj(x, ada, w_in, layer):
    s, d = x.shape
    n = w_in.shape[2]
    tm, tn = 2048, 512
    return pl.pallas_call(
        _proj_kernel,
        grid=(s // tm, n // tn),
        in_specs=[
            pl.BlockSpec((tm, d), lambda i, j: (i, 0), pipeline_mode=pl.Buffered(1)),
            pl.BlockSpec((None, 1, d), lambda i, j: (layer, 0, 0)),
            pl.BlockSpec((None, 1, d), lambda i, j: (layer, 0, 1)),
            pl.BlockSpec((None, d, tn), lambda i, j: (layer, 0, j)),
        ],
        out_specs=pl.BlockSpec((tm, tn), lambda i, j: (i, j)),
        out_shape=jax.ShapeDtypeStruct((s, n), jnp.float32),
        scratch_shapes=[pltpu.VMEM((tm, d), jnp.bfloat16)],
        compiler_params=_params("arbitrary", "arbitrary"),
        name="proj",
    )(x, ada, ada, w_in)


ATT_TILE = max(DILATIONS) * BLK
ATT_BLOCKS = ATT_TILE // BLK
N_PAIRS = ATT_HEADS // 2
DEINT = 4
PLANE = ATT_TILE // DEINT
QROWS = BLK // DEINT
MIX_ROWS = 256
ATT_UNROLL = 16
MASKED = -1e30
LOG2E = math.log2(math.e)


def _t5_bucket_tables():
    assert DILATIONS == (1, 4, 16) and DEINT == 4
    i = np.arange(BLK)[:, None]
    j = np.arange(2 * BLK)[None, :]
    delta = i + BLK - j
    valid = (delta >= 0) & (delta <= SPAN)
    max_exact = REL_BUCKETS // 2
    tables = []
    for dil in DILATIONS:
        dist = np.clip(delta, 0, SPAN) * dil
        nf = np.maximum(dist, 1).astype(np.float32)
        large = max_exact + (np.log(nf / np.float32(max_exact)) / np.float32(math.log(REL_MAX_DIST / max_exact))
                             * np.float32(REL_BUCKETS - max_exact)).astype(np.int32)
        bucket = np.where(dist < max_exact, dist, np.minimum(large, REL_BUCKETS - 1))
        table = np.stack([np.where(valid & (j >= BLK), bucket, -1), np.where(valid, bucket, -1)])
        if dil == 1:
            rows = np.array([DEINT * a + r for r in range(DEINT) for a in range(QROWS)])
            cols = np.array([blk * BLK + DEINT * a + r
                             for r in range(DEINT) for blk in range(2) for a in range(QROWS)])
            table = table[:, rows][:, :, cols]
        tables.append(table)
    return jnp.asarray(np.stack(tables), jnp.int32)


def _attend(q, k, v, bias, low):
    q = (q * (LOG2E * ATT_HEAD_DIM ** -0.5)).astype(jnp.bfloat16)
    zero = jnp.zeros_like(q)
    q2 = jnp.concatenate([jnp.where(low, q, zero), jnp.where(low, zero, q)], axis=0)
    sc = lax.dot_general(q2, k.astype(jnp.bfloat16), (((1,), (1,)), ((), ())),
                         preferred_element_type=jnp.float32) + bias
    m = jnp.max(sc, axis=-1, keepdims=True)
    p = jnp.exp2(sc - m)
    l = jnp.sum(p, axis=-1, keepdims=True)
    pv = jnp.dot(p.astype(jnp.bfloat16), v.astype(jnp.bfloat16), preferred_element_type=jnp.float32)
    return (jnp.where(low, pv[:BLK], pv[BLK:]), jnp.where(low, m[:BLK], m[BLK:]),
            jnp.where(low, l[:BLK], l[BLK:]))


def _attn_kernel(rb_ref, bucket_ref, q_ref, k_ref, v_ref, g_ref, y_ref,
                 q4_ref, k4_ref, v4_ref, o4_ref, m4_ref, l4_ref, ynat_ref, bias_ref):
    pair = pl.program_id(0)
    t = pl.program_id(1)
    lane = lax.broadcasted_iota(jnp.int32, (BLK, LANES), 1)
    low = lane < ATT_HEAD_DIM
    n_pat = len(DILATIONS)

    @pl.when(t == 0)
    def _():
        k4_ref[:, 0:PLANE, :] = jnp.zeros((DEINT, PLANE, LANES), jnp.float32)
        v4_ref[:, 0:PLANE, :] = jnp.zeros((DEINT, PLANE, LANES), jnp.float32)
        for g in range(n_pat):
            bucket = bucket_ref[g, 1]
            no_prev = bucket_ref[g, 0] < 0
            hits = [bucket == b for b in range(REL_BUCKETS)]
            for half in range(2):
                acc = jnp.full(bucket.shape, MASKED, jnp.float32)
                for b in range(REL_BUCKETS):
                    acc = jnp.where(hits[b], LOG2E * rb_ref[b, 2 * pair + half], acc)
                bias_ref[g, 1, half * BLK:(half + 1) * BLK, :] = acc
                bias_ref[g, 0, half * BLK:(half + 1) * BLK, :] = jnp.where(no_prev, MASKED, acc)

    for r in range(DEINT):
        q4_ref[r] = q_ref[pl.ds(r, PLANE, stride=DEINT), :]
        k4_ref[r, PLANE:2 * PLANE, :] = k_ref[pl.ds(r, PLANE, stride=DEINT), :]
        v4_ref[r, PLANE:2 * PLANE, :] = v_ref[pl.ds(r, PLANE, stride=DEINT), :]

    def d1_qk(n):
        a0 = pl.multiple_of(n * QROWS, QROWS)
        q = jnp.concatenate([q4_ref[r, pl.ds(a0, QROWS), :] for r in range(DEINT)], axis=0)
        k = jnp.concatenate([k4_ref[r, pl.ds(PLANE - QROWS + a0, 2 * QROWS), :] for r in range(DEINT)], axis=0)
        return q, k, (t > 0) | (n > 0)

    def d1_v(n):
        a0 = pl.multiple_of(n * QROWS, QROWS)
        return jnp.concatenate([v4_ref[r, pl.ds(PLANE - QROWS + a0, 2 * QROWS), :] for r in range(DEINT)], axis=0)

    stats_refs = (o4_ref, m4_ref, l4_ref)

    def d1_store(n, stats):
        a0 = pl.multiple_of(n * QROWS, QROWS)
        for ref, val in zip(stats_refs, stats):
            for r in range(DEINT):
                ref[0, r, pl.ds(a0, QROWS), :] = val[r * QROWS:(r + 1) * QROWS]

    def d4_qk(b):
        r, n = b // DEINT, b % DEINT
        a0 = pl.multiple_of(n * BLK, BLK)
        return (q4_ref[r, pl.ds(a0, BLK), :], k4_ref[r, pl.ds(PLANE - BLK + a0, 2 * BLK), :],
                (t > 0) | (n > 0))

    def d4_v(b):
        r, n = b // DEINT, b % DEINT
        return v4_ref[r, pl.ds(PLANE - BLK + pl.multiple_of(n * BLK, BLK), 2 * BLK), :]

    def d4_store(b, stats):
        r, n = b // DEINT, b % DEINT
        a0 = pl.multiple_of(n * BLK, BLK)
        for ref, val in zip(stats_refs, stats):
            ref[1, r, pl.ds(a0, BLK), :] = val

    def d16_qk(b):
        lo, hi = b // DEINT, b % DEINT
        return (q4_ref[lo, pl.ds(hi, BLK, stride=DEINT), :],
                k4_ref[lo, pl.ds(hi, 2 * BLK, stride=DEINT), :], t > 0)

    def d16_v(b):
        lo, hi = b // DEINT, b % DEINT
        return v4_ref[lo, pl.ds(hi, 2 * BLK, stride=DEINT), :]

    def d16_store(b, stats):
        lo, hi = b // DEINT, b % DEINT
        for ref, val in zip(stats_refs, stats):
            ref[2, lo, pl.ds(hi, BLK, stride=DEINT), :] = val

    for g, (qk, vload, store) in enumerate(((d1_qk, d1_v, d1_store), (d4_qk, d4_v, d4_store),
                                            (d16_qk, d16_v, d16_store))):
        def unit(u, carry, g=g, qk=qk, vload=vload, store=store):
            q, k, has_prev = qk(u)
            store(u, _attend(q, k, vload(u), bias_ref[g, jnp.where(has_prev, 1, 0)], low))
            return carry

        lax.fori_loop(0, ATT_BLOCKS, unit, 0, unroll=ATT_UNROLL)

    for r in range(DEINT):
        def mix(c, carry, r=r):
            a0 = pl.multiple_of(c * MIX_ROWS, MIX_ROWS)
            nat = pl.ds(DEINT * a0 + r, MIX_ROWS, stride=DEINT)
            rows = pl.ds(a0, MIX_ROWS)
            ms = [m4_ref[g, r, rows, :] for g in range(n_pat)]
            m = functools.reduce(jnp.maximum, ms)
            e = [jnp.exp2(x - m) for x in ms]
            num = sum(e[g] * o4_ref[g, r, rows, :] for g in range(n_pat))
            den = sum(e[g] * l4_ref[g, r, rows, :] for g in range(n_pat))
            ynat_ref[nat, :] = num / den * _silu(g_ref[nat, :])
            return carry

        lax.fori_loop(0, PLANE // MIX_ROWS, mix, 0)

    y_ref[...] = ynat_ref[...].astype(y_ref.dtype)
    k4_ref[:, 0:PLANE, :] = k4_ref[:, PLANE:2 * PLANE, :]
    v4_ref[:, 0:PLANE, :] = v4_ref[:, PLANE:2 * PLANE, :]


def _attn(proj, rel_bias):
    s = proj.shape[0]
    assert s % ATT_TILE == 0
    lanes_per_col = BR // LANES
    col = lambda c: pl.BlockSpec((ATT_TILE, LANES), lambda p, t, c=c: (t, c * lanes_per_col + p))
    n_pat = len(DILATIONS)
    return pl.pallas_call(
        _attn_kernel,
        grid=(N_PAIRS, s // ATT_TILE),
        in_specs=[
            pl.BlockSpec(memory_space=pltpu.SMEM),
            pl.BlockSpec((n_pat, 2, BLK, 2 * BLK), lambda p, t: (0, 0, 0, 0)),
            col(COL_Q), col(COL_K), col(COL_V), col(COL_B_G),
        ],
        out_specs=pl.BlockSpec((ATT_TILE, LANES), lambda p, t: (t, p)),
        out_shape=jax.ShapeDtypeStruct((s, BR), jnp.bfloat16),
        scratch_shapes=[pltpu.VMEM((DEINT, PLANE, LANES), jnp.float32),
                        pltpu.VMEM((DEINT, 2 * PLANE, LANES), jnp.float32),
                        pltpu.VMEM((DEINT, 2 * PLANE, LANES), jnp.float32),
                        pltpu.VMEM((n_pat, DEINT, PLANE, LANES), jnp.float32),
                        pltpu.VMEM((n_pat, DEINT, PLANE, LANES), jnp.float32),
                        pltpu.VMEM((n_pat, DEINT, PLANE, LANES), jnp.float32),
                        pltpu.VMEM((ATT_TILE, LANES), jnp.float32),
                        pltpu.VMEM((n_pat, 2, 2 * BLK, 2 * BLK), jnp.float32)],
        compiler_params=_params("arbitrary", "arbitrary"),
        name="dilated_attn",
    )(rel_bias, _t5_bucket_tables(), proj, proj, proj, proj)


def _rglru_tile(cx, cg, cw_ref, cb_ref, w_ref, ba_ref, bx_ref, lam_ref, hist_ref, h_ref):
    tm = cx.shape[0]
    d1, d2, d3 = _delayed(cx, hist_ref, 3)
    xc = d3 * cw_ref[0:1, :]
    xc = xc + d2 * cw_ref[1:2, :]
    xc = xc + d1 * cw_ref[2:3, :]
    xc = xc + cx * cw_ref[3:4, :]
    xc = xc + cb_ref[...]

    z = jnp.dot(xc.astype(jnp.bfloat16), w_ref[...], preferred_element_type=jnp.float32)
    r = _sigmoid(z[:, :BR] + ba_ref[...])
    ig = _sigmoid(z[:, BR:] + bx_ref[...])
    neg_lam = -lam_ref[...]
    softplus = jnp.maximum(neg_lam, 0.0) + _log1p(jnp.exp(-jnp.abs(neg_lam)))
    log_a = -LRU_C * r * softplus
    a = jnp.exp(log_a)
    b = jnp.sqrt(jnp.tanh(-log_a) * (a * a + 1.0)) * ig * xc

    d = 1
    while d < SUBLANES:
        b = b + a * _shift_rows_fill(b, d, 0.0, period=SUBLANES)
        a = a * _shift_rows_fill(a, d, 1.0, period=SUBLANES)
        d *= 2
    carry = h_ref[...]
    groups = []
    for g in range(tm // SUBLANES):
        rows = slice(g * SUBLANES, (g + 1) * SUBLANES)
        hg = b[rows] + a[rows] * carry
        carry = hg[SUBLANES - 1:]
        groups.append(hg)
    h_ref[...] = carry
    h = jnp.concatenate(groups, axis=0)
    return (h * _silu(cg)).astype(jnp.bfloat16)


N_STATE = S5_GROUPS * S5_STATE
S5_TM = 512
S5_CHUNKS = BR // LANES
S5_CHUNK_STATES = N_STATE // S5_CHUNKS
SLABS_PER_CHUNK = S5_CHUNK_STATES // LANES
N_SLABS = N_STATE // LANES
SLAB_PITCH = S5_TM + SUBLANES
S5_UNROLL = 8


def _s5_kernel(u_ref, g_ref, bmat_ref, ar_ref, ai_ref, cmat_ref, d_ref, wglu_ref, bglu_ref,
               y_ref, xs_ref, state_ref, *, layer):
    d_ref = d_ref.at[pl.ds(layer, 1)]
    bglu_ref = bglu_ref.at[pl.ds(layer, 1)]

    @pl.when(pl.program_id(0) == 0)
    def _():
        state_ref[...] = jnp.zeros_like(state_ref)

    u = u_ref[...]
    tm = u.shape[0]
    ub = u.astype(jnp.bfloat16)
    for j in range(S5_CHUNKS):
        bu = jnp.dot(ub[:, j * LANES:(j + 1) * LANES], bmat_ref[j], preferred_element_type=jnp.float32)
        for part in range(2):
            for cc in range(SLABS_PER_CHUNK):
                slab = part * N_SLABS + j * SLABS_PER_CHUNK + cc
                col = part * S5_CHUNK_STATES + cc * LANES
                xs_ref[pl.ds(slab * SLAB_PITCH, tm), :] = bu[:, col:col + LANES]

    n_vreg = N_SLABS // SUBLANES
    a_re = [ar_ref[v * SUBLANES:(v + 1) * SUBLANES, :] for v in range(n_vreg)]
    a_im = [ai_ref[v * SUBLANES:(v + 1) * SUBLANES, :] for v in range(n_vreg)]

    def rows(part, v, t):
        return pl.ds((part * N_SLABS + v * SUBLANES) * SLAB_PITCH + t, SUBLANES, stride=SLAB_PITCH)

    def step(t, x):
        new = []
        for v in range(n_vreg):
            xr, xi = x[2 * v], x[2 * v + 1]
            nr = (a_re[v] * xr + xs_ref[rows(0, v, t), :]) - a_im[v] * xi
            ni = (a_re[v] * xi + xs_ref[rows(1, v, t), :]) + a_im[v] * xr
            xs_ref[rows(0, v, t), :] = nr
            xs_ref[rows(1, v, t), :] = ni
            new += [nr, ni]
        return tuple(new)

    x0 = tuple(state_ref[i] for i in range(2 * n_vreg))
    x1 = lax.fori_loop(0, tm, step, x0, unroll=S5_UNROLL)
    for i in range(2 * n_vreg):
        state_ref[i] = x1[i]

    ys = []
    for j in range(S5_CHUNKS):
        xcat = jnp.concatenate(
            [xs_ref[pl.ds((part * N_SLABS + j * SLABS_PER_CHUNK + cc) * SLAB_PITCH, tm), :]
             for part in range(2) for cc in range(SLABS_PER_CHUNK)], axis=1)
        ys.append(jnp.dot(xcat.astype(jnp.bfloat16), cmat_ref[j], preferred_element_type=jnp.float32))
    y = jnp.concatenate(ys, axis=1) + d_ref[...] * u
    y = 0.5 * y * (1.0 + jnp.tanh(math.sqrt(2.0 / math.pi) * (y + 0.044715 * (y * y * y))))
    gate = jnp.dot(y.astype(jnp.bfloat16), wglu_ref[...], preferred_element_type=jnp.float32) + bglu_ref[...]
    y = y * _sigmoid(gate)
    y_ref[...] = (y * _silu(g_ref[...])).astype(y_ref.dtype)


def _s5_discretize(lam_re, lam_im, log_dt, b_re, b_im, c_re, c_im):
    depth = lam_re.shape[0]
    per = S5_GROUPS // S5_CHUNKS
    dt = jnp.exp(log_dt)[..., None]
    mag = jnp.exp(lam_re * dt)
    ab_re = mag * jnp.cos(lam_im * dt)
    ab_im = mag * jnp.sin(lam_im * dt)
    den = lam_re * lam_re + lam_im * lam_im
    f_re = ((ab_re - 1.0) * lam_re + ab_im * lam_im) / den
    f_im = (ab_im * lam_re - (ab_re - 1.0) * lam_im) / den
    bb_re = f_re[..., None] * b_re - f_im[..., None] * b_im
    bb_im = f_re[..., None] * b_im + f_im[..., None] * b_re
    eye = jnp.eye(per, dtype=jnp.float32)
    bb = jnp.stack([bb_re, bb_im], axis=1).reshape(depth, 2, S5_CHUNKS, per, S5_STATE, S5_CH)
    bmat = jnp.einsum('lkjgpc,gh->ljgckhp', bb, eye).reshape(depth, S5_CHUNKS, LANES, 2 * S5_CHUNK_STATES)
    cc = jnp.stack([c_re, -c_im], axis=1).reshape(depth, 2, S5_CHUNKS, per, S5_CH, S5_STATE)
    cmat = jnp.einsum('lkjgcp,gh->ljkhpgc', cc, eye).reshape(depth, S5_CHUNKS, 2 * S5_CHUNK_STATES, LANES)
    return (bmat.astype(jnp.bfloat16), cmat.astype(jnp.bfloat16),
            ab_re.reshape(depth, N_SLABS, LANES), ab_im.reshape(depth, N_SLABS, LANES))


def _s5(proj, bmat, a_re, a_im, cmat, s5_d, w_glu_bf16, b_glu, layer):
    s = proj.shape[0]
    tm = S5_TM
    full = lambda a: pl.BlockSpec(a.shape, lambda i, nd=a.ndim: (0,) * nd)
    vec = lambda a: pl.BlockSpec((None,) + a.shape[1:], lambda i, nd=a.ndim: (layer,) + (0,) * (nd - 1))
    return pl.pallas_call(
        functools.partial(_s5_kernel, layer=layer),
        grid=(s // tm,),
        in_specs=[pl.BlockSpec((tm, BR), lambda i: (i, COL_D_U)),
                  pl.BlockSpec((tm, BR), lambda i: (i, COL_D_G)),
                  vec(bmat), vec(a_re), vec(a_im), vec(cmat),
                  full(s5_d), vec(w_glu_bf16), full(b_glu)],
        out_specs=pl.BlockSpec((tm, BR), lambda i: (i, 0)),
        out_shape=jax.ShapeDtypeStruct((s, BR), jnp.bfloat16),
        scratch_shapes=[pltpu.VMEM((2 * N_SLABS * SLAB_PITCH, LANES), jnp.float32),
                        pltpu.VMEM((2 * N_SLABS // SUBLANES, SUBLANES, LANES), jnp.float32)],
        compiler_params=_params("arbitrary"),
        name="s5",
    )(proj, proj, bmat, a_re, a_im, cmat, s5_d, w_glu_bf16, b_glu)


OUT_TM = 512
OUT_SUB = 256


def _gated_conv_tile(ab, ac, ax, ag, cw_ref, hist_ref):
    u = ac * ax
    d1, d2 = _delayed(u, hist_ref, 2)
    conv = d2 * cw_ref[0:1, :]
    conv = conv + d1 * cw_ref[1:2, :]
    conv = conv + u * cw_ref[2:3, :]
    return (ab * conv * _silu(ag)).astype(jnp.bfloat16)


def _out_kernel(x_ref, ab_ref, ac_ref, ax_ref, ag_ref, caw_ref,
                cx_ref, cg_ref, ccw_ref, ccb_ref, lw_ref, lba_ref, lbx_ref, lam_ref,
                yb_ref, yd_ref, w_ref, gate_ref, g_ref, b_ref, o_ref,
                hist_a_ref, hist_c_ref, h_ref, *, layer):
    ccb_ref, lba_ref, lbx_ref, lam_ref, g_ref, b_ref = (
        r.at[pl.ds(layer, 1)] for r in (ccb_ref, lba_ref, lbx_ref, lam_ref, g_ref, b_ref))

    @pl.when(pl.program_id(0) == 0)
    def _():
        hist_a_ref[...] = jnp.zeros_like(hist_a_ref)
        hist_c_ref[...] = jnp.zeros_like(hist_c_ref)
        h_ref[...] = jnp.zeros_like(h_ref)

    for sub in range(x_ref.shape[0] // OUT_SUB):
        rows = slice(sub * OUT_SUB, (sub + 1) * OUT_SUB)
        ya = _gated_conv_tile(ab_ref[rows, :], ac_ref[rows, :], ax_ref[rows, :], ag_ref[rows, :],
                              caw_ref, hist_a_ref)
        yc = _rglru_tile(cx_ref[rows, :], cg_ref[rows, :], ccw_ref, ccb_ref, lw_ref, lba_ref, lbx_ref,
                         lam_ref, hist_c_ref, h_ref)
        y = jnp.dot(ya, w_ref[0 * BR:1 * BR, :], preferred_element_type=jnp.float32)
        y = y + jnp.dot(yb_ref[rows, :], w_ref[1 * BR:2 * BR, :], preferred_element_type=jnp.float32)
        y = y + jnp.dot(yc, w_ref[2 * BR:3 * BR, :], preferred_element_type=jnp.float32)
        y = y + jnp.dot(yd_ref[rows, :], w_ref[3 * BR:4 * BR, :], preferred_element_type=jnp.float32)
        z = ALPHA * x_ref[rows, :] + (1.0 + gate_ref[...]) * y
        mu = jnp.mean(z, axis=-1, keepdims=True)
        zc = z - mu
        var = jnp.mean(zc * zc, axis=-1, keepdims=True)
        o_ref[rows, :] = zc * lax.rsqrt(var + LN_EPS) * g_ref[...] + b_ref[...]


def _out(x, proj, conv_a, conv_c, conv_c_b, lru_w, lru_ba, lru_bx, lru_lambda, yb, yd, w_out_bf16, ada,
         ln_g, ln_b, layer):
    s, d = x.shape
    tm = OUT_TM
    col = lambda c: pl.BlockSpec((tm, BR), lambda i, c=c: (i, c))
    branch = pl.BlockSpec((tm, BR), lambda i: (i, 0))
    full = lambda a: pl.BlockSpec(a.shape, lambda i, nd=a.ndim: (0,) * nd)
    vec = lambda a: pl.BlockSpec((None,) + a.shape[1:], lambda i, nd=a.ndim: (layer,) + (0,) * (nd - 1))
    return pl.pallas_call(
        functools.partial(_out_kernel, layer=layer),
        grid=(s // tm,),
        in_specs=[pl.BlockSpec((tm, d), lambda i: (i, 0)),
                  col(COL_A_B), col(COL_A_C), col(COL_A_X), col(COL_A_G), vec(conv_a),
                  col(COL_C_X), col(COL_C_G), vec(conv_c), full(conv_c_b), vec(lru_w),
                  full(lru_ba), full(lru_bx), full(lru_lambda),
                  branch, branch,
                  pl.BlockSpec((None, 4 * BR, d), lambda i: (layer, 0, 0), pipeline_mode=pl.Buffered(1)),
                  pl.BlockSpec((None, 1, d), lambda i: (layer, 0, 2)),
                  full(ln_g), full(ln_b)],
        out_specs=pl.BlockSpec((tm, d), lambda i: (i, 0)),
        out_shape=jax.ShapeDtypeStruct((s, d), jnp.float32),
        scratch_shapes=[pltpu.VMEM((SUBLANES, BR), jnp.float32), pltpu.VMEM((SUBLANES, BR), jnp.float32),
                        pltpu.VMEM((1, BR), jnp.float32)],
        compiler_params=_params("arbitrary"),
        name="out_proj_ln",
    )(x, proj, proj, proj, proj, conv_a, proj, proj, conv_c, conv_c_b, lru_w, lru_ba, lru_bx, lru_lambda,
      yb, yd, w_out_bf16, ada, ln_g, ln_b)


def kernel(x, c, rel_bias, w_ada, b_ada, w_in, conv_a, conv_c, conv_c_b, lru_wa, lru_ba, lru_wx, lru_bx, lru_lambda, s5_lam_re, s5_lam_im, s5_log_dt, s5_b_re, s5_b_im, s5_c_re, s5_c_im, s5_d, s5_w_glu, s5_b_glu, w_out, ln_g, ln_b):
    bsz, s, d = x.shape
    assert bsz == 1 and w_in.shape == (DEPTH, d, N_IN)
    xs = x.reshape(s, d)
    w_out_bf16 = w_out.astype(jnp.bfloat16)
    w_glu_bf16 = s5_w_glu.astype(jnp.bfloat16)
    eye = jnp.eye(LRU_HEADS, dtype=jnp.float32)
    lru_w = jnp.einsum('lkhij,hg->lhikgj', jnp.stack([lru_wa, lru_wx], axis=1), eye)
    lru_w = lru_w.reshape(DEPTH, BR, 2 * BR).astype(jnp.bfloat16)
    bmat, cmat, a_re, a_im = _s5_discretize(s5_lam_re, s5_lam_im, s5_log_dt, s5_b_re, s5_b_im, s5_c_re, s5_c_im)
    ada = _ada(c.reshape(d, 1), w_ada, b_ada)

    for l in range(DEPTH):
        proj = _proj(xs, ada, w_in, l)
        yb = _attn(proj, rel_bias)
        yd = _s5(proj, bmat, a_re, a_im, cmat, s5_d, w_glu_bf16, s5_b_glu, l)
        xs = _out(xs, proj, conv_a, conv_c, conv_c_b, lru_w, lru_ba, lru_bx, lru_lambda, yb, yd, w_out_bf16, ada,
                  ln_g, ln_b, l)
    return xs.reshape(bsz, s, d)
```

```python
import functools
import math

import numpy as np
import jax
import jax.numpy as jnp
from jax import lax
from jax.experimental import pallas as pl
from jax.experimental.pallas import tpu as pltpu

BR = 512
N_IN = 12 * BR
ATT_HEADS = 8
ATT_HEAD_DIM = 64
BLK = 128
SPAN = 128
DILATIONS = (1, 4, 16)
REL_BUCKETS = 32
REL_MAX_DIST = 2048
LRU_HEADS = 8
LRU_C = 8.0
S5_CH = 16
S5_GROUPS = 32
S5_STATE = 64
DEPTH = 2
ALPHA = (2 * DEPTH) ** 0.25
LN_EPS = 1e-5

SUBLANES = 8
LANES = 128
VMEM_LIMIT = 56 * 1024 * 1024

COL_A_B, COL_A_C, COL_A_X, COL_A_G = 0, 1, 2, 3
COL_Q, COL_K, COL_V, COL_B_G = 4, 5, 6, 7
COL_C_X, COL_C_G = 8, 9
COL_D_U, COL_D_G = 10, 11


def _silu(x):
    return x * _sigmoid(x)


_sigmoid = jax.nn.sigmoid


def _log1p(x):
    w = 1.0 + x
    return jnp.where(w == 1.0, x, x * jnp.log(w) / (w - 1.0))


def _params(*sem):
    return pltpu.CompilerParams(dimension_semantics=sem, vmem_limit_bytes=VMEM_LIMIT)


def _delayed(x, prev_ref, max_delay):
    assert 0 < max_delay < SUBLANES
    prev = prev_ref[...]
    row = lax.broadcasted_iota(jnp.int32, prev.shape, 0)
    taps = []
    for d in range(1, max_delay + 1):
        rolled = pltpu.roll(x, d, 0)
        top = jnp.where(row < d, pltpu.roll(prev, d, 0), rolled[:SUBLANES])
        taps.append(jnp.concatenate([top, rolled[SUBLANES:]], axis=0))
    prev_ref[...] = x[x.shape[0] - SUBLANES:]
    return taps


def _shift_rows_fill(x, d, fill, period):
    assert 0 < d < period <= SUBLANES
    rolled = pltpu.roll(x, d, 0)
    row = lax.broadcasted_iota(jnp.int32, x.shape, 0) & (period - 1)
    return jnp.where(row < d, jnp.asarray(fill, x.dtype), rolled)


def _ada_kernel(c_ref, w_ref, b_ref, o_ref):
    cond = _silu(c_ref[...])
    bias = b_ref[pl.ds(pl.program_id(0), 1), :]
    o_ref[...] = jnp.sum(cond * w_ref[...], axis=0, keepdims=True) + bias


def _ada(c_col, w_ada, b_ada):
    d = c_col.shape[0]
    depth, _, n = w_ada.shape
    tn = 512
    return pl.pallas_call(
        _ada_kernel,
        grid=(depth, n // tn),
        in_specs=[
            pl.BlockSpec((d, 1), lambda l, j: (0, 0)),
            pl.BlockSpec((None, d, tn), lambda l, j: (l, 0, j)),
            pl.BlockSpec((depth, tn), lambda l, j: (0, j)),
        ],
        out_specs=pl.BlockSpec((None, 1, tn), lambda l, j: (l, 0, j)),
        out_shape=jax.ShapeDtypeStruct((depth, 1, n), jnp.float32),
        compiler_params=_params("arbitrary", "arbitrary"),
        name="ada",
    )(c_col, w_ada, b_ada)


def _proj_kernel(x_ref, shift_ref, scale_ref, w_ref, o_ref, h_ref):
    @pl.when(pl.program_id(1) == 0)
    def _():
        h = x_ref[...] * (1.0 + scale_ref[...]) + shift_ref[...]
        h_ref[...] = h.astype(jnp.bfloat16)

    o_ref[...] = jnp.dot(h_ref[...], w_ref[...].astype(jnp.bfloat16), preferred_element_type=jnp.float32)


def _proj(x, ada, w_in, layer):
    s, d = x.shape
    n = w_in.shape[2]
    tm, tn = 2048, 512
    return pl.pallas_call(
        _proj_kernel,
        grid=(s // tm, n // tn),
        in_specs=[
            pl.BlockSpec((tm, d), lambda i, j: (i, 0), pipeline_mode=pl.Buffered(1)),
            pl.BlockSpec((None, 1, d), lambda i, j: (layer, 0, 0)),
            pl.BlockSpec((None, 1, d), lambda i, j: (layer, 0, 1)),
            pl.BlockSpec((None, d, tn), lambda i, j: (layer, 0, j)),
        ],
        out_specs=pl.BlockSpec((tm, tn), lambda i, j: (i, j)),
        out_shape=jax.ShapeDtypeStruct((s, n), jnp.float32),
        scratch_shapes=[pltpu.VMEM((tm, d), jnp.bfloat16)],
        compiler_params=_params("arbitrary", "arbitrary"),
        name="proj",
    )(x, ada, ada, w_in)


ATT_TILE = max(DILATIONS) * BLK
ATT_BLOCKS = ATT_TILE // BLK
N_PAIRS = ATT_HEADS // 2
DEINT = 4
PLANE = ATT_TILE // DEINT
QROWS = BLK // DEINT
MIX_ROWS = 256
ATT_UNROLL = 16
MASKED = -1e30
LOG2E = math.log2(math.e)


def _t5_bucket_tables():
    assert DILATIONS == (1, 4, 16) and DEINT == 4
    i = np.arange(BLK)[:, None]
    j = np.arange(2 * BLK)[None, :]
    delta = i + BLK - j
    valid = (delta >= 0) & (delta <= SPAN)
    max_exact = REL_BUCKETS // 2
    tables = []
    for dil in DILATIONS:
        dist = np.clip(delta, 0, SPAN) * dil
        nf = np.maximum(dist, 1).astype(np.float32)
        large = max_exact + (np.log(nf / np.float32(max_exact)) / np.float32(math.log(REL_MAX_DIST / max_exact))
                             * np.float32(REL_BUCKETS - max_exact)).astype(np.int32)
        bucket = np.where(dist < max_exact, dist, np.minimum(large, REL_BUCKETS - 1))
        table = np.stack([np.where(valid & (j >= BLK), bucket, -1), np.where(valid, bucket, -1)])
        if dil == 1:
            rows = np.array([DEINT * a + r for r in range(DEINT) for a in range(QROWS)])
            cols = np.array([blk * BLK + DEINT * a + r
                             for r in range(DEINT) for blk in range(2) for a in range(QROWS)])
            table = table[:, rows][:, :, cols]
        tables.append(table)
    return jnp.asarray(np.stack(tables), jnp.int32)


def _attend(q, k, v, bias, low):
    q = (q * (LOG2E * ATT_HEAD_DIM ** -0.5)).astype(jnp.bfloat16)
    zero = jnp.zeros_like(q)
    q2 = jnp.concatenate([jnp.where(low, q, zero), jnp.where(low, zero, q)], axis=0)
    sc = lax.dot_general(q2, k.astype(jnp.bfloat16), (((1,), (1,)), ((), ())),
                         preferred_element_type=jnp.float32) + bias
    m = jnp.max(sc, axis=-1, keepdims=True)
    p = jnp.exp2(sc - m)
    l = jnp.sum(p, axis=-1, keepdims=True)
    pv = jnp.dot(p.astype(jnp.bfloat16), v.astype(jnp.bfloat16), preferred_element_type=jnp.float32)
    return (jnp.where(low, pv[:BLK], pv[BLK:]), jnp.where(low, m[:BLK], m[BLK:]),
            jnp.where(low, l[:BLK], l[BLK:]))


def _attn_kernel(rb_ref, bucket_ref, q_ref, k_ref, v_ref, g_ref, y_ref,
                 q4_ref, k4_ref, v4_ref, o4_ref, m4_ref, l4_ref, ynat_ref, bias_ref):
    pair = pl.program_id(0)
    t = pl.program_id(1)
    lane = lax.broadcasted_iota(jnp.int32, (BLK, LANES), 1)
    low = lane < ATT_HEAD_DIM
    n_pat = len(DILATIONS)

    @pl.when(t == 0)
    def _():
        k4_ref[:, 0:PLANE, :] = jnp.zeros((DEINT, PLANE, LANES), jnp.float32)
        v4_ref[:, 0:PLANE, :] = jnp.zeros((DEINT, PLANE, LANES), jnp.float32)
        for g in range(n_pat):
            bucket = bucket_ref[g, 1]
            no_prev = bucket_ref[g, 0] < 0
            hits = [bucket == b for b in range(REL_BUCKETS)]
            for half in range(2):
                acc = jnp.full(bucket.shape, MASKED, jnp.float32)
                for b in range(REL_BUCKETS):
                    acc = jnp.where(hits[b], LOG2E * rb_ref[b, 2 * pair + half], acc)
                bias_ref[g, 1, half * BLK:(half + 1) * BLK, :] = acc
                bias_ref[g, 0, half * BLK:(half + 1) * BLK, :] = jnp.where(no_prev, MASKED, acc)

    for r in range(DEINT):
        q4_ref[r] = q_ref[pl.ds(r, PLANE, stride=DEINT), :]
        k4_ref[r, PLANE:2 * PLANE, :] = k_ref[pl.ds(r, PLANE, stride=DEINT), :]
        v4_ref[r, PLANE:2 * PLANE, :] = v_ref[pl.ds(r, PLANE, stride=DEINT), :]

    def d1_qk(n):
        a0 = pl.multiple_of(n * QROWS, QROWS)
        q = jnp.concatenate([q4_ref[r, pl.ds(a0, QROWS), :] for r in range(DEINT)], axis=0)
        k = jnp.concatenate([k4_ref[r, pl.ds(PLANE - QROWS + a0, 2 * QROWS), :] for r in range(DEINT)], axis=0)
        return q, k, (t > 0) | (n > 0)

    def d1_v(n):
        a0 = pl.multiple_of(n * QROWS, QROWS)
        return jnp.concatenate([v4_ref[r, pl.ds(PLANE - QROWS + a0, 2 * QROWS), :] for r in range(DEINT)], axis=0)

    stats_refs = (o4_ref, m4_ref, l4_ref)

    def d1_store(n, stats):
        a0 = pl.multiple_of(n * QROWS, QROWS)
        for ref, val in zip(stats_refs, stats):
            for r in range(DEINT):
                ref[0, r, pl.ds(a0, QROWS), :] = val[r * QROWS:(r + 1) * QROWS]

    def d4_qk(b):
        r, n = b // DEINT, b % DEINT
        a0 = pl.multiple_of(n * BLK, BLK)
        return (q4_ref[r, pl.ds(a0, BLK), :], k4_ref[r, pl.ds(PLANE - BLK + a0, 2 * BLK), :],
                (t > 0) | (n > 0))

    def d4_v(b):
        r, n = b // DEINT, b % DEINT
        return v4_ref[r, pl.ds(PLANE - BLK + pl.multiple_of(n * BLK, BLK), 2 * BLK), :]

    def d4_store(b, stats):
        r, n = b // DEINT, b % DEINT
        a0 = pl.multiple_of(n * BLK, BLK)
        for ref, val in zip(stats_refs, stats):
            ref[1, r, pl.ds(a0, BLK), :] = val

    def d16_qk(b):
        lo, hi = b // DEINT, b % DEINT
        return (q4_ref[lo, pl.ds(hi, BLK, stride=DEINT), :],
                k4_ref[lo, pl.ds(hi, 2 * BLK, stride=DEINT), :], t > 0)

    def d16_v(b):
        lo, hi = b // DEINT, b % DEINT
        return v4_ref[lo, pl.ds(hi, 2 * BLK, stride=DEINT), :]

    def d16_store(b, stats):
        lo, hi = b // DEINT, b % DEINT
        for ref, val in zip(stats_refs, stats):
            ref[2, lo, pl.ds(hi, BLK, stride=DEINT), :] = val

    for g, (qk, vload, store) in enumerate(((d1_qk, d1_v, d1_store), (d4_qk, d4_v, d4_store),
                                            (d16_qk, d16_v, d16_store))):
        def unit(u, carry, g=g, qk=qk, vload=vload, store=store):
            q, k, has_prev = qk(u)
            store(u, _attend(q, k, vload(u), bias_ref[g, jnp.where(has_prev, 1, 0)], low))
            return carry

        lax.fori_loop(0, ATT_BLOCKS, unit, 0, unroll=ATT_UNROLL)

    for r in range(DEINT):
        def mix(c, carry, r=r):
            a0 = pl.multiple_of(c * MIX_ROWS, MIX_ROWS)
            nat = pl.ds(DEINT * a0 + r, MIX_ROWS, stride=DEINT)
            rows = pl.ds(a0, MIX_ROWS)
            ms = [m4_ref[g, r, rows, :] for g in range(n_pat)]
            m = functools.reduce(jnp.maximum, ms)
            e = [jnp.exp2(x - m) for x in ms]
            num = sum(e[g] * o4_ref[g, r, rows, :] for g in range(n_pat))
            den = sum(e[g] * l4_ref[g, r, rows, :] for g in range(n_pat))
            ynat_ref[nat, :] = num / den * _silu(g_ref[nat, :])
            return carry

        lax.fori_loop(0, PLANE // MIX_ROWS, mix, 0)

    y_ref[...] = ynat_ref[...].astype(y_ref.dtype)
    k4_ref[:, 0:PLANE, :] = k4_ref[:, PLANE:2 * PLANE, :]
    v4_ref[:, 0:PLANE, :] = v4_ref[:, PLANE:2 * PLANE, :]


def _attn(proj, rel_bias):
    s = proj.shape[0]
    assert s % ATT_TILE == 0
    lanes_per_col = BR // LANES
    col = lambda c: pl.BlockSpec((ATT_TILE, LANES), lambda p, t, c=c: (t, c * lanes_per_col + p))
    n_pat = len(DILATIONS)
    return pl.pallas_call(
        _attn_kernel,
        grid=(N_PAIRS, s // ATT_TILE),
        in_specs=[
            pl.BlockSpec(memory_space=pltpu.SMEM),
            pl.BlockSpec((n_pat, 2, BLK, 2 * BLK), lambda p, t: (0, 0, 0, 0)),
            col(COL_Q), col(COL_K), col(COL_V), col(COL_B_G),
        ],
        out_specs=pl.BlockSpec((ATT_TILE, LANES), lambda p, t: (t, p)),
        out_shape=jax.ShapeDtypeStruct((s, BR), jnp.bfloat16),
        scratch_shapes=[pltpu.VMEM((DEINT, PLANE, LANES), jnp.float32),
                        pltpu.VMEM((DEINT, 2 * PLANE, LANES), jnp.float32),
                        pltpu.VMEM((DEINT, 2 * PLANE, LANES), jnp.float32),
                        pltpu.VMEM((n_pat, DEINT, PLANE, LANES), jnp.float32),
                        pltpu.VMEM((n_pat, DEINT, PLANE, LANES), jnp.float32),
                        pltpu.VMEM((n_pat, DEINT, PLANE, LANES), jnp.float32),
                        pltpu.VMEM((ATT_TILE, LANES), jnp.float32),
                        pltpu.VMEM((n_pat, 2, 2 * BLK, 2 * BLK), jnp.float32)],
        compiler_params=_params("arbitrary", "arbitrary"),
        name="dilated_attn",
    )(rel_bias, _t5_bucket_tables(), proj, proj, proj, proj)


def _rglru_tile(cx, cg, cw_ref, cb_ref, w_ref, ba_ref, bx_ref, lam_ref, hist_ref, h_ref):
    tm = cx.shape[0]
    d1, d2, d3 = _delayed(cx, hist_ref, 3)
    xc = d3 * cw_ref[0:1, :]
    xc = xc + d2 * cw_ref[1:2, :]
    xc = xc + d1 * cw_ref[2:3, :]
    xc = xc + cx * cw_ref[3:4, :]
    xc = xc + cb_ref[...]

    z = jnp.dot(xc.astype(jnp.bfloat16), w_ref[...], preferred_element_type=jnp.float32)
    r = _sigmoid(z[:, :BR] + ba_ref[...])
    ig = _sigmoid(z[:, BR:] + bx_ref[...])
    neg_lam = -lam_ref[...]
    softplus = jnp.maximum(neg_lam, 0.0) + _log1p(jnp.exp(-jnp.abs(neg_lam)))
    log_a = -LRU_C * r * softplus
    a = jnp.exp(log_a)
    b = jnp.sqrt(jnp.tanh(-log_a) * (a * a + 1.0)) * ig * xc

    d = 1
    while d < SUBLANES:
        b = b + a * _shift_rows_fill(b, d, 0.0, period=SUBLANES)
        a = a * _shift_rows_fill(a, d, 1.0, period=SUBLANES)
        d *= 2
    carry = h_ref[...]
    groups = []
    for g in range(tm // SUBLANES):
        rows = slice(g * SUBLANES, (g + 1) * SUBLANES)
        hg = b[rows] + a[rows] * carry
        carry = hg[SUBLANES - 1:]
        groups.append(hg)
    h_ref[...] = carry
    h = jnp.concatenate(groups, axis=0)
    return (h * _silu(cg)).astype(jnp.bfloat16)


N_STATE = S5_GROUPS * S5_STATE
S5_TM = 512
S5_CHUNKS = BR // LANES
S5_CHUNK_STATES = N_STATE // S5_CHUNKS
SLABS_PER_CHUNK = S5_CHUNK_STATES // LANES
N_SLABS = N_STATE // LANES
SLAB_PITCH = S5_TM + SUBLANES // 2
S5_UNROLL = 8


def _s5_kernel(u_ref, g_ref, bmat_ref, ar_ref, ai_ref, cmat_ref, d_ref, wglu_ref, bglu_ref,
               y_ref, xs_ref, state_ref, *, layer):
    d_ref = d_ref.at[pl.ds(layer, 1)]
    bglu_ref = bglu_ref.at[pl.ds(layer, 1)]

    @pl.when(pl.program_id(0) == 0)
    def _():
        state_ref[...] = jnp.zeros_like(state_ref)

    u = u_ref[...]
    tm = u.shape[0]
    ub = u.astype(jnp.bfloat16)
    for j in range(S5_CHUNKS):
        bu = jnp.dot(ub[:, j * LANES:(j + 1) * LANES], bmat_ref[j], preferred_element_type=jnp.float32)
        for part in range(2):
            for cc in range(SLABS_PER_CHUNK):
                slab = part * N_SLABS + j * SLABS_PER_CHUNK + cc
                col = part * S5_CHUNK_STATES + cc * LANES
                xs_ref[pl.ds(slab * SLAB_PITCH, tm), :] = bu[:, col:col + LANES]

    n_vreg = N_SLABS // SUBLANES
    a_re = [ar_ref[v * SUBLANES:(v + 1) * SUBLANES, :] for v in range(n_vreg)]
    a_im = [ai_ref[v * SUBLANES:(v + 1) * SUBLANES, :] for v in range(n_vreg)]

    def rows(part, v, t):
        return pl.ds((part * N_SLABS + v * SUBLANES) * SLAB_PITCH + t, SUBLANES, stride=SLAB_PITCH)

    def step(t, x):
        new = []
        for v in range(n_vreg):
            xr, xi = x[2 * v], x[2 * v + 1]
            nr = (a_re[v] * xr + xs_ref[rows(0, v, t), :]) - a_im[v] * xi
            ni = (a_re[v] * xi + xs_ref[rows(1, v, t), :]) + a_im[v] * xr
            xs_ref[rows(0, v, t), :] = nr
            xs_ref[rows(1, v, t), :] = ni
            new += [nr, ni]
        return tuple(new)

    x0 = tuple(state_ref[i] for i in range(2 * n_vreg))
    x1 = lax.fori_loop(0, tm, step, x0, unroll=S5_UNROLL)
    for i in range(2 * n_vreg):
        state_ref[i] = x1[i]

    ys = []
    for j in range(S5_CHUNKS):
        xcat = jnp.concatenate(
            [xs_ref[pl.ds((part * N_SLABS + j * SLABS_PER_CHUNK + cc) * SLAB_PITCH, tm), :]
             for part in range(2) for cc in range(SLABS_PER_CHUNK)], axis=1)
        ys.append(jnp.dot(xcat.astype(jnp.bfloat16), cmat_ref[j], preferred_element_type=jnp.float32))
    y = jnp.concatenate(ys, axis=1) + d_ref[...] * u
    y = 0.5 * y * (1.0 + jnp.tanh(math.sqrt(2.0 / math.pi) * (y + 0.044715 * (y * y * y))))
    gate = jnp.dot(y.astype(jnp.bfloat16), wglu_ref[...], preferred_element_type=jnp.float32) + bglu_ref[...]
    y = y * _sigmoid(gate)
    y_ref[...] = (y * _silu(g_ref[...])).astype(y_ref.dtype)


def _s5_discretize(lam_re, lam_im, log_dt, b_re, b_im, c_re, c_im):
    depth = lam_re.shape[0]
    per = S5_GROUPS // S5_CHUNKS
    dt = jnp.exp(log_dt)[..., None]
    mag = jnp.exp(lam_re * dt)
    ab_re = mag * jnp.cos(lam_im * dt)
    ab_im = mag * jnp.sin(lam_im * dt)
    den = lam_re * lam_re + lam_im * lam_im
    f_re = ((ab_re - 1.0) * lam_re + ab_im * lam_im) / den
    f_im = (ab_im * lam_re - (ab_re - 1.0) * lam_im) / den
    bb_re = f_re[..., None] * b_re - f_im[..., None] * b_im
    bb_im = f_re[..., None] * b_im + f_im[..., None] * b_re
    eye = jnp.eye(per, dtype=jnp.float32)
    bb = jnp.stack([bb_re, bb_im], axis=1).reshape(depth, 2, S5_CHUNKS, per, S5_STATE, S5_CH)
    bmat = jnp.einsum('lkjgpc,gh->ljgckhp', bb, eye).reshape(depth, S5_CHUNKS, LANES, 2 * S5_CHUNK_STATES)
    cc = jnp.stack([c_re, -c_im], axis=1).reshape(depth, 2, S5_CHUNKS, per, S5_CH, S5_STATE)
    cmat = jnp.einsum('lkjgcp,gh->ljkhpgc', cc, eye).reshape(depth, S5_CHUNKS, 2 * S5_CHUNK_STATES, LANES)
    return (bmat.astype(jnp.bfloat16), cmat.astype(jnp.bfloat16),
            ab_re.reshape(depth, N_SLABS, LANES), ab_im.reshape(depth, N_SLABS, LANES))


def _s5(proj, bmat, a_re, a_im, cmat, s5_d, w_glu_bf16, b_glu, layer):
    s = proj.shape[0]
    tm = S5_TM
    full = lambda a: pl.BlockSpec(a.shape, lambda i, nd=a.ndim: (0,) * nd)
    vec = lambda a: pl.BlockSpec((None,) + a.shape[1:], lambda i, nd=a.ndim: (layer,) + (0,) * (nd - 1))
    return pl.pallas_call(
        functools.partial(_s5_kernel, layer=layer),
        grid=(s // tm,),
        in_specs=[pl.BlockSpec((tm, BR), lambda i: (i, COL_D_U)),
                  pl.BlockSpec((tm, BR), lambda i: (i, COL_D_G)),
                  vec(bmat), vec(a_re), vec(a_im), vec(cmat),
                  full(s5_d), vec(w_glu_bf16), full(b_glu)],
        out_specs=pl.BlockSpec((tm, BR), lambda i: (i, 0)),
        out_shape=jax.ShapeDtypeStruct((s, BR), jnp.bfloat16),
        scratch_shapes=[pltpu.VMEM((2 * N_SLABS * SLAB_PITCH, LANES), jnp.float32),
                        pltpu.VMEM((2 * N_SLABS // SUBLANES, SUBLANES, LANES), jnp.float32)],
        compiler_params=_params("arbitrary"),
        name="s5",
    )(proj, proj, bmat, a_re, a_im, cmat, s5_d, w_glu_bf16, b_glu)


OUT_TM = 512
OUT_SUB = 256


def _gated_conv_tile(ab, ac, ax, ag, cw_ref, hist_ref):
    u = ac * ax
    d1, d2 = _delayed(u, hist_ref, 2)
    conv = d2 * cw_ref[0:1, :]
    conv = conv + d1 * cw_ref[1:2, :]
    conv = conv + u * cw_ref[2:3, :]
    return (ab * conv * _silu(ag)).astype(jnp.bfloat16)


def _out_kernel(x_ref, ab_ref, ac_ref, ax_ref, ag_ref, caw_ref,
                cx_ref, cg_ref, ccw_ref, ccb_ref, lw_ref, lba_ref, lbx_ref, lam_ref,
                yb_ref, yd_ref, w_ref, gate_ref, g_ref, b_ref, o_ref,
                hist_a_ref, hist_c_ref, h_ref, *, layer):
    ccb_ref, lba_ref, lbx_ref, lam_ref, g_ref, b_ref = (
        r.at[pl.ds(layer, 1)] for r in (ccb_ref, lba_ref, lbx_ref, lam_ref, g_ref, b_ref))

    @pl.when(pl.program_id(0) == 0)
    def _():
        hist_a_ref[...] = jnp.zeros_like(hist_a_ref)
        hist_c_ref[...] = jnp.zeros_like(hist_c_ref)
        h_ref[...] = jnp.zeros_like(h_ref)

    for sub in range(x_ref.shape[0] // OUT_SUB):
        rows = slice(sub * OUT_SUB, (sub + 1) * OUT_SUB)
        ya = _gated_conv_tile(ab_ref[rows, :], ac_ref[rows, :], ax_ref[rows, :], ag_ref[rows, :],
                              caw_ref, hist_a_ref)
        yc = _rglru_tile(cx_ref[rows, :], cg_ref[rows, :], ccw_ref, ccb_ref, lw_ref, lba_ref, lbx_ref,
                         lam_ref, hist_c_ref, h_ref)
        y = jnp.dot(ya, w_ref[0 * BR:1 * BR, :], preferred_element_type=jnp.float32)
        y = y + jnp.dot(yb_ref[rows, :], w_ref[1 * BR:2 * BR, :], preferred_element_type=jnp.float32)
        y = y + jnp.dot(yc, w_ref[2 * BR:3 * BR, :], preferred_element_type=jnp.float32)
        y = y + jnp.dot(yd_ref[rows, :], w_ref[3 * BR:4 * BR, :], preferred_element_type=jnp.float32)
        z = ALPHA * x_ref[rows, :] + (1.0 + gate_ref[...]) * y
        mu = jnp.mean(z, axis=-1, keepdims=True)
        zc = z - mu
        var = jnp.mean(zc * zc, axis=-1, keepdims=True)
        o_ref[rows, :] = zc * lax.rsqrt(var + LN_EPS) * g_ref[...] + b_ref[...]


def _out(x, proj, conv_a, conv_c, conv_c_b, lru_w, lru_ba, lru_bx, lru_lambda, yb, yd, w_out_bf16, ada,
         ln_g, ln_b, layer):
    s, d = x.shape
    tm = OUT_TM
    col = lambda c: pl.BlockSpec((tm, BR), lambda i, c=c: (i, c))
    branch = pl.BlockSpec((tm, BR), lambda i: (i, 0))
    full = lambda a: pl.BlockSpec(a.shape, lambda i, nd=a.ndim: (0,) * nd)
    vec = lambda a: pl.BlockSpec((None,) + a.shape[1:], lambda i, nd=a.ndim: (layer,) + (0,) * (nd - 1))
    return pl.pallas_call(
        functools.partial(_out_kernel, layer=layer),
        grid=(s // tm,),
        in_specs=[pl.BlockSpec((tm, d), lambda i: (i, 0)),
                  col(COL_A_B), col(COL_A_C), col(COL_A_X), col(COL_A_G), vec(conv_a),
                  col(COL_C_X), col(COL_C_G), vec(conv_c), full(conv_c_b), vec(lru_w),
                  full(lru_ba), full(lru_bx), full(lru_lambda),
                  branch, branch,
                  pl.BlockSpec((None, 4 * BR, d), lambda i: (layer, 0, 0), pipeline_mode=pl.Buffered(1)),
                  pl.BlockSpec((None, 1, d), lambda i: (layer, 0, 2)),
                  full(ln_g), full(ln_b)],
        out_specs=pl.BlockSpec((tm, d), lambda i: (i, 0)),
        out_shape=jax.ShapeDtypeStruct((s, d), jnp.float32),
        scratch_shapes=[pltpu.VMEM((SUBLANES, BR), jnp.float32), pltpu.VMEM((SUBLANES, BR), jnp.float32),
                        pltpu.VMEM((1, BR), jnp.float32)],
        compiler_params=_params("arbitrary"),
        name="out_proj_ln",
    )(x, proj, proj, proj, proj, conv_a, proj, proj, conv_c, conv_c_b, lru_w, lru_ba, lru_bx, lru_lambda,
      yb, yd, w_out_bf16, ada, ln_g, ln_b)


def kernel(x, c, rel_bias, w_ada, b_ada, w_in, conv_a, conv_c, conv_c_b, lru_wa, lru_ba, lru_wx, lru_bx, lru_lambda, s5_lam_re, s5_lam_im, s5_log_dt, s5_b_re, s5_b_im, s5_c_re, s5_c_im, s5_d, s5_w_glu, s5_b_glu, w_out, ln_g, ln_b):
    bsz, s, d = x.shape
    assert bsz == 1 and w_in.shape == (DEPTH, d, N_IN)
    xs = x.reshape(s, d)
    w_out_bf16 = w_out.astype(jnp.bfloat16)
    w_glu_bf16 = s5_w_glu.astype(jnp.bfloat16)
    eye = jnp.eye(LRU_HEADS, dtype=jnp.float32)
    lru_w = jnp.einsum('lkhij,hg->lhikgj', jnp.stack([lru_wa, lru_wx], axis=1), eye)
    lru_w = lru_w.reshape(DEPTH, BR, 2 * BR).astype(jnp.bfloat16)
    bmat, cmat, a_re, a_im = _s5_discretize(s5_lam_re, s5_lam_im, s5_log_dt, s5_b_re, s5_b_im, s5_c_re, s5_c_im)
    ada = _ada(c.reshape(d, 1), w_ada, b_ada)

    for l in range(DEPTH):
        proj = _proj(xs, ada, w_in, l)
        yb = _attn(proj, rel_bias)
        yd = _s5(proj, bmat, a_re, a_im, cmat, s5_d, w_glu_bf16, s5_b_glu, l)
        xs = _out(xs, proj, conv_a, conv_c, conv_c_b, lru_w, lru_ba, lru_bx, lru_lambda, yb, yd, w_out_bf16, ada,
                  ln_g, ln_b, l)
    return xs.reshape(bsz, s, d)
```

```python
import functools
import math

import numpy as np
import jax
import jax.numpy as jnp
from jax import lax
from jax.experimental import pallas as pl
from jax.experimental.pallas import tpu as pltpu

BR = 512
N_IN = 12 * BR
ATT_HEADS = 8
ATT_HEAD_DIM = 64
BLK = 128
SPAN = 128
DILATIONS = (1, 4, 16)
REL_BUCKETS = 32
REL_MAX_DIST = 2048
LRU_HEADS = 8
LRU_C = 8.0
S5_CH = 16
S5_GROUPS = 32
S5_STATE = 64
DEPTH = 2
ALPHA = (2 * DEPTH) ** 0.25
LN_EPS = 1e-5

SUBLANES = 8
LANES = 128
VMEM_LIMIT = 56 * 1024 * 1024

COL_A_B, COL_A_C, COL_A_X, COL_A_G = 0, 1, 2, 3
COL_Q, COL_K, COL_V, COL_B_G = 4, 5, 6, 7
COL_C_X, COL_C_G = 8, 9
COL_D_U, COL_D_G = 10, 11


def _silu(x):
    return x * _sigmoid(x)


_sigmoid = jax.nn.sigmoid


def _log1p(x):
    w = 1.0 + x
    return jnp.where(w == 1.0, x, x * jnp.log(w) / (w - 1.0))


def _params(*sem):
    return pltpu.CompilerParams(dimension_semantics=sem, vmem_limit_bytes=VMEM_LIMIT)


def _delayed(x, prev_ref, max_delay):
    assert 0 < max_delay < SUBLANES
    prev = prev_ref[...]
    row = lax.broadcasted_iota(jnp.int32, prev.shape, 0)
    taps = []
    for d in range(1, max_delay + 1):
        rolled = pltpu.roll(x, d, 0)
        top = jnp.where(row < d, pltpu.roll(prev, d, 0), rolled[:SUBLANES])
        taps.append(jnp.concatenate([top, rolled[SUBLANES:]], axis=0))
    prev_ref[...] = x[x.shape[0] - SUBLANES:]
    return taps


def _shift_rows_fill(x, d, fill, period):
    assert 0 < d < period <= SUBLANES
    rolled = pltpu.roll(x, d, 0)
    row = lax.broadcasted_iota(jnp.int32, x.shape, 0) & (period - 1)
    return jnp.where(row < d, jnp.asarray(fill, x.dtype), rolled)


def _ada_kernel(c_ref, w_ref, b_ref, o_ref):
    cond = _silu(c_ref[...])
    bias = b_ref[pl.ds(pl.program_id(0), 1), :]
    o_ref[...] = jnp.sum(cond * w_ref[...], axis=0, keepdims=True) + bias


def _ada(c_col, w_ada, b_ada):
    d = c_col.shape[0]
    depth, _, n = w_ada.shape
    tn = 1536
    return pl.pallas_call(
        _ada_kernel,
        grid=(depth, n // tn),
        in_specs=[
            pl.BlockSpec((d, 1), lambda l, j: (0, 0)),
            pl.BlockSpec((None, d, tn), lambda l, j: (l, 0, j)),
            pl.BlockSpec((depth, tn), lambda l, j: (0, j)),
        ],
        out_specs=pl.BlockSpec((None, 1, tn), lambda l, j: (l, 0, j)),
        out_shape=jax.ShapeDtypeStruct((depth, 1, n), jnp.float32),
        compiler_params=_params("arbitrary", "arbitrary"),
        name="ada",
    )(c_col, w_ada, b_ada)


def _proj_kernel(x_ref, shift_ref, scale_ref, w_ref, o_ref):
    h = (x_ref[...] * (1.0 + scale_ref[...]) + shift_ref[...]).astype(jnp.bfloat16)
    o_ref[...] = jnp.dot(h, w_ref[...].astype(jnp.bfloat16), preferred_element_type=jnp.float32)


def _proj(x, ada, w_in, layer):
    s, d = x.shape
    n = w_in.shape[2]
    tm, tn = 2048, 512
    return pl.pallas_call(
        _proj_kernel,
        grid=(s // tm, n // tn),
        in_specs=[
            pl.BlockSpec((tm, d), lambda i, j: (i, 0)),
            pl.BlockSpec((None, 1, d), lambda i, j: (layer, 0, 0)),
            pl.BlockSpec((None, 1, d), lambda i, j: (layer, 0, 1)),
            pl.BlockSpec((None, d, tn), lambda i, j: (layer, 0, j)),
        ],
        out_specs=pl.BlockSpec((tm, tn), lambda i, j: (i, j)),
        out_shape=jax.ShapeDtypeStruct((s, n), jnp.float32),
        compiler_params=_params("arbitrary", "arbitrary"),
        name="proj",
    )(x, ada, ada, w_in)


ATT_TILE = max(DILATIONS) * BLK
ATT_BLOCKS = ATT_TILE // BLK
N_PAIRS = ATT_HEADS // 2
DEINT = 4
PLANE = ATT_TILE // DEINT
QROWS = BLK // DEINT
MIX_ROWS = 256
ATT_UNROLL = 16
MASKED = -1e30
LOG2E = math.log2(math.e)


def _t5_bucket_tables():
    assert DILATIONS == (1, 4, 16) and DEINT == 4
    i = np.arange(BLK)[:, None]
    j = np.arange(2 * BLK)[None, :]
    delta = i + BLK - j
    valid = (delta >= 0) & (delta <= SPAN)
    max_exact = REL_BUCKETS // 2
    tables = []
    for dil in DILATIONS:
        dist = np.clip(delta, 0, SPAN) * dil
        nf = np.maximum(dist, 1).astype(np.float32)
        large = max_exact + (np.log(nf / np.float32(max_exact)) / np.float32(math.log(REL_MAX_DIST / max_exact))
                             * np.float32(REL_BUCKETS - max_exact)).astype(np.int32)
        bucket = np.where(dist < max_exact, dist, np.minimum(large, REL_BUCKETS - 1))
        table = np.stack([np.where(valid & (j >= BLK), bucket, -1), np.where(valid, bucket, -1)])
        if dil == 1:
            rows = np.array([DEINT * a + r for r in range(DEINT) for a in range(QROWS)])
            cols = np.array([blk * BLK + DEINT * a + r
                             for r in range(DEINT) for blk in range(2) for a in range(QROWS)])
            table = table[:, rows][:, :, cols]
        tables.append(table)
    return jnp.asarray(np.stack(tables), jnp.int32)


def _attend(q, k, v, bias, low):
    q = (q * (LOG2E * ATT_HEAD_DIM ** -0.5)).astype(jnp.bfloat16)
    zero = jnp.zeros_like(q)
    q2 = jnp.concatenate([jnp.where(low, q, zero), jnp.where(low, zero, q)], axis=0)
    sc = lax.dot_general(q2, k.astype(jnp.bfloat16), (((1,), (1,)), ((), ())),
                         preferred_element_type=jnp.float32) + bias
    m = jnp.max(sc, axis=-1, keepdims=True)
    p = jnp.exp2(sc - m)
    l = jnp.sum(p, axis=-1, keepdims=True)
    pv = jnp.dot(p.astype(jnp.bfloat16), v.astype(jnp.bfloat16), preferred_element_type=jnp.float32)
    return (jnp.where(low, pv[:BLK], pv[BLK:]), jnp.where(low, m[:BLK], m[BLK:]),
            jnp.where(low, l[:BLK], l[BLK:]))


def _attn_kernel(rb_ref, bucket_ref, q_ref, k_ref, v_ref, g_ref, y_ref,
                 q4_ref, k4_ref, v4_ref, o4_ref, m4_ref, l4_ref, ynat_ref, bias_ref):
    pair = pl.program_id(0)
    t = pl.program_id(1)
    lane = lax.broadcasted_iota(jnp.int32, (BLK, LANES), 1)
    low = lane < ATT_HEAD_DIM
    n_pat = len(DILATIONS)

    @pl.when(t == 0)
    def _():
        k4_ref[:, 0:PLANE, :] = jnp.zeros((DEINT, PLANE, LANES), jnp.float32)
        v4_ref[:, 0:PLANE, :] = jnp.zeros((DEINT, PLANE, LANES), jnp.float32)
        for g in range(n_pat):
            bucket = bucket_ref[g, 1]
            no_prev = bucket_ref[g, 0] < 0
            hits = [bucket == b for b in range(REL_BUCKETS)]
            for half in range(2):
                acc = jnp.full(bucket.shape, MASKED, jnp.float32)
                for b in range(REL_BUCKETS):
                    acc = jnp.where(hits[b], LOG2E * rb_ref[b, 2 * pair + half], acc)
                bias_ref[g, 1, half * BLK:(half + 1) * BLK, :] = acc
                bias_ref[g, 0, half * BLK:(half + 1) * BLK, :] = jnp.where(no_prev, MASKED, acc)

    for r in range(DEINT):
        q4_ref[r] = q_ref[pl.ds(r, PLANE, stride=DEINT), :]
        k4_ref[r, PLANE:2 * PLANE, :] = k_ref[pl.ds(r, PLANE, stride=DEINT), :]
        v4_ref[r, PLANE:2 * PLANE, :] = v_ref[pl.ds(r, PLANE, stride=DEINT), :]

    def d1_qk(n):
        a0 = pl.multiple_of(n * QROWS, QROWS)
        q = jnp.concatenate([q4_ref[r, pl.ds(a0, QROWS), :] for r in range(DEINT)], axis=0)
        k = jnp.concatenate([k4_ref[r, pl.ds(PLANE - QROWS + a0, 2 * QROWS), :] for r in range(DEINT)], axis=0)
        return q, k, (t > 0) | (n > 0)

    def d1_v(n):
        a0 = pl.multiple_of(n * QROWS, QROWS)
        return jnp.concatenate([v4_ref[r, pl.ds(PLANE - QROWS + a0, 2 * QROWS), :] for r in range(DEINT)], axis=0)

    stats_refs = (o4_ref, m4_ref, l4_ref)

    def d1_store(n, stats):
        a0 = pl.multiple_of(n * QROWS, QROWS)
        for ref, val in zip(stats_refs, stats):
            for r in range(DEINT):
                ref[0, r, pl.ds(a0, QROWS), :] = val[r * QROWS:(r + 1) * QROWS]

    def d4_qk(b):
        r, n = b // DEINT, b % DEINT
        a0 = pl.multiple_of(n * BLK, BLK)
        return (q4_ref[r, pl.ds(a0, BLK), :], k4_ref[r, pl.ds(PLANE - BLK + a0, 2 * BLK), :],
                (t > 0) | (n > 0))

    def d4_v(b):
        r, n = b // DEINT, b % DEINT
        return v4_ref[r, pl.ds(PLANE - BLK + pl.multiple_of(n * BLK, BLK), 2 * BLK), :]

    def d4_store(b, stats):
        r, n = b // DEINT, b % DEINT
        a0 = pl.multiple_of(n * BLK, BLK)
        for ref, val in zip(stats_refs, stats):
            ref[1, r, pl.ds(a0, BLK), :] = val

    def d16_qk(b):
        lo, hi = b // DEINT, b % DEINT
        return (q4_ref[lo, pl.ds(hi, BLK, stride=DEINT), :],
                k4_ref[lo, pl.ds(hi, 2 * BLK, stride=DEINT), :], t > 0)

    def d16_v(b):
        lo, hi = b // DEINT, b % DEINT
        return v4_ref[lo, pl.ds(hi, 2 * BLK, stride=DEINT), :]

    def d16_store(b, stats):
        lo, hi = b // DEINT, b % DEINT
        for ref, val in zip(stats_refs, stats):
            ref[2, lo, pl.ds(hi, BLK, stride=DEINT), :] = val

    for g, (qk, vload, store) in enumerate(((d1_qk, d1_v, d1_store), (d4_qk, d4_v, d4_store),
                                            (d16_qk, d16_v, d16_store))):
        def unit(u, carry, g=g, qk=qk, vload=vload, store=store):
            q, k, has_prev = qk(u)
            store(u, _attend(q, k, vload(u), bias_ref[g, jnp.where(has_prev, 1, 0)], low))
            return carry

        lax.fori_loop(0, ATT_BLOCKS, unit, 0, unroll=ATT_UNROLL)

    for r in range(DEINT):
        def mix(c, carry, r=r):
            a0 = pl.multiple_of(c * MIX_ROWS, MIX_ROWS)
            nat = pl.ds(DEINT * a0 + r, MIX_ROWS, stride=DEINT)
            rows = pl.ds(a0, MIX_ROWS)
            ms = [m4_ref[g, r, rows, :] for g in range(n_pat)]
            m = functools.reduce(jnp.maximum, ms)
            e = [jnp.exp2(x - m) for x in ms]
            num = sum(e[g] * o4_ref[g, r, rows, :] for g in range(n_pat))
            den = sum(e[g] * l4_ref[g, r, rows, :] for g in range(n_pat))
            ynat_ref[nat, :] = num / den * _silu(g_ref[nat, :])
            return carry

        lax.fori_loop(0, PLANE // MIX_ROWS, mix, 0)

    y_ref[...] = ynat_ref[...].astype(y_ref.dtype)
    k4_ref[:, 0:PLANE, :] = k4_ref[:, PLANE:2 * PLANE, :]
    v4_ref[:, 0:PLANE, :] = v4_ref[:, PLANE:2 * PLANE, :]


def _attn(proj, rel_bias):
    s = proj.shape[0]
    assert s % ATT_TILE == 0
    lanes_per_col = BR // LANES
    col = lambda c: pl.BlockSpec((ATT_TILE, LANES), lambda p, t, c=c: (t, c * lanes_per_col + p))
    n_pat = len(DILATIONS)
    return pl.pallas_call(
        _attn_kernel,
        grid=(N_PAIRS, s // ATT_TILE),
        in_specs=[
            pl.BlockSpec(memory_space=pltpu.SMEM),
            pl.BlockSpec((n_pat, 2, BLK, 2 * BLK), lambda p, t: (0, 0, 0, 0)),
            col(COL_Q), col(COL_K), col(COL_V), col(COL_B_G),
        ],
        out_specs=pl.BlockSpec((ATT_TILE, LANES), lambda p, t: (t, p)),
        out_shape=jax.ShapeDtypeStruct((s, BR), jnp.bfloat16),
        scratch_shapes=[pltpu.VMEM((DEINT, PLANE, LANES), jnp.float32),
                        pltpu.VMEM((DEINT, 2 * PLANE, LANES), jnp.float32),
                        pltpu.VMEM((DEINT, 2 * PLANE, LANES), jnp.float32),
                        pltpu.VMEM((n_pat, DEINT, PLANE, LANES), jnp.float32),
                        pltpu.VMEM((n_pat, DEINT, PLANE, LANES), jnp.float32),
                        pltpu.VMEM((n_pat, DEINT, PLANE, LANES), jnp.float32),
                        pltpu.VMEM((ATT_TILE, LANES), jnp.float32),
                        pltpu.VMEM((n_pat, 2, 2 * BLK, 2 * BLK), jnp.float32)],
        compiler_params=_params("arbitrary", "arbitrary"),
        name="dilated_attn",
    )(rel_bias, _t5_bucket_tables(), proj, proj, proj, proj)


def _rglru_tile(cx, cg, cw_ref, cb_ref, w_ref, ba_ref, bx_ref, lam_ref, hist_ref, h_ref):
    tm = cx.shape[0]
    d1, d2, d3 = _delayed(cx, hist_ref, 3)
    xc = d3 * cw_ref[0:1, :]
    xc = xc + d2 * cw_ref[1:2, :]
    xc = xc + d1 * cw_ref[2:3, :]
    xc = xc + cx * cw_ref[3:4, :]
    xc = xc + cb_ref[...]

    z = jnp.dot(xc.astype(jnp.bfloat16), w_ref[...], preferred_element_type=jnp.float32)
    r = _sigmoid(z[:, :BR] + ba_ref[...])
    ig = _sigmoid(z[:, BR:] + bx_ref[...])
    neg_lam = -lam_ref[...]
    softplus = jnp.maximum(neg_lam, 0.0) + _log1p(jnp.exp(-jnp.abs(neg_lam)))
    log_a = -LRU_C * r * softplus
    a = jnp.exp(log_a)
    b = jnp.sqrt(jnp.tanh(-log_a) * (a * a + 1.0)) * ig * xc

    d = 1
    while d < SUBLANES:
        b = b + a * _shift_rows_fill(b, d, 0.0, period=SUBLANES)
        a = a * _shift_rows_fill(a, d, 1.0, period=SUBLANES)
        d *= 2
    carry = h_ref[...]
    groups = []
    for g in range(tm // SUBLANES):
        rows = slice(g * SUBLANES, (g + 1) * SUBLANES)
        hg = b[rows] + a[rows] * carry
        carry = hg[SUBLANES - 1:]
        groups.append(hg)
    h_ref[...] = carry
    h = jnp.concatenate(groups, axis=0)
    return (h * _silu(cg)).astype(jnp.bfloat16)


N_STATE = S5_GROUPS * S5_STATE
S5_TM = 512
S5_CHUNKS = BR // LANES
S5_CHUNK_STATES = N_STATE // S5_CHUNKS
SLABS_PER_CHUNK = S5_CHUNK_STATES // LANES
N_SLABS = N_STATE // LANES
SLAB_PITCH = S5_TM + SUBLANES // 2
S5_UNROLL = 8


def _s5_kernel(u_ref, g_ref, bmat_ref, ar_ref, ai_ref, cmat_ref, d_ref, wglu_ref, bglu_ref,
               y_ref, xs_ref, state_ref, *, layer):
    d_ref = d_ref.at[pl.ds(layer, 1)]
    bglu_ref = bglu_ref.at[pl.ds(layer, 1)]

    @pl.when(pl.program_id(0) == 0)
    def _():
        state_ref[...] = jnp.zeros_like(state_ref)

    u = u_ref[...]
    tm = u.shape[0]
    ub = u.astype(jnp.bfloat16)
    for j in range(S5_CHUNKS):
        bu = jnp.dot(ub[:, j * LANES:(j + 1) * LANES], bmat_ref[j], preferred_element_type=jnp.float32)
        for part in range(2):
            for cc in range(SLABS_PER_CHUNK):
                slab = part * N_SLABS + j * SLABS_PER_CHUNK + cc
                col = part * S5_CHUNK_STATES + cc * LANES
                xs_ref[pl.ds(slab * SLAB_PITCH, tm), :] = bu[:, col:col + LANES]

    n_vreg = N_SLABS // SUBLANES
    a_re = [ar_ref[v * SUBLANES:(v + 1) * SUBLANES, :] for v in range(n_vreg)]
    a_im = [ai_ref[v * SUBLANES:(v + 1) * SUBLANES, :] for v in range(n_vreg)]

    def rows(part, v, t):
        return pl.ds((part * N_SLABS + v * SUBLANES) * SLAB_PITCH + t, SUBLANES, stride=SLAB_PITCH)

    def step(t, x):
        new = []
        for v in range(n_vreg):
            xr, xi = x[2 * v], x[2 * v + 1]
            nr = (a_re[v] * xr + xs_ref[rows(0, v, t), :]) - a_im[v] * xi
            ni = (a_re[v] * xi + xs_ref[rows(1, v, t), :]) + a_im[v] * xr
            xs_ref[rows(0, v, t), :] = nr
            xs_ref[rows(1, v, t), :] = ni
            new += [nr, ni]
        return tuple(new)

    x0 = tuple(state_ref[i] for i in range(2 * n_vreg))
    x1 = lax.fori_loop(0, tm, step, x0, unroll=S5_UNROLL)
    for i in range(2 * n_vreg):
        state_ref[i] = x1[i]

    ys = []
    for j in range(S5_CHUNKS):
        xcat = jnp.concatenate(
            [xs_ref[pl.ds((part * N_SLABS + j * SLABS_PER_CHUNK + cc) * SLAB_PITCH, tm), :]
             for part in range(2) for cc in range(SLABS_PER_CHUNK)], axis=1)
        ys.append(jnp.dot(xcat.astype(jnp.bfloat16), cmat_ref[j], preferred_element_type=jnp.float32))
    y = jnp.concatenate(ys, axis=1) + d_ref[...] * u
    y = 0.5 * y * (1.0 + jnp.tanh(math.sqrt(2.0 / math.pi) * (y + 0.044715 * (y * y * y))))
    gate = jnp.dot(y.astype(jnp.bfloat16), wglu_ref[...], preferred_element_type=jnp.float32) + bglu_ref[...]
    y = y * _sigmoid(gate)
    y_ref[...] = (y * _silu(g_ref[...])).astype(y_ref.dtype)


def _s5_discretize(lam_re, lam_im, log_dt, b_re, b_im, c_re, c_im):
    depth = lam_re.shape[0]
    per = S5_GROUPS // S5_CHUNKS
    dt = jnp.exp(log_dt)[..., None]
    mag = jnp.exp(lam_re * dt)
    ab_re = mag * jnp.cos(lam_im * dt)
    ab_im = mag * jnp.sin(lam_im * dt)
    den = lam_re * lam_re + lam_im * lam_im
    f_re = ((ab_re - 1.0) * lam_re + ab_im * lam_im) / den
    f_im = (ab_im * lam_re - (ab_re - 1.0) * lam_im) / den
    bb_re = f_re[..., None] * b_re - f_im[..., None] * b_im
    bb_im = f_re[..., None] * b_im + f_im[..., None] * b_re
    eye = jnp.eye(per, dtype=jnp.float32)
    bb = jnp.stack([bb_re, bb_im], axis=1).reshape(depth, 2, S5_CHUNKS, per, S5_STATE, S5_CH)
    bmat = jnp.einsum('lkjgpc,gh->ljgckhp', bb, eye).reshape(depth, S5_CHUNKS, LANES, 2 * S5_CHUNK_STATES)
    cc = jnp.stack([c_re, -c_im], axis=1).reshape(depth, 2, S5_CHUNKS, per, S5_CH, S5_STATE)
    cmat = jnp.einsum('lkjgcp,gh->ljkhpgc', cc, eye).reshape(depth, S5_CHUNKS, 2 * S5_CHUNK_STATES, LANES)
    return (bmat.astype(jnp.bfloat16), cmat.astype(jnp.bfloat16),
            ab_re.reshape(depth, N_SLABS, LANES), ab_im.reshape(depth, N_SLABS, LANES))


def _s5(proj, bmat, a_re, a_im, cmat, s5_d, w_glu_bf16, b_glu, layer):
    s = proj.shape[0]
    tm = S5_TM
    full = lambda a: pl.BlockSpec(a.shape, lambda i, nd=a.ndim: (0,) * nd)
    vec = lambda a: pl.BlockSpec((None,) + a.shape[1:], lambda i, nd=a.ndim: (layer,) + (0,) * (nd - 1))
    return pl.pallas_call(
        functools.partial(_s5_kernel, layer=layer),
        grid=(s // tm,),
        in_specs=[pl.BlockSpec((tm, BR), lambda i: (i, COL_D_U)),
                  pl.BlockSpec((tm, BR), lambda i: (i, COL_D_G)),
                  vec(bmat), vec(a_re), vec(a_im), vec(cmat),
                  full(s5_d), vec(w_glu_bf16), full(b_glu)],
        out_specs=pl.BlockSpec((tm, BR), lambda i: (i, 0)),
        out_shape=jax.ShapeDtypeStruct((s, BR), jnp.bfloat16),
        scratch_shapes=[pltpu.VMEM((2 * N_SLABS * SLAB_PITCH, LANES), jnp.float32),
                        pltpu.VMEM((2 * N_SLABS // SUBLANES, SUBLANES, LANES), jnp.float32)],
        compiler_params=_params("arbitrary"),
        name="s5",
    )(proj, proj, bmat, a_re, a_im, cmat, s5_d, w_glu_bf16, b_glu)


OUT_TM = 512
OUT_SUB = 256


def _gated_conv_tile(ab, ac, ax, ag, cw_ref, hist_ref):
    u = ac * ax
    d1, d2 = _delayed(u, hist_ref, 2)
    conv = d2 * cw_ref[0:1, :]
    conv = conv + d1 * cw_ref[1:2, :]
    conv = conv + u * cw_ref[2:3, :]
    return (ab * conv * _silu(ag)).astype(jnp.bfloat16)


def _out_kernel(x_ref, ab_ref, ac_ref, ax_ref, ag_ref, caw_ref,
                cx_ref, cg_ref, ccw_ref, ccb_ref, lw_ref, lba_ref, lbx_ref, lam_ref,
                yb_ref, yd_ref, w_ref, gate_ref, g_ref, b_ref, o_ref,
                hist_a_ref, hist_c_ref, h_ref, *, layer):
    ccb_ref, lba_ref, lbx_ref, lam_ref, g_ref, b_ref = (
        r.at[pl.ds(layer, 1)] for r in (ccb_ref, lba_ref, lbx_ref, lam_ref, g_ref, b_ref))

    @pl.when(pl.program_id(0) == 0)
    def _():
        hist_a_ref[...] = jnp.zeros_like(hist_a_ref)
        hist_c_ref[...] = jnp.zeros_like(hist_c_ref)
        h_ref[...] = jnp.zeros_like(h_ref)

    for sub in range(x_ref.shape[0] // OUT_SUB):
        rows = slice(sub * OUT_SUB, (sub + 1) * OUT_SUB)
        ya = _gated_conv_tile(ab_ref[rows, :], ac_ref[rows, :], ax_ref[rows, :], ag_ref[rows, :],
                              caw_ref, hist_a_ref)
        yc = _rglru_tile(cx_ref[rows, :], cg_ref[rows, :], ccw_ref, ccb_ref, lw_ref, lba_ref, lbx_ref,
                         lam_ref, hist_c_ref, h_ref)
        y = jnp.dot(ya, w_ref[0 * BR:1 * BR, :], preferred_element_type=jnp.float32)
        y = y + jnp.dot(yb_ref[rows, :], w_ref[1 * BR:2 * BR, :], preferred_element_type=jnp.float32)
        y = y + jnp.dot(yc, w_ref[2 * BR:3 * BR, :], preferred_element_type=jnp.float32)
        y = y + jnp.dot(yd_ref[rows, :], w_ref[3 * BR:4 * BR, :], preferred_element_type=jnp.float32)
        z = ALPHA * x_ref[rows, :] + (1.0 + gate_ref[...]) * y
        mu = jnp.mean(z, axis=-1, keepdims=True)
        zc = z - mu
        var = jnp.mean(zc * zc, axis=-1, keepdims=True)
        o_ref[rows, :] = zc * lax.rsqrt(var + LN_EPS) * g_ref[...] + b_ref[...]


def _out(x, proj, conv_a, conv_c, conv_c_b, lru_w, lru_ba, lru_bx, lru_lambda, yb, yd, w_out_bf16, ada,
         ln_g, ln_b, layer):
    s, d = x.shape
    tm = OUT_TM
    col = lambda c: pl.BlockSpec((tm, BR), lambda i, c=c: (i, c))
    branch = pl.BlockSpec((tm, BR), lambda i: (i, 0))
    full = lambda a: pl.BlockSpec(a.shape, lambda i, nd=a.ndim: (0,) * nd)
    vec = lambda a: pl.BlockSpec((None,) + a.shape[1:], lambda i, nd=a.ndim: (layer,) + (0,) * (nd - 1))
    return pl.pallas_call(
        functools.partial(_out_kernel, layer=layer),
        grid=(s // tm,),
        in_specs=[pl.BlockSpec((tm, d), lambda i: (i, 0)),
                  col(COL_A_B), col(COL_A_C), col(COL_A_X), col(COL_A_G), vec(conv_a),
                  col(COL_C_X), col(COL_C_G), vec(conv_c), full(conv_c_b), vec(lru_w),
                  full(lru_ba), full(lru_bx), full(lru_lambda),
                  branch, branch,
                  pl.BlockSpec((None, 4 * BR, d), lambda i: (layer, 0, 0), pipeline_mode=pl.Buffered(1)),
                  pl.BlockSpec((None, 1, d), lambda i: (layer, 0, 2)),
                  full(ln_g), full(ln_b)],
        out_specs=pl.BlockSpec((tm, d), lambda i: (i, 0)),
        out_shape=jax.ShapeDtypeStruct((s, d), jnp.float32),
        scratch_shapes=[pltpu.VMEM((SUBLANES, BR), jnp.float32), pltpu.VMEM((SUBLANES, BR), jnp.float32),
                        pltpu.VMEM((1, BR), jnp.float32)],
        compiler_params=_params("arbitrary"),
        name="out_proj_ln",
    )(x, proj, proj, proj, proj, conv_a, proj, proj, conv_c, conv_c_b, lru_w, lru_ba, lru_bx, lru_lambda,
      yb, yd, w_out_bf16, ada, ln_g, ln_b)


def kernel(x, c, rel_bias, w_ada, b_ada, w_in, conv_a, conv_c, conv_c_b, lru_wa, lru_ba, lru_wx, lru_bx, lru_lambda, s5_lam_re, s5_lam_im, s5_log_dt, s5_b_re, s5_b_im, s5_c_re, s5_c_im, s5_d, s5_w_glu, s5_b_glu, w_out, ln_g, ln_b):
    bsz, s, d = x.shape
    assert bsz == 1 and w_in.shape == (DEPTH, d, N_IN)
    xs = x.reshape(s, d)
    w_out_bf16 = w_out.astype(jnp.bfloat16)
    w_glu_bf16 = s5_w_glu.astype(jnp.bfloat16)
    eye = jnp.eye(LRU_HEADS, dtype=jnp.float32)
    lru_w = jnp.einsum('lkhij,hg->lhikgj', jnp.stack([lru_wa, lru_wx], axis=1), eye)
    lru_w = lru_w.reshape(DEPTH, BR, 2 * BR).astype(jnp.bfloat16)
    bmat, cmat, a_re, a_im = _s5_discretize(s5_lam_re, s5_lam_im, s5_log_dt, s5_b_re, s5_b_im, s5_c_re, s5_c_im)
    ada = _ada(c.reshape(d, 1), w_ada, b_ada)

    for l in range(DEPTH):
        proj = _proj(xs, ada, w_in, l)
        yb = _attn(proj, rel_bias)
        yd = _s5(proj, bmat, a_re, a_im, cmat, s5_d, w_glu_bf16, s5_b_glu, l)
        xs = _out(xs, proj, conv_a, conv_c, conv_c_b, lru_w, lru_ba, lru_bx, lru_lambda, yb, yd, w_out_bf16, ada,
                  ln_g, ln_b, l)
    return xs.reshape(bsz, s, d)
```

```python
import functools
import math

import numpy as np
import jax
import jax.numpy as jnp
from jax import lax
from jax.experimental import pallas as pl
from jax.experimental.pallas import tpu as pltpu

BR = 512
N_IN = 12 * BR
ATT_HEADS = 8
ATT_HEAD_DIM = 64
BLK = 128
SPAN = 128
DILATIONS = (1, 4, 16)
REL_BUCKETS = 32
REL_MAX_DIST = 2048
LRU_HEADS = 8
LRU_C = 8.0
S5_CH = 16
S5_GROUPS = 32
S5_STATE = 64
DEPTH = 2
ALPHA = (2 * DEPTH) ** 0.25
LN_EPS = 1e-5

SUBLANES = 8
LANES = 128
VMEM_LIMIT = 56 * 1024 * 1024

COL_A_B, COL_A_C, COL_A_X, COL_A_G = 0, 1, 2, 3
COL_Q, COL_K, COL_V, COL_B_G = 4, 5, 6, 7
COL_C_X, COL_C_G = 8, 9
COL_D_U, COL_D_G = 10, 11


def _silu(x):
    return x * _sigmoid(x)


_sigmoid = jax.nn.sigmoid


def _log1p(x):
    w = 1.0 + x
    return jnp.where(w == 1.0, x, x * jnp.log(w) / (w - 1.0))


def _params(*sem):
    return pltpu.CompilerParams(dimension_semantics=sem, vmem_limit_bytes=VMEM_LIMIT)


def _delayed(x, prev_ref, max_delay):
    assert 0 < max_delay < SUBLANES
    prev = prev_ref[...]
    row = lax.broadcasted_iota(jnp.int32, prev.shape, 0)
    taps = []
    for d in range(1, max_delay + 1):
        rolled = pltpu.roll(x, d, 0)
        top = jnp.where(row < d, pltpu.roll(prev, d, 0), rolled[:SUBLANES])
        taps.append(jnp.concatenate([top, rolled[SUBLANES:]], axis=0))
    prev_ref[...] = x[x.shape[0] - SUBLANES:]
    return taps


def _shift_rows_fill(x, d, fill, period):
    assert 0 < d < period <= SUBLANES
    rolled = pltpu.roll(x, d, 0)
    row = lax.broadcasted_iota(jnp.int32, x.shape, 0) & (period - 1)
    return jnp.where(row < d, jnp.asarray(fill, x.dtype), rolled)


def _ada_kernel(c_ref, w_ref, b_ref, o_ref):
    cond = _silu(c_ref[...])
    bias = b_ref[pl.ds(pl.program_id(0), 1), :]
    o_ref[...] = jnp.sum(cond * w_ref[...], axis=0, keepdims=True) + bias


def _ada(c_col, w_ada, b_ada):
    d = c_col.shape[0]
    depth, _, n = w_ada.shape
    tn = 1536
    return pl.pallas_call(
        _ada_kernel,
        grid=(depth, n // tn),
        in_specs=[
            pl.BlockSpec((d, 1), lambda l, j: (0, 0)),
            pl.BlockSpec((None, d, tn), lambda l, j: (l, 0, j)),
            pl.BlockSpec((depth, tn), lambda l, j: (0, j)),
        ],
        out_specs=pl.BlockSpec((None, 1, tn), lambda l, j: (l, 0, j)),
        out_shape=jax.ShapeDtypeStruct((depth, 1, n), jnp.float32),
        compiler_params=_params("arbitrary", "arbitrary"),
        name="ada",
    )(c_col, w_ada, b_ada)


def _proj_kernel(x_ref, shift_ref, scale_ref, w_ref, o_ref):
    h = (x_ref[...] * (1.0 + scale_ref[...]) + shift_ref[...]).astype(jnp.bfloat16)
    o_ref[...] = jnp.dot(h, w_ref[...].astype(jnp.bfloat16), preferred_element_type=jnp.float32)


def _proj(x, ada, w_in, layer):
    s, d = x.shape
    n = w_in.shape[2]
    tm, tn = 2048, 512
    return pl.pallas_call(
        _proj_kernel,
        grid=(s // tm, n // tn),
        in_specs=[
            pl.BlockSpec((tm, d), lambda i, j: (i, 0)),
            pl.BlockSpec((None, 1, d), lambda i, j: (layer, 0, 0)),
            pl.BlockSpec((None, 1, d), lambda i, j: (layer, 0, 1)),
            pl.BlockSpec((None, d, tn), lambda i, j: (layer, 0, j)),
        ],
        out_specs=pl.BlockSpec((tm, tn), lambda i, j: (i, j)),
        out_shape=jax.ShapeDtypeStruct((s, n), jnp.float32),
        compiler_params=_params("arbitrary", "arbitrary"),
        name="proj",
    )(x, ada, ada, w_in)


ATT_TILE = max(DILATIONS) * BLK
ATT_BLOCKS = ATT_TILE // BLK
N_PAIRS = ATT_HEADS // 2
DEINT = 4
PLANE = ATT_TILE // DEINT
QROWS = BLK // DEINT
MIX_ROWS = 256
ATT_UNROLL = 16
MASKED = -1e30
LOG2E = math.log2(math.e)


def _t5_bucket_tables():
    assert DILATIONS == (1, 4, 16) and DEINT == 4
    i = np.arange(BLK)[:, None]
    j = np.arange(2 * BLK)[None, :]
    delta = i + BLK - j
    valid = (delta >= 0) & (delta <= SPAN)
    max_exact = REL_BUCKETS // 2
    tables = []
    for dil in DILATIONS:
        dist = np.clip(delta, 0, SPAN) * dil
        nf = np.maximum(dist, 1).astype(np.float32)
        large = max_exact + (np.log(nf / np.float32(max_exact)) / np.float32(math.log(REL_MAX_DIST / max_exact))
                             * np.float32(REL_BUCKETS - max_exact)).astype(np.int32)
        bucket = np.where(dist < max_exact, dist, np.minimum(large, REL_BUCKETS - 1))
        table = np.stack([np.where(valid & (j >= BLK), bucket, -1), np.where(valid, bucket, -1)])
        if dil == 1:
            rows = np.array([DEINT * a + r for r in range(DEINT) for a in range(QROWS)])
            cols = np.array([blk * BLK + DEINT * a + r
                             for r in range(DEINT) for blk in range(2) for a in range(QROWS)])
            table = table[:, rows][:, :, cols]
        tables.append(table)
    return jnp.asarray(np.stack(tables), jnp.int32)


def _attend(q, k, v, bias, low):
    q = (q * (LOG2E * ATT_HEAD_DIM ** -0.5)).astype(jnp.bfloat16)
    zero = jnp.zeros_like(q)
    q2 = jnp.concatenate([jnp.where(low, q, zero), jnp.where(low, zero, q)], axis=0)
    sc = lax.dot_general(q2, k.astype(jnp.bfloat16), (((1,), (1,)), ((), ())),
                         preferred_element_type=jnp.float32) + bias
    m = jnp.max(sc, axis=-1, keepdims=True)
    p = jnp.exp2(sc - m)
    l = jnp.sum(p, axis=-1, keepdims=True)
    pv = jnp.dot(p.astype(jnp.bfloat16), v.astype(jnp.bfloat16), preferred_element_type=jnp.float32)
    return (jnp.where(low, pv[:BLK], pv[BLK:]), jnp.where(low, m[:BLK], m[BLK:]),
            jnp.where(low, l[:BLK], l[BLK:]))


def _attn_kernel(rb_ref, bucket_ref, q_ref, k_ref, v_ref, g_ref, y_ref,
                 q4_ref, k4_ref, v4_ref, o4_ref, m4_ref, l4_ref, ynat_ref, bias_ref):
    pair = pl.program_id(0)
    t = pl.program_id(1)
    lane = lax.broadcasted_iota(jnp.int32, (BLK, LANES), 1)
    low = lane < ATT_HEAD_DIM
    n_pat = len(DILATIONS)

    @pl.when(t == 0)
    def _():
        k4_ref[:, 0:PLANE, :] = jnp.zeros((DEINT, PLANE, LANES), jnp.float32)
        v4_ref[:, 0:PLANE, :] = jnp.zeros((DEINT, PLANE, LANES), jnp.float32)
        for g in range(n_pat):
            bucket = bucket_ref[g, 1]
            no_prev = bucket_ref[g, 0] < 0
            hits = [bucket == b for b in range(REL_BUCKETS)]
            for half in range(2):
                acc = jnp.full(bucket.shape, MASKED, jnp.float32)
                for b in range(REL_BUCKETS):
                    acc = jnp.where(hits[b], LOG2E * rb_ref[b, 2 * pair + half], acc)
                bias_ref[g, 1, half * BLK:(half + 1) * BLK, :] = acc
                bias_ref[g, 0, half * BLK:(half + 1) * BLK, :] = jnp.where(no_prev, MASKED, acc)

    for r in range(DEINT):
        q4_ref[r] = q_ref[pl.ds(r, PLANE, stride=DEINT), :]
        k4_ref[r, PLANE:2 * PLANE, :] = k_ref[pl.ds(r, PLANE, stride=DEINT), :]
        v4_ref[r, PLANE:2 * PLANE, :] = v_ref[pl.ds(r, PLANE, stride=DEINT), :]

    def d1_qk(n):
        a0 = pl.multiple_of(n * QROWS, QROWS)
        q = jnp.concatenate([q4_ref[r, pl.ds(a0, QROWS), :] for r in range(DEINT)], axis=0)
        k = jnp.concatenate([k4_ref[r, pl.ds(PLANE - QROWS + a0, 2 * QROWS), :] for r in range(DEINT)], axis=0)
        return q, k, (t > 0) | (n > 0)

    def d1_v(n):
        a0 = pl.multiple_of(n * QROWS, QROWS)
        return jnp.concatenate([v4_ref[r, pl.ds(PLANE - QROWS + a0, 2 * QROWS), :] for r in range(DEINT)], axis=0)

    stats_refs = (o4_ref, m4_ref, l4_ref)

    def d1_store(n, stats):
        a0 = pl.multiple_of(n * QROWS, QROWS)
        for ref, val in zip(stats_refs, stats):
            for r in range(DEINT):
                ref[0, r, pl.ds(a0, QROWS), :] = val[r * QROWS:(r + 1) * QROWS]

    def d4_qk(b):
        r, n = b // DEINT, b % DEINT
        a0 = pl.multiple_of(n * BLK, BLK)
        return (q4_ref[r, pl.ds(a0, BLK), :], k4_ref[r, pl.ds(PLANE - BLK + a0, 2 * BLK), :],
                (t > 0) | (n > 0))

    def d4_v(b):
        r, n = b // DEINT, b % DEINT
        return v4_ref[r, pl.ds(PLANE - BLK + pl.multiple_of(n * BLK, BLK), 2 * BLK), :]

    def d4_store(b, stats):
        r, n = b // DEINT, b % DEINT
        a0 = pl.multiple_of(n * BLK, BLK)
        for ref, val in zip(stats_refs, stats):
            ref[1, r, pl.ds(a0, BLK), :] = val

    def d16_qk(b):
        lo, hi = b // DEINT, b % DEINT
        return (q4_ref[lo, pl.ds(hi, BLK, stride=DEINT), :],
                k4_ref[lo, pl.ds(hi, 2 * BLK, stride=DEINT), :], t > 0)

    def d16_v(b):
        lo, hi = b // DEINT, b % DEINT
        return v4_ref[lo, pl.ds(hi, 2 * BLK, stride=DEINT), :]

    def d16_store(b, stats):
        lo, hi = b // DEINT, b % DEINT
        for ref, val in zip(stats_refs, stats):
            ref[2, lo, pl.ds(hi, BLK, stride=DEINT), :] = val

    for g, (qk, vload, store) in enumerate(((d1_qk, d1_v, d1_store), (d4_qk, d4_v, d4_store),
                                            (d16_qk, d16_v, d16_store))):
        def unit(u, carry, g=g, qk=qk, vload=vload, store=store):
            q, k, has_prev = qk(u)
            store(u, _attend(q, k, vload(u), bias_ref[g, jnp.where(has_prev, 1, 0)], low))
            return carry

        lax.fori_loop(0, ATT_BLOCKS, unit, 0, unroll=ATT_UNROLL)

    for r in range(DEINT):
        def mix(c, carry, r=r):
            a0 = pl.multiple_of(c * MIX_ROWS, MIX_ROWS)
            nat = pl.ds(DEINT * a0 + r, MIX_ROWS, stride=DEINT)
            rows = pl.ds(a0, MIX_ROWS)
            ms = [m4_ref[g, r, rows, :] for g in range(n_pat)]
            m = functools.reduce(jnp.maximum, ms)
            e = [jnp.exp2(x - m) for x in ms]
            num = sum(e[g] * o4_ref[g, r, rows, :] for g in range(n_pat))
            den = sum(e[g] * l4_ref[g, r, rows, :] for g in range(n_pat))
            ynat_ref[nat, :] = num / den * _silu(g_ref[nat, :])
            return carry

        lax.fori_loop(0, PLANE // MIX_ROWS, mix, 0)

    y_ref[...] = ynat_ref[...].astype(y_ref.dtype)
    k4_ref[:, 0:PLANE, :] = k4_ref[:, PLANE:2 * PLANE, :]
    v4_ref[:, 0:PLANE, :] = v4_ref[:, PLANE:2 * PLANE, :]


def _attn(proj, rel_bias):
    s = proj.shape[0]
    assert s % ATT_TILE == 0
    lanes_per_col = BR // LANES
    col = lambda c: pl.BlockSpec((ATT_TILE, LANES), lambda p, t, c=c: (t, c * lanes_per_col + p))
    n_pat = len(DILATIONS)
    return pl.pallas_call(
        _attn_kernel,
        grid=(N_PAIRS, s // ATT_TILE),
        in_specs=[
            pl.BlockSpec(memory_space=pltpu.SMEM),
            pl.BlockSpec((n_pat, 2, BLK, 2 * BLK), lambda p, t: (0, 0, 0, 0)),
            col(COL_Q), col(COL_K), col(COL_V), col(COL_B_G),
        ],
        out_specs=pl.BlockSpec((ATT_TILE, LANES), lambda p, t: (t, p)),
        out_shape=jax.ShapeDtypeStruct((s, BR), jnp.bfloat16),
        scratch_shapes=[pltpu.VMEM((DEINT, PLANE, LANES), jnp.float32),
                        pltpu.VMEM((DEINT, 2 * PLANE, LANES), jnp.float32),
                        pltpu.VMEM((DEINT, 2 * PLANE, LANES), jnp.float32),
                        pltpu.VMEM((n_pat, DEINT, PLANE, LANES), jnp.float32),
                        pltpu.VMEM((n_pat, DEINT, PLANE, LANES), jnp.float32),
                        pltpu.VMEM((n_pat, DEINT, PLANE, LANES), jnp.float32),
                        pltpu.VMEM((ATT_TILE, LANES), jnp.float32),
                        pltpu.VMEM((n_pat, 2, 2 * BLK, 2 * BLK), jnp.float32)],
        compiler_params=_params("arbitrary", "arbitrary"),
        name="dilated_attn",
    )(rel_bias, _t5_bucket_tables(), proj, proj, proj, proj)


def _rglru_tile(cx, cg, cw_ref, cb_ref, w_ref, ba_ref, bx_ref, lam_ref, hist_ref, h_ref):
    tm = cx.shape[0]
    d1, d2, d3 = _delayed(cx, hist_ref, 3)
    xc = d3 * cw_ref[0:1, :]
    xc = xc + d2 * cw_ref[1:2, :]
    xc = xc + d1 * cw_ref[2:3, :]
    xc = xc + cx * cw_ref[3:4, :]
    xc = xc + cb_ref[...]

    z = jnp.dot(xc.astype(jnp.bfloat16), w_ref[...], preferred_element_type=jnp.float32)
    r = _sigmoid(z[:, :BR] + ba_ref[...])
    ig = _sigmoid(z[:, BR:] + bx_ref[...])
    neg_lam = -lam_ref[...]
    softplus = jnp.maximum(neg_lam, 0.0) + _log1p(jnp.exp(-jnp.abs(neg_lam)))
    log_a = -LRU_C * r * softplus
    a = jnp.exp(log_a)
    b = jnp.sqrt(jnp.tanh(-log_a) * (a * a + 1.0)) * ig * xc

    d = 1
    while d < SUBLANES:
        b = b + a * _shift_rows_fill(b, d, 0.0, period=SUBLANES)
        a = a * _shift_rows_fill(a, d, 1.0, period=SUBLANES)
        d *= 2
    carry = h_ref[...]
    groups = []
    for g in range(tm // SUBLANES):
        rows = slice(g * SUBLANES, (g + 1) * SUBLANES)
        hg = b[rows] + a[rows] * carry
        carry = hg[SUBLANES - 1:]
        groups.append(hg)
    h_ref[...] = carry
    h = jnp.concatenate(groups, axis=0)
    return (h * _silu(cg)).astype(jnp.bfloat16)


N_STATE = S5_GROUPS * S5_STATE
S5_TM = 1024
S5_CHUNKS = BR // LANES
S5_CHUNK_STATES = N_STATE // S5_CHUNKS
SLABS_PER_CHUNK = S5_CHUNK_STATES // LANES
N_SLABS = N_STATE // LANES
SLAB_PITCH = S5_TM + SUBLANES // 2
S5_UNROLL = 8


def _s5_kernel(u_ref, g_ref, bmat_ref, ar_ref, ai_ref, cmat_ref, d_ref, wglu_ref, bglu_ref,
               y_ref, xs_ref, state_ref, *, layer):
    d_ref = d_ref.at[pl.ds(layer, 1)]
    bglu_ref = bglu_ref.at[pl.ds(layer, 1)]

    @pl.when(pl.program_id(0) == 0)
    def _():
        state_ref[...] = jnp.zeros_like(state_ref)

    u = u_ref[...]
    tm = u.shape[0]
    ub = u.astype(jnp.bfloat16)
    for j in range(S5_CHUNKS):
        bu = jnp.dot(ub[:, j * LANES:(j + 1) * LANES], bmat_ref[j], preferred_element_type=jnp.float32)
        for part in range(2):
            for cc in range(SLABS_PER_CHUNK):
                slab = part * N_SLABS + j * SLABS_PER_CHUNK + cc
                col = part * S5_CHUNK_STATES + cc * LANES
                xs_ref[pl.ds(slab * SLAB_PITCH, tm), :] = bu[:, col:col + LANES]

    n_vreg = N_SLABS // SUBLANES
    a_re = [ar_ref[v * SUBLANES:(v + 1) * SUBLANES, :] for v in range(n_vreg)]
    a_im = [ai_ref[v * SUBLANES:(v + 1) * SUBLANES, :] for v in range(n_vreg)]

    def rows(part, v, t):
        return pl.ds((part * N_SLABS + v * SUBLANES) * SLAB_PITCH + t, SUBLANES, stride=SLAB_PITCH)

    def step(t, x):
        new = []
        for v in range(n_vreg):
            xr, xi = x[2 * v], x[2 * v + 1]
            nr = (a_re[v] * xr + xs_ref[rows(0, v, t), :]) - a_im[v] * xi
            ni = (a_re[v] * xi + xs_ref[rows(1, v, t), :]) + a_im[v] * xr
            xs_ref[rows(0, v, t), :] = nr
            xs_ref[rows(1, v, t), :] = ni
            new += [nr, ni]
        return tuple(new)

    x0 = tuple(state_ref[i] for i in range(2 * n_vreg))
    x1 = lax.fori_loop(0, tm, step, x0, unroll=S5_UNROLL)
    for i in range(2 * n_vreg):
        state_ref[i] = x1[i]

    ys = []
    for j in range(S5_CHUNKS):
        xcat = jnp.concatenate(
            [xs_ref[pl.ds((part * N_SLABS + j * SLABS_PER_CHUNK + cc) * SLAB_PITCH, tm), :]
             for part in range(2) for cc in range(SLABS_PER_CHUNK)], axis=1)
        ys.append(jnp.dot(xcat.astype(jnp.bfloat16), cmat_ref[j], preferred_element_type=jnp.float32))
    y = jnp.concatenate(ys, axis=1) + d_ref[...] * u
    y = 0.5 * y * (1.0 + jnp.tanh(math.sqrt(2.0 / math.pi) * (y + 0.044715 * (y * y * y))))
    gate = jnp.dot(y.astype(jnp.bfloat16), wglu_ref[...], preferred_element_type=jnp.float32) + bglu_ref[...]
    y = y * _sigmoid(gate)
    y_ref[...] = (y * _silu(g_ref[...])).astype(y_ref.dtype)


def _s5_discretize(lam_re, lam_im, log_dt, b_re, b_im, c_re, c_im):
    depth = lam_re.shape[0]
    per = S5_GROUPS // S5_CHUNKS
    dt = jnp.exp(log_dt)[..., None]
    mag = jnp.exp(lam_re * dt)
    ab_re = mag * jnp.cos(lam_im * dt)
    ab_im = mag * jnp.sin(lam_im * dt)
    den = lam_re * lam_re + lam_im * lam_im
    f_re = ((ab_re - 1.0) * lam_re + ab_im * lam_im) / den
    f_im = (ab_im * lam_re - (ab_re - 1.0) * lam_im) / den
    bb_re = f_re[..., None] * b_re - f_im[..., None] * b_im
    bb_im = f_re[..., None] * b_im + f_im[..., None] * b_re
    eye = jnp.eye(per, dtype=jnp.float32)
    bb = jnp.stack([bb_re, bb_im], axis=1).reshape(depth, 2, S5_CHUNKS, per, S5_STATE, S5_CH)
    bmat = jnp.einsum('lkjgpc,gh->ljgckhp', bb, eye).reshape(depth, S5_CHUNKS, LANES, 2 * S5_CHUNK_STATES)
    cc = jnp.stack([c_re, -c_im], axis=1).reshape(depth, 2, S5_CHUNKS, per, S5_CH, S5_STATE)
    cmat = jnp.einsum('lkjgcp,gh->ljkhpgc', cc, eye).reshape(depth, S5_CHUNKS, 2 * S5_CHUNK_STATES, LANES)
    return (bmat.astype(jnp.bfloat16), cmat.astype(jnp.bfloat16),
            ab_re.reshape(depth, N_SLABS, LANES), ab_im.reshape(depth, N_SLABS, LANES))


def _s5(proj, bmat, a_re, a_im, cmat, s5_d, w_glu_bf16, b_glu, layer):
    s = proj.shape[0]
    tm = S5_TM
    full = lambda a: pl.BlockSpec(a.shape, lambda i, nd=a.ndim: (0,) * nd)
    vec = lambda a: pl.BlockSpec((None,) + a.shape[1:], lambda i, nd=a.ndim: (layer,) + (0,) * (nd - 1))
    return pl.pallas_call(
        functools.partial(_s5_kernel, layer=layer),
        grid=(s // tm,),
        in_specs=[pl.BlockSpec((tm, BR), lambda i: (i, COL_D_U)),
                  pl.BlockSpec((tm, BR), lambda i: (i, COL_D_G)),
                  vec(bmat), vec(a_re), vec(a_im), vec(cmat),
                  full(s5_d), vec(w_glu_bf16), full(b_glu)],
        out_specs=pl.BlockSpec((tm, BR), lambda i: (i, 0)),
        out_shape=jax.ShapeDtypeStruct((s, BR), jnp.bfloat16),
        scratch_shapes=[pltpu.VMEM((2 * N_SLABS * SLAB_PITCH, LANES), jnp.float32),
                        pltpu.VMEM((2 * N_SLABS // SUBLANES, SUBLANES, LANES), jnp.float32)],
        compiler_params=_params("arbitrary"),
        name="s5",
    )(proj, proj, bmat, a_re, a_im, cmat, s5_d, w_glu_bf16, b_glu)


OUT_TM = 512
OUT_SUB = 256


def _gated_conv_tile(ab, ac, ax, ag, cw_ref, hist_ref):
    u = ac * ax
    d1, d2 = _delayed(u, hist_ref, 2)
    conv = d2 * cw_ref[0:1, :]
    conv = conv + d1 * cw_ref[1:2, :]
    conv = conv + u * cw_ref[2:3, :]
    return (ab * conv * _silu(ag)).astype(jnp.bfloat16)


def _out_kernel(x_ref, ab_ref, ac_ref, ax_ref, ag_ref, caw_ref,
                cx_ref, cg_ref, ccw_ref, ccb_ref, lw_ref, lba_ref, lbx_ref, lam_ref,
                yb_ref, yd_ref, w_ref, gate_ref, g_ref, b_ref, o_ref,
                hist_a_ref, hist_c_ref, h_ref, *, layer):
    ccb_ref, lba_ref, lbx_ref, lam_ref, g_ref, b_ref = (
        r.at[pl.ds(layer, 1)] for r in (ccb_ref, lba_ref, lbx_ref, lam_ref, g_ref, b_ref))

    @pl.when(pl.program_id(0) == 0)
    def _():
        hist_a_ref[...] = jnp.zeros_like(hist_a_ref)
        hist_c_ref[...] = jnp.zeros_like(hist_c_ref)
        h_ref[...] = jnp.zeros_like(h_ref)

    for sub in range(x_ref.shape[0] // OUT_SUB):
        rows = slice(sub * OUT_SUB, (sub + 1) * OUT_SUB)
        ya = _gated_conv_tile(ab_ref[rows, :], ac_ref[rows, :], ax_ref[rows, :], ag_ref[rows, :],
                              caw_ref, hist_a_ref)
        yc = _rglru_tile(cx_ref[rows, :], cg_ref[rows, :], ccw_ref, ccb_ref, lw_ref, lba_ref, lbx_ref,
                         lam_ref, hist_c_ref, h_ref)
        y = jnp.dot(ya, w_ref[0 * BR:1 * BR, :], preferred_element_type=jnp.float32)
        y = y + jnp.dot(yb_ref[rows, :], w_ref[1 * BR:2 * BR, :], preferred_element_type=jnp.float32)
        y = y + jnp.dot(yc, w_ref[2 * BR:3 * BR, :], preferred_element_type=jnp.float32)
        y = y + jnp.dot(yd_ref[rows, :], w_ref[3 * BR:4 * BR, :], preferred_element_type=jnp.float32)
        z = ALPHA * x_ref[rows, :] + (1.0 + gate_ref[...]) * y
        mu = jnp.mean(z, axis=-1, keepdims=True)
        zc = z - mu
        var = jnp.mean(zc * zc, axis=-1, keepdims=True)
        o_ref[rows, :] = zc * lax.rsqrt(var + LN_EPS) * g_ref[...] + b_ref[...]


def _out(x, proj, conv_a, conv_c, conv_c_b, lru_w, lru_ba, lru_bx, lru_lambda, yb, yd, w_out_bf16, ada,
         ln_g, ln_b, layer):
    s, d = x.shape
    tm = OUT_TM
    col = lambda c: pl.BlockSpec((tm, BR), lambda i, c=c: (i, c))
    branch = pl.BlockSpec((tm, BR), lambda i: (i, 0))
    full = lambda a: pl.BlockSpec(a.shape, lambda i, nd=a.ndim: (0,) * nd)
    vec = lambda a: pl.BlockSpec((None,) + a.shape[1:], lambda i, nd=a.ndim: (layer,) + (0,) * (nd - 1))
    return pl.pallas_call(
        functools.partial(_out_kernel, layer=layer),
        grid=(s // tm,),
        in_specs=[pl.BlockSpec((tm, d), lambda i: (i, 0)),
                  col(COL_A_B), col(COL_A_C), col(COL_A_X), col(COL_A_G), vec(conv_a),
                  col(COL_C_X), col(COL_C_G), vec(conv_c), full(conv_c_b), vec(lru_w),
                  full(lru_ba), full(lru_bx), full(lru_lambda),
                  branch, branch,
                  pl.BlockSpec((None, 4 * BR, d), lambda i: (layer, 0, 0), pipeline_mode=pl.Buffered(1)),
                  pl.BlockSpec((None, 1, d), lambda i: (layer, 0, 2)),
                  full(ln_g), full(ln_b)],
        out_specs=pl.BlockSpec((tm, d), lambda i: (i, 0)),
        out_shape=jax.ShapeDtypeStruct((s, d), jnp.float32),
        scratch_shapes=[pltpu.VMEM((SUBLANES, BR), jnp.float32), pltpu.VMEM((SUBLANES, BR), jnp.float32),
                        pltpu.VMEM((1, BR), jnp.float32)],
        compiler_params=_params("arbitrary"),
        name="out_proj_ln",
    )(x, proj, proj, proj, proj, conv_a, proj, proj, conv_c, conv_c_b, lru_w, lru_ba, lru_bx, lru_lambda,
      yb, yd, w_out_bf16, ada, ln_g, ln_b)


def kernel(x, c, rel_bias, w_ada, b_ada, w_in, conv_a, conv_c, conv_c_b, lru_wa, lru_ba, lru_wx, lru_bx, lru_lambda, s5_lam_re, s5_lam_im, s5_log_dt, s5_b_re, s5_b_im, s5_c_re, s5_c_im, s5_d, s5_w_glu, s5_b_glu, w_out, ln_g, ln_b):
    bsz, s, d = x.shape
    assert bsz == 1 and w_in.shape == (DEPTH, d, N_IN)
    xs = x.reshape(s, d)
    w_out_bf16 = w_out.astype(jnp.bfloat16)
    w_glu_bf16 = s5_w_glu.astype(jnp.bfloat16)
    eye = jnp.eye(LRU_HEADS, dtype=jnp.float32)
    lru_w = jnp.einsum('lkhij,hg->lhikgj', jnp.stack([lru_wa, lru_wx], axis=1), eye)
    lru_w = lru_w.reshape(DEPTH, BR, 2 * BR).astype(jnp.bfloat16)
    bmat, cmat, a_re, a_im = _s5_discretize(s5_lam_re, s5_lam_im, s5_log_dt, s5_b_re, s5_b_im, s5_c_re, s5_c_im)
    ada = _ada(c.reshape(d, 1), w_ada, b_ada)

    for l in range(DEPTH):
        proj = _proj(xs, ada, w_in, l)
        yb = _attn(proj, rel_bias)
        yd = _s5(proj, bmat, a_re, a_im, cmat, s5_d, w_glu_bf16, s5_b_glu, l)
        xs = _out(xs, proj, conv_a, conv_c, conv_c_b, lru_w, lru_ba, lru_bx, lru_lambda, yb, yd, w_out_bf16, ada,
                  ln_g, ln_b, l)
    return xs.reshape(bsz, s, d)
```

```python
import functools
import math

import numpy as np
import jax
import jax.numpy as jnp
from jax import lax
from jax.experimental import pallas as pl
from jax.experimental.pallas import tpu as pltpu

BR = 512
N_IN = 12 * BR
ATT_HEADS = 8
ATT_HEAD_DIM = 64
BLK = 128
SPAN = 128
DILATIONS = (1, 4, 16)
REL_BUCKETS = 32
REL_MAX_DIST = 2048
LRU_HEADS = 8
LRU_C = 8.0
S5_CH = 16
S5_GROUPS = 32
S5_STATE = 64
DEPTH = 2
ALPHA = (2 * DEPTH) ** 0.25
LN_EPS = 1e-5

SUBLANES = 8
LANES = 128
VMEM_LIMIT = 56 * 1024 * 1024

COL_A_B, COL_A_C, COL_A_X, COL_A_G = 0, 1, 2, 3
COL_Q, COL_K, COL_V, COL_B_G = 4, 5, 6, 7
COL_C_X, COL_C_G = 8, 9
COL_D_U, COL_D_G = 10, 11


def _silu(x):
    return x * _sigmoid(x)


_sigmoid = jax.nn.sigmoid


def _log1p(x):
    w = 1.0 + x
    return jnp.where(w == 1.0, x, x * jnp.log(w) / (w - 1.0))


def _params(*sem):
    return pltpu.CompilerParams(dimension_semantics=sem, vmem_limit_bytes=VMEM_LIMIT)


def _delayed(x, prev_ref, max_delay):
    assert 0 < max_delay < SUBLANES
    prev = prev_ref[...]
    row = lax.broadcasted_iota(jnp.int32, prev.shape, 0)
    taps = []
    for d in range(1, max_delay + 1):
        rolled = pltpu.roll(x, d, 0)
        top = jnp.where(row < d, pltpu.roll(prev, d, 0), rolled[:SUBLANES])
        taps.append(jnp.concatenate([top, rolled[SUBLANES:]], axis=0))
    prev_ref[...] = x[x.shape[0] - SUBLANES:]
    return taps


def _shift_rows_fill(x, d, fill, period):
    assert 0 < d < period <= SUBLANES
    rolled = pltpu.roll(x, d, 0)
    row = lax.broadcasted_iota(jnp.int32, x.shape, 0) & (period - 1)
    return jnp.where(row < d, jnp.asarray(fill, x.dtype), rolled)


def _ada_columns(c_ref, w_ref, b_ref, layer):
    cond = _silu(c_ref[...])
    return jnp.sum(cond * w_ref[...], axis=0, keepdims=True) + b_ref[layer:layer + 1, :]


def _ada_kernel(c_ref, w_ref, b_ref, o_ref):
    o_ref[...] = _ada_columns(c_ref, w_ref, b_ref, 0)


def _ada(c_col, w_ada, b_ada):
    d = c_col.shape[0]
    depth, _, n = w_ada.shape
    tn = 1536
    return pl.pallas_call(
        _ada_kernel,
        grid=(n // tn,),
        in_specs=[
            pl.BlockSpec((d, 1), lambda j: (0, 0)),
            pl.BlockSpec((None, d, tn), lambda j: (0, 0, j)),
            pl.BlockSpec((depth, tn), lambda j: (0, j)),
        ],
        out_specs=pl.BlockSpec((None, 1, tn), lambda j: (0, 0, j)),
        out_shape=jax.ShapeDtypeStruct((1, 1, n), jnp.float32),
        compiler_params=_params("arbitrary"),
        name="ada",
    )(c_col, w_ada, b_ada)


def _proj_kernel(x_ref, shift_ref, scale_ref, w_ref, o_ref):
    h = (x_ref[...] * (1.0 + scale_ref[...]) + shift_ref[...]).astype(jnp.bfloat16)
    o_ref[...] = jnp.dot(h, w_ref[...].astype(jnp.bfloat16), preferred_element_type=jnp.float32)


def _proj(x, ada, w_in, layer):
    s, d = x.shape
    n = w_in.shape[2]
    tm, tn = 2048, 512
    return pl.pallas_call(
        _proj_kernel,
        grid=(s // tm, n // tn),
        in_specs=[
            pl.BlockSpec((tm, d), lambda i, j: (i, 0)),
            pl.BlockSpec((None, 1, d), lambda i, j: (0, 0, 0)),
            pl.BlockSpec((None, 1, d), lambda i, j: (0, 0, 1)),
            pl.BlockSpec((None, d, tn), lambda i, j: (layer, 0, j)),
        ],
        out_specs=pl.BlockSpec((tm, tn), lambda i, j: (i, j)),
        out_shape=jax.ShapeDtypeStruct((s, n), jnp.float32),
        compiler_params=_params("arbitrary", "arbitrary"),
        name="proj",
    )(x, ada, ada, w_in)


ATT_TILE = max(DILATIONS) * BLK
ATT_BLOCKS = ATT_TILE // BLK
N_PAIRS = ATT_HEADS // 2
DEINT = 4
PLANE = ATT_TILE // DEINT
QROWS = BLK // DEINT
MIX_ROWS = 256
ATT_UNROLL = 16
MASKED = -1e30
LOG2E = math.log2(math.e)


def _t5_bucket_tables():
    assert DILATIONS == (1, 4, 16) and DEINT == 4
    i = np.arange(BLK)[:, None]
    j = np.arange(2 * BLK)[None, :]
    delta = i + BLK - j
    valid = (delta >= 0) & (delta <= SPAN)
    max_exact = REL_BUCKETS // 2
    tables = []
    for dil in DILATIONS:
        dist = np.clip(delta, 0, SPAN) * dil
        nf = np.maximum(dist, 1).astype(np.float32)
        large = max_exact + (np.log(nf / np.float32(max_exact)) / np.float32(math.log(REL_MAX_DIST / max_exact))
                             * np.float32(REL_BUCKETS - max_exact)).astype(np.int32)
        bucket = np.where(dist < max_exact, dist, np.minimum(large, REL_BUCKETS - 1))
        table = np.stack([np.where(valid & (j >= BLK), bucket, -1), np.where(valid, bucket, -1)])
        if dil == 1:
            rows = np.array([DEINT * a + r for r in range(DEINT) for a in range(QROWS)])
            cols = np.array([blk * BLK + DEINT * a + r
                             for r in range(DEINT) for blk in range(2) for a in range(QROWS)])
            table = table[:, rows][:, :, cols]
        tables.append(table)
    return jnp.asarray(np.stack(tables), jnp.int32)


def _attend(q, k, v, bias, low):
    q = (q * (LOG2E * ATT_HEAD_DIM ** -0.5)).astype(jnp.bfloat16)
    zero = jnp.zeros_like(q)
    q2 = jnp.concatenate([jnp.where(low, q, zero), jnp.where(low, zero, q)], axis=0)
    sc = lax.dot_general(q2, k.astype(jnp.bfloat16), (((1,), (1,)), ((), ())),
                         preferred_element_type=jnp.float32) + bias
    m = jnp.max(sc, axis=-1, keepdims=True)
    p = jnp.exp2(sc - m)
    l = jnp.sum(p, axis=-1, keepdims=True)
    pv = jnp.dot(p.astype(jnp.bfloat16), v.astype(jnp.bfloat16), preferred_element_type=jnp.float32)
    return (jnp.where(low, pv[:BLK], pv[BLK:]), jnp.where(low, m[:BLK], m[BLK:]),
            jnp.where(low, l[:BLK], l[BLK:]))


def _attn_kernel(rb_ref, bucket_ref, q_ref, k_ref, v_ref, g_ref, *rest, next_layer):
    if next_layer is None:
        y_ref, *scratch = rest
    else:
        c_ref, wada_ref, bada_ref, y_ref, ada_ref, *scratch = rest
        ada_ref[...] = _ada_columns(c_ref, wada_ref, bada_ref, next_layer)
    q4_ref, k4_ref, v4_ref, o4_ref, m4_ref, l4_ref, ynat_ref, bias_ref = scratch
    pair = pl.program_id(0)
    t = pl.program_id(1)
    lane = lax.broadcasted_iota(jnp.int32, (BLK, LANES), 1)
    low = lane < ATT_HEAD_DIM
    n_pat = len(DILATIONS)

    @pl.when(t == 0)
    def _():
        k4_ref[:, 0:PLANE, :] = jnp.zeros((DEINT, PLANE, LANES), jnp.float32)
        v4_ref[:, 0:PLANE, :] = jnp.zeros((DEINT, PLANE, LANES), jnp.float32)
        for g in range(n_pat):
            bucket = bucket_ref[g, 1]
            no_prev = bucket_ref[g, 0] < 0
            hits = [bucket == b for b in range(REL_BUCKETS)]
            for half in range(2):
                acc = jnp.full(bucket.shape, MASKED, jnp.float32)
                for b in range(REL_BUCKETS):
                    acc = jnp.where(hits[b], LOG2E * rb_ref[b, 2 * pair + half], acc)
                bias_ref[g, 1, half * BLK:(half + 1) * BLK, :] = acc
                bias_ref[g, 0, half * BLK:(half + 1) * BLK, :] = jnp.where(no_prev, MASKED, acc)

    for r in range(DEINT):
        q4_ref[r] = q_ref[pl.ds(r, PLANE, stride=DEINT), :]
        k4_ref[r, PLANE:2 * PLANE, :] = k_ref[pl.ds(r, PLANE, stride=DEINT), :]
        v4_ref[r, PLANE:2 * PLANE, :] = v_ref[pl.ds(r, PLANE, stride=DEINT), :]

    def d1_qk(n):
        a0 = pl.multiple_of(n * QROWS, QROWS)
        q = jnp.concatenate([q4_ref[r, pl.ds(a0, QROWS), :] for r in range(DEINT)], axis=0)
        k = jnp.concatenate([k4_ref[r, pl.ds(PLANE - QROWS + a0, 2 * QROWS), :] for r in range(DEINT)], axis=0)
        return q, k, (t > 0) | (n > 0)

    def d1_v(n):
        a0 = pl.multiple_of(n * QROWS, QROWS)
        return jnp.concatenate([v4_ref[r, pl.ds(PLANE - QROWS + a0, 2 * QROWS), :] for r in range(DEINT)], axis=0)

    stats_refs = (o4_ref, m4_ref, l4_ref)

    def d1_store(n, stats):
        a0 = pl.multiple_of(n * QROWS, QROWS)
        for ref, val in zip(stats_refs, stats):
            for r in range(DEINT):
                ref[0, r, pl.ds(a0, QROWS), :] = val[r * QROWS:(r + 1) * QROWS]

    def d4_qk(b):
        r, n = b // DEINT, b % DEINT
        a0 = pl.multiple_of(n * BLK, BLK)
        return (q4_ref[r, pl.ds(a0, BLK), :], k4_ref[r, pl.ds(PLANE - BLK + a0, 2 * BLK), :],
                (t > 0) | (n > 0))

    def d4_v(b):
        r, n = b // DEINT, b % DEINT
        return v4_ref[r, pl.ds(PLANE - BLK + pl.multiple_of(n * BLK, BLK), 2 * BLK), :]

    def d4_store(b, stats):
        r, n = b // DEINT, b % DEINT
        a0 = pl.multiple_of(n * BLK, BLK)
        for ref, val in zip(stats_refs, stats):
            ref[1, r, pl.ds(a0, BLK), :] = val

    def d16_qk(b):
        lo, hi = b // DEINT, b % DEINT
        return (q4_ref[lo, pl.ds(hi, BLK, stride=DEINT), :],
                k4_ref[lo, pl.ds(hi, 2 * BLK, stride=DEINT), :], t > 0)

    def d16_v(b):
        lo, hi = b // DEINT, b % DEINT
        return v4_ref[lo, pl.ds(hi, 2 * BLK, stride=DEINT), :]

    def d16_store(b, stats):
        lo, hi = b // DEINT, b % DEINT
        for ref, val in zip(stats_refs, stats):
            ref[2, lo, pl.ds(hi, BLK, stride=DEINT), :] = val

    for g, (qk, vload, store) in enumerate(((d1_qk, d1_v, d1_store), (d4_qk, d4_v, d4_store),
                                            (d16_qk, d16_v, d16_store))):
        def unit(u, carry, g=g, qk=qk, vload=vload, store=store):
            q, k, has_prev = qk(u)
            store(u, _attend(q, k, vload(u), bias_ref[g, jnp.where(has_prev, 1, 0)], low))
            return carry

        lax.fori_loop(0, ATT_BLOCKS, unit, 0, unroll=ATT_UNROLL)

    for r in range(DEINT):
        def mix(c, carry, r=r):
            a0 = pl.multiple_of(c * MIX_ROWS, MIX_ROWS)
            nat = pl.ds(DEINT * a0 + r, MIX_ROWS, stride=DEINT)
            rows = pl.ds(a0, MIX_ROWS)
            ms = [m4_ref[g, r, rows, :] for g in range(n_pat)]
            m = functools.reduce(jnp.maximum, ms)
            e = [jnp.exp2(x - m) for x in ms]
            num = sum(e[g] * o4_ref[g, r, rows, :] for g in range(n_pat))
            den = sum(e[g] * l4_ref[g, r, rows, :] for g in range(n_pat))
            ynat_ref[nat, :] = num / den * _silu(g_ref[nat, :])
            return carry

        lax.fori_loop(0, PLANE // MIX_ROWS, mix, 0)

    y_ref[...] = ynat_ref[...].astype(y_ref.dtype)
    k4_ref[:, 0:PLANE, :] = k4_ref[:, PLANE:2 * PLANE, :]
    v4_ref[:, 0:PLANE, :] = v4_ref[:, PLANE:2 * PLANE, :]


def _attn(proj, rel_bias, ada_args=None):
    s = proj.shape[0]
    assert s % ATT_TILE == 0
    lanes_per_col = BR // LANES
    n_tiles = s // ATT_TILE
    col = lambda c: pl.BlockSpec((ATT_TILE, LANES), lambda p, t, c=c: (t, c * lanes_per_col + p))
    n_pat = len(DILATIONS)
    in_specs = [
        pl.BlockSpec(memory_space=pltpu.SMEM),
        pl.BlockSpec((n_pat, 2, BLK, 2 * BLK), lambda p, t: (0, 0, 0, 0)),
        col(COL_Q), col(COL_K), col(COL_V), col(COL_B_G),
    ]
    out_specs = [pl.BlockSpec((ATT_TILE, LANES), lambda p, t: (t, p))]
    out_shape = [jax.ShapeDtypeStruct((s, BR), jnp.bfloat16)]
    operands = [rel_bias, _t5_bucket_tables(), proj, proj, proj, proj]
    next_layer = None
    if ada_args is not None:
        c_col, w_ada, b_ada, next_layer = ada_args
        d = c_col.shape[0]
        depth, _, n = w_ada.shape
        tn = n // (N_PAIRS * n_tiles)
        assert tn % LANES == 0
        in_specs += [pl.BlockSpec((d, 1), lambda p, t: (0, 0)),
                     pl.BlockSpec((None, d, tn), lambda p, t: (next_layer, 0, p * n_tiles + t)),
                     pl.BlockSpec((depth, tn), lambda p, t: (0, p * n_tiles + t))]
        out_specs.append(pl.BlockSpec((None, 1, tn), lambda p, t: (0, 0, p * n_tiles + t)))
        out_shape.append(jax.ShapeDtypeStruct((1, 1, n), jnp.float32))
        operands += [c_col, w_ada, b_ada]
    outs = pl.pallas_call(
        functools.partial(_attn_kernel, next_layer=next_layer),
        grid=(N_PAIRS, n_tiles),
        in_specs=in_specs,
        out_specs=out_specs,
        out_shape=out_shape,
        scratch_shapes=[pltpu.VMEM((DEINT, PLANE, LANES), jnp.float32),
                        pltpu.VMEM((DEINT, 2 * PLANE, LANES), jnp.float32),
                        pltpu.VMEM((DEINT, 2 * PLANE, LANES), jnp.float32),
                        pltpu.VMEM((n_pat, DEINT, PLANE, LANES), jnp.float32),
                        pltpu.VMEM((n_pat, DEINT, PLANE, LANES), jnp.float32),
                        pltpu.VMEM((n_pat, DEINT, PLANE, LANES), jnp.float32),
                        pltpu.VMEM((ATT_TILE, LANES), jnp.float32),
                        pltpu.VMEM((n_pat, 2, 2 * BLK, 2 * BLK), jnp.float32)],
        compiler_params=_params("arbitrary", "arbitrary"),
        name="dilated_attn",
    )(*operands)
    return outs[0] if ada_args is None else tuple(outs)


def _rglru_tile(cx, cg, cw_ref, cb_ref, w_ref, ba_ref, bx_ref, lam_ref, hist_ref, h_ref):
    tm = cx.shape[0]
    d1, d2, d3 = _delayed(cx, hist_ref, 3)
    xc = d3 * cw_ref[0:1, :]
    xc = xc + d2 * cw_ref[1:2, :]
    xc = xc + d1 * cw_ref[2:3, :]
    xc = xc + cx * cw_ref[3:4, :]
    xc = xc + cb_ref[...]

    z = jnp.dot(xc.astype(jnp.bfloat16), w_ref[...], preferred_element_type=jnp.float32)
    r = _sigmoid(z[:, :BR] + ba_ref[...])
    ig = _sigmoid(z[:, BR:] + bx_ref[...])
    neg_lam = -lam_ref[...]
    softplus = jnp.maximum(neg_lam, 0.0) + _log1p(jnp.exp(-jnp.abs(neg_lam)))
    log_a = -LRU_C * r * softplus
    a = jnp.exp(log_a)
    b = jnp.sqrt(jnp.tanh(-log_a) * (a * a + 1.0)) * ig * xc

    d = 1
    while d < SUBLANES:
        b = b + a * _shift_rows_fill(b, d, 0.0, period=SUBLANES)
        a = a * _shift_rows_fill(a, d, 1.0, period=SUBLANES)
        d *= 2
    carry = h_ref[...]
    groups = []
    for g in range(tm // SUBLANES):
        rows = slice(g * SUBLANES, (g + 1) * SUBLANES)
        hg = b[rows] + a[rows] * carry
        carry = hg[SUBLANES - 1:]
        groups.append(hg)
    h_ref[...] = carry
    h = jnp.concatenate(groups, axis=0)
    return (h * _silu(cg)).astype(jnp.bfloat16)


N_STATE = S5_GROUPS * S5_STATE
S5_TM = 1024
S5_CHUNKS = BR // LANES
S5_CHUNK_STATES = N_STATE // S5_CHUNKS
SLABS_PER_CHUNK = S5_CHUNK_STATES // LANES
N_SLABS = N_STATE // LANES
SLAB_PITCH = S5_TM + SUBLANES // 2
S5_UNROLL = 8


def _s5_kernel(u_ref, g_ref, bmat_ref, ar_ref, ai_ref, cmat_ref, d_ref, wglu_ref, bglu_ref,
               y_ref, xs_ref, state_ref, *, layer):
    d_ref = d_ref.at[pl.ds(layer, 1)]
    bglu_ref = bglu_ref.at[pl.ds(layer, 1)]

    @pl.when(pl.program_id(0) == 0)
    def _():
        state_ref[...] = jnp.zeros_like(state_ref)

    u = u_ref[...]
    tm = u.shape[0]
    ub = u.astype(jnp.bfloat16)
    for j in range(S5_CHUNKS):
        bu = jnp.dot(ub[:, j * LANES:(j + 1) * LANES], bmat_ref[j], preferred_element_type=jnp.float32)
        for part in range(2):
            for cc in range(SLABS_PER_CHUNK):
                slab = part * N_SLABS + j * SLABS_PER_CHUNK + cc
                col = part * S5_CHUNK_STATES + cc * LANES
                xs_ref[pl.ds(slab * SLAB_PITCH, tm), :] = bu[:, col:col + LANES]

    n_vreg = N_SLABS // SUBLANES
    a_re = [ar_ref[v * SUBLANES:(v + 1) * SUBLANES, :] for v in range(n_vreg)]
    a_im = [ai_ref[v * SUBLANES:(v + 1) * SUBLANES, :] for v in range(n_vreg)]

    def rows(part, v, t):
        return pl.ds((part * N_SLABS + v * SUBLANES) * SLAB_PITCH + t, SUBLANES, stride=SLAB_PITCH)

    def step(t, x):
        new = []
        for v in range(n_vreg):
            xr, xi = x[2 * v], x[2 * v + 1]
            nr = (a_re[v] * xr + xs_ref[rows(0, v, t), :]) - a_im[v] * xi
            ni = (a_re[v] * xi + xs_ref[rows(1, v, t), :]) + a_im[v] * xr
            xs_ref[rows(0, v, t), :] = nr
            xs_ref[rows(1, v, t), :] = ni
            new += [nr, ni]
        return tuple(new)

    x0 = tuple(state_ref[i] for i in range(2 * n_vreg))
    x1 = lax.fori_loop(0, tm, step, x0, unroll=S5_UNROLL)
    for i in range(2 * n_vreg):
        state_ref[i] = x1[i]

    ys = []
    for j in range(S5_CHUNKS):
        xcat = jnp.concatenate(
            [xs_ref[pl.ds((part * N_SLABS + j * SLABS_PER_CHUNK + cc) * SLAB_PITCH, tm), :]
             for part in range(2) for cc in range(SLABS_PER_CHUNK)], axis=1)
        ys.append(jnp.dot(xcat.astype(jnp.bfloat16), cmat_ref[j], preferred_element_type=jnp.float32))
    y = jnp.concatenate(ys, axis=1) + d_ref[...] * u
    y = 0.5 * y * (1.0 + jnp.tanh(math.sqrt(2.0 / math.pi) * (y + 0.044715 * (y * y * y))))
    gate = jnp.dot(y.astype(jnp.bfloat16), wglu_ref[...], preferred_element_type=jnp.float32) + bglu_ref[...]
    y = y * _sigmoid(gate)
    y_ref[...] = (y * _silu(g_ref[...])).astype(y_ref.dtype)


def _s5_discretize(lam_re, lam_im, log_dt, b_re, b_im, c_re, c_im):
    depth = lam_re.shape[0]
    per = S5_GROUPS // S5_CHUNKS
    dt = jnp.exp(log_dt)[..., None]
    mag = jnp.exp(lam_re * dt)
    ab_re = mag * jnp.cos(lam_im * dt)
    ab_im = mag * jnp.sin(lam_im * dt)
    den = lam_re * lam_re + lam_im * lam_im
    f_re = ((ab_re - 1.0) * lam_re + ab_im * lam_im) / den
    f_im = (ab_im * lam_re - (ab_re - 1.0) * lam_im) / den
    bb_re = f_re[..., None] * b_re - f_im[..., None] * b_im
    bb_im = f_re[..., None] * b_im + f_im[..., None] * b_re
    eye = jnp.eye(per, dtype=jnp.float32)
    bb = jnp.stack([bb_re, bb_im], axis=1).reshape(depth, 2, S5_CHUNKS, per, S5_STATE, S5_CH)
    bmat = jnp.einsum('lkjgpc,gh->ljgckhp', bb, eye).reshape(depth, S5_CHUNKS, LANES, 2 * S5_CHUNK_STATES)
    cc = jnp.stack([c_re, -c_im], axis=1).reshape(depth, 2, S5_CHUNKS, per, S5_CH, S5_STATE)
    cmat = jnp.einsum('lkjgcp,gh->ljkhpgc', cc, eye).reshape(depth, S5_CHUNKS, 2 * S5_CHUNK_STATES, LANES)
    return (bmat.astype(jnp.bfloat16), cmat.astype(jnp.bfloat16),
            ab_re.reshape(depth, N_SLABS, LANES), ab_im.reshape(depth, N_SLABS, LANES))


def _s5(proj, bmat, a_re, a_im, cmat, s5_d, w_glu_bf16, b_glu, layer):
    s = proj.shape[0]
    tm = S5_TM
    full = lambda a: pl.BlockSpec(a.shape, lambda i, nd=a.ndim: (0,) * nd)
    vec = lambda a: pl.BlockSpec((None,) + a.shape[1:], lambda i, nd=a.ndim: (layer,) + (0,) * (nd - 1))
    return pl.pallas_call(
        functools.partial(_s5_kernel, layer=layer),
        grid=(s // tm,),
        in_specs=[pl.BlockSpec((tm, BR), lambda i: (i, COL_D_U)),
                  pl.BlockSpec((tm, BR), lambda i: (i, COL_D_G)),
                  vec(bmat), vec(a_re), vec(a_im), vec(cmat),
                  full(s5_d), vec(w_glu_bf16), full(b_glu)],
        out_specs=pl.BlockSpec((tm, BR), lambda i: (i, 0)),
        out_shape=jax.ShapeDtypeStruct((s, BR), jnp.bfloat16),
        scratch_shapes=[pltpu.VMEM((2 * N_SLABS * SLAB_PITCH, LANES), jnp.float32),
                        pltpu.VMEM((2 * N_SLABS // SUBLANES, SUBLANES, LANES), jnp.float32)],
        compiler_params=_params("arbitrary"),
        name="s5",
    )(proj, proj, bmat, a_re, a_im, cmat, s5_d, w_glu_bf16, b_glu)


OUT_TM = 512
OUT_SUB = 256


def _gated_conv_tile(ab, ac, ax, ag, cw_ref, hist_ref):
    u = ac * ax
    d1, d2 = _delayed(u, hist_ref, 2)
    conv = d2 * cw_ref[0:1, :]
    conv = conv + d1 * cw_ref[1:2, :]
    conv = conv + u * cw_ref[2:3, :]
    return (ab * conv * _silu(ag)).astype(jnp.bfloat16)


def _out_kernel(x_ref, ab_ref, ac_ref, ax_ref, ag_ref, caw_ref,
                cx_ref, cg_ref, ccw_ref, ccb_ref, lw_ref, lba_ref, lbx_ref, lam_ref,
                yb_ref, yd_ref, w_ref, gate_ref, g_ref, b_ref, o_ref,
                hist_a_ref, hist_c_ref, h_ref, *, layer):
    ccb_ref, lba_ref, lbx_ref, lam_ref, g_ref, b_ref = (
        r.at[pl.ds(layer, 1)] for r in (ccb_ref, lba_ref, lbx_ref, lam_ref, g_ref, b_ref))

    @pl.when(pl.program_id(0) == 0)
    def _():
        hist_a_ref[...] = jnp.zeros_like(hist_a_ref)
        hist_c_ref[...] = jnp.zeros_like(hist_c_ref)
        h_ref[...] = jnp.zeros_like(h_ref)

    for sub in range(x_ref.shape[0] // OUT_SUB):
        rows = slice(sub * OUT_SUB, (sub + 1) * OUT_SUB)
        ya = _gated_conv_tile(ab_ref[rows, :], ac_ref[rows, :], ax_ref[rows, :], ag_ref[rows, :],
                              caw_ref, hist_a_ref)
        yc = _rglru_tile(cx_ref[rows, :], cg_ref[rows, :], ccw_ref, ccb_ref, lw_ref, lba_ref, lbx_ref,
                         lam_ref, hist_c_ref, h_ref)
        y = jnp.dot(ya, w_ref[0 * BR:1 * BR, :], preferred_element_type=jnp.float32)
        y = y + jnp.dot(yb_ref[rows, :], w_ref[1 * BR:2 * BR, :], preferred_element_type=jnp.float32)
        y = y + jnp.dot(yc, w_ref[2 * BR:3 * BR, :], preferred_element_type=jnp.float32)
        y = y + jnp.dot(yd_ref[rows, :], w_ref[3 * BR:4 * BR, :], preferred_element_type=jnp.float32)
        z = ALPHA * x_ref[rows, :] + (1.0 + gate_ref[...]) * y
        mu = jnp.mean(z, axis=-1, keepdims=True)
        zc = z - mu
        var = jnp.mean(zc * zc, axis=-1, keepdims=True)
        o_ref[rows, :] = zc * lax.rsqrt(var + LN_EPS) * g_ref[...] + b_ref[...]


def _out(x, proj, conv_a, conv_c, conv_c_b, lru_w, lru_ba, lru_bx, lru_lambda, yb, yd, w_out_bf16, ada,
         ln_g, ln_b, layer):
    s, d = x.shape
    tm = OUT_TM
    col = lambda c: pl.BlockSpec((tm, BR), lambda i, c=c: (i, c))
    branch = pl.BlockSpec((tm, BR), lambda i: (i, 0))
    full = lambda a: pl.BlockSpec(a.shape, lambda i, nd=a.ndim: (0,) * nd)
    vec = lambda a: pl.BlockSpec((None,) + a.shape[1:], lambda i, nd=a.ndim: (layer,) + (0,) * (nd - 1))
    return pl.pallas_call(
        functools.partial(_out_kernel, layer=layer),
        grid=(s // tm,),
        in_specs=[pl.BlockSpec((tm, d), lambda i: (i, 0)),
                  col(COL_A_B), col(COL_A_C), col(COL_A_X), col(COL_A_G), vec(conv_a),
                  col(COL_C_X), col(COL_C_G), vec(conv_c), full(conv_c_b), vec(lru_w),
                  full(lru_ba), full(lru_bx), full(lru_lambda),
                  branch, branch,
                  pl.BlockSpec((None, 4 * BR, d), lambda i: (layer, 0, 0), pipeline_mode=pl.Buffered(1)),
                  pl.BlockSpec((None, 1, d), lambda i: (0, 0, 2)),
                  full(ln_g), full(ln_b)],
        out_specs=pl.BlockSpec((tm, d), lambda i: (i, 0)),
        out_shape=jax.ShapeDtypeStruct((s, d), jnp.float32),
        scratch_shapes=[pltpu.VMEM((SUBLANES, BR), jnp.float32), pltpu.VMEM((SUBLANES, BR), jnp.float32),
                        pltpu.VMEM((1, BR), jnp.float32)],
        compiler_params=_params("arbitrary"),
        name="out_proj_ln",
    )(x, proj, proj, proj, proj, conv_a, proj, proj, conv_c, conv_c_b, lru_w, lru_ba, lru_bx, lru_lambda,
      yb, yd, w_out_bf16, ada, ln_g, ln_b)


def kernel(x, c, rel_bias, w_ada, b_ada, w_in, conv_a, conv_c, conv_c_b, lru_wa, lru_ba, lru_wx, lru_bx, lru_lambda, s5_lam_re, s5_lam_im, s5_log_dt, s5_b_re, s5_b_im, s5_c_re, s5_c_im, s5_d, s5_w_glu, s5_b_glu, w_out, ln_g, ln_b):
    bsz, s, d = x.shape
    assert bsz == 1 and w_in.shape == (DEPTH, d, N_IN)
    xs = x.reshape(s, d)
    w_out_bf16 = w_out.astype(jnp.bfloat16)
    w_glu_bf16 = s5_w_glu.astype(jnp.bfloat16)
    eye = jnp.eye(LRU_HEADS, dtype=jnp.float32)
    lru_w = jnp.einsum('lkhij,hg->lhikgj', jnp.stack([lru_wa, lru_wx], axis=1), eye)
    lru_w = lru_w.reshape(DEPTH, BR, 2 * BR).astype(jnp.bfloat16)
    bmat, cmat, a_re, a_im = _s5_discretize(s5_lam_re, s5_lam_im, s5_log_dt, s5_b_re, s5_b_im, s5_c_re, s5_c_im)
    c_col = c.reshape(d, 1)
    ada = _ada(c_col, w_ada, b_ada)

    for l in range(DEPTH):
        proj = _proj(xs, ada, w_in, l)
        if l + 1 < DEPTH:
            yb, ada_next = _attn(proj, rel_bias, (c_col, w_ada, b_ada, l + 1))
        else:
            yb, ada_next = _attn(proj, rel_bias), None
        yd = _s5(proj, bmat, a_re, a_im, cmat, s5_d, w_glu_bf16, s5_b_glu, l)
        xs = _out(xs, proj, conv_a, conv_c, conv_c_b, lru_w, lru_ba, lru_bx, lru_lambda, yb, yd, w_out_bf16, ada,
                  ln_g, ln_b, l)
        ada = ada_next
    return xs.reshape(bsz, s, d)
```

```python
import functools
import math

import numpy as np
import jax
import jax.numpy as jnp
from jax import lax
from jax.experimental import pallas as pl
from jax.experimental.pallas import tpu as pltpu

BR = 512
N_IN = 12 * BR
ATT_HEADS = 8
ATT_HEAD_DIM = 64
BLK = 128
SPAN = 128
DILATIONS = (1, 4, 16)
REL_BUCKETS = 32
REL_MAX_DIST = 2048
LRU_HEADS = 8
LRU_C = 8.0
S5_CH = 16
S5_GROUPS = 32
S5_STATE = 64
DEPTH = 2
ALPHA = (2 * DEPTH) ** 0.25
LN_EPS = 1e-5

SUBLANES = 8
LANES = 128
VMEM_LIMIT = 56 * 1024 * 1024

COL_A_B, COL_A_C, COL_A_X, COL_A_G = 0, 1, 2, 3
COL_Q, COL_K, COL_V, COL_B_G = 4, 5, 6, 7
COL_C_X, COL_C_G = 8, 9
COL_D_U, COL_D_G = 10, 11


def _silu(x):
    return x * _sigmoid(x)


_sigmoid = jax.nn.sigmoid


def _log1p(x):
    w = 1.0 + x
    return jnp.where(w == 1.0, x, x * jnp.log(w) / (w - 1.0))


def _params(*sem):
    return pltpu.CompilerParams(dimension_semantics=sem, vmem_limit_bytes=VMEM_LIMIT)


def _delayed(x, prev_ref, max_delay):
    assert 0 < max_delay < SUBLANES
    prev = prev_ref[...]
    row = lax.broadcasted_iota(jnp.int32, prev.shape, 0)
    taps = []
    for d in range(1, max_delay + 1):
        rolled = pltpu.roll(x, d, 0)
        top = jnp.where(row < d, pltpu.roll(prev, d, 0), rolled[:SUBLANES])
        taps.append(jnp.concatenate([top, rolled[SUBLANES:]], axis=0))
    prev_ref[...] = x[x.shape[0] - SUBLANES:]
    return taps


def _shift_rows_fill(x, d, fill, period):
    assert 0 < d < period <= SUBLANES
    rolled = pltpu.roll(x, d, 0)
    row = lax.broadcasted_iota(jnp.int32, x.shape, 0) & (period - 1)
    return jnp.where(row < d, jnp.asarray(fill, x.dtype), rolled)


def _ada_columns(c_ref, w_ref, b_ref, layer):
    cond = _silu(c_ref[...])
    return jnp.sum(cond * w_ref[...], axis=0, keepdims=True) + b_ref[layer:layer + 1, :]


def _ada_kernel(c_ref, w_ref, b_ref, o_ref):
    o_ref[...] = _ada_columns(c_ref, w_ref, b_ref, 0)


def _ada(c_col, w_ada, b_ada):
    d = c_col.shape[0]
    depth, _, n = w_ada.shape
    tn = 1536
    return pl.pallas_call(
        _ada_kernel,
        grid=(n // tn,),
        in_specs=[
            pl.BlockSpec((d, 1), lambda j: (0, 0)),
            pl.BlockSpec((None, d, tn), lambda j: (0, 0, j)),
            pl.BlockSpec((depth, tn), lambda j: (0, j)),
        ],
        out_specs=pl.BlockSpec((None, 1, tn), lambda j: (0, 0, j)),
        out_shape=jax.ShapeDtypeStruct((1, 1, n), jnp.float32),
        compiler_params=_params("arbitrary"),
        name="ada",
    )(c_col, w_ada, b_ada)


def _proj_kernel(x_ref, shift_ref, scale_ref, w_ref, o_ref):
    h = (x_ref[...] * (1.0 + scale_ref[...]) + shift_ref[...]).astype(jnp.bfloat16)
    o_ref[...] = jnp.dot(h, w_ref[...].astype(jnp.bfloat16), preferred_element_type=jnp.float32)


def _proj(x, ada, w_in, layer):
    s, d = x.shape
    n = w_in.shape[2]
    tm, tn = 2048, 512
    return pl.pallas_call(
        _proj_kernel,
        grid=(s // tm, n // tn),
        in_specs=[
            pl.BlockSpec((tm, d), lambda i, j: (i, 0)),
            pl.BlockSpec((None, 1, d), lambda i, j: (0, 0, 0)),
            pl.BlockSpec((None, 1, d), lambda i, j: (0, 0, 1)),
            pl.BlockSpec((None, d, tn), lambda i, j: (layer, 0, j)),
        ],
        out_specs=pl.BlockSpec((tm, tn), lambda i, j: (i, j)),
        out_shape=jax.ShapeDtypeStruct((s, n), jnp.float32),
        compiler_params=_params("arbitrary", "arbitrary"),
        name="proj",
    )(x, ada, ada, w_in)


ATT_TILE = max(DILATIONS) * BLK
ATT_BLOCKS = ATT_TILE // BLK
N_PAIRS = ATT_HEADS // 2
DEINT = 4
PLANE = ATT_TILE // DEINT
QROWS = BLK // DEINT
MIX_ROWS = 256
ATT_UNROLL = 16
MASKED = -1e30
LOG2E = math.log2(math.e)


def _t5_bucket_tables():
    assert DILATIONS == (1, 4, 16) and DEINT == 4
    i = np.arange(BLK)[:, None]
    j = np.arange(2 * BLK)[None, :]
    delta = i + BLK - j
    valid = (delta >= 0) & (delta <= SPAN)
    max_exact = REL_BUCKETS // 2
    tables = []
    for dil in DILATIONS:
        dist = np.clip(delta, 0, SPAN) * dil
        nf = np.maximum(dist, 1).astype(np.float32)
        large = max_exact + (np.log(nf / np.float32(max_exact)) / np.float32(math.log(REL_MAX_DIST / max_exact))
                             * np.float32(REL_BUCKETS - max_exact)).astype(np.int32)
        bucket = np.where(dist < max_exact, dist, np.minimum(large, REL_BUCKETS - 1))
        table = np.stack([np.where(valid & (j >= BLK), bucket, -1), np.where(valid, bucket, -1)])
        if dil == 1:
            rows = np.array([DEINT * a + r for r in range(DEINT) for a in range(QROWS)])
            cols = np.array([blk * BLK + DEINT * a + r
                             for r in range(DEINT) for blk in range(2) for a in range(QROWS)])
            table = table[:, rows][:, :, cols]
        tables.append(table)
    return jnp.asarray(np.stack(tables), jnp.int32)


def _attend(q, k, v, bias, low):
    q = (q * (LOG2E * ATT_HEAD_DIM ** -0.5)).astype(jnp.bfloat16)
    zero = jnp.zeros_like(q)
    q2 = jnp.concatenate([jnp.where(low, q, zero), jnp.where(low, zero, q)], axis=0)
    sc = lax.dot_general(q2, k.astype(jnp.bfloat16), (((1,), (1,)), ((), ())),
                         preferred_element_type=jnp.float32) + bias
    m = jnp.max(sc, axis=-1, keepdims=True)
    p = jnp.exp2(sc - m)
    l = jnp.sum(p, axis=-1, keepdims=True)
    pv = jnp.dot(p.astype(jnp.bfloat16), v.astype(jnp.bfloat16), preferred_element_type=jnp.float32)
    return (jnp.where(low, pv[:BLK], pv[BLK:]), jnp.where(low, m[:BLK], m[BLK:]),
            jnp.where(low, l[:BLK], l[BLK:]))


def _attn_kernel(rb_ref, bucket_ref, q_ref, k_ref, v_ref, g_ref, *rest, next_layer):
    if next_layer is None:
        y_ref, *scratch = rest
    else:
        c_ref, wada_ref, bada_ref, wout_ref, y_ref, ada_ref, wout_bf16_ref, *scratch = rest
        ada_ref[...] = _ada_columns(c_ref, wada_ref, bada_ref, next_layer)
        wout_bf16_ref[...] = wout_ref[...].astype(jnp.bfloat16)
    q4_ref, k4_ref, v4_ref, o4_ref, m4_ref, l4_ref, ynat_ref, bias_ref = scratch
    pair = pl.program_id(0)
    t = pl.program_id(1)
    lane = lax.broadcasted_iota(jnp.int32, (BLK, LANES), 1)
    low = lane < ATT_HEAD_DIM
    n_pat = len(DILATIONS)

    @pl.when(t == 0)
    def _():
        k4_ref[:, 0:PLANE, :] = jnp.zeros((DEINT, PLANE, LANES), jnp.float32)
        v4_ref[:, 0:PLANE, :] = jnp.zeros((DEINT, PLANE, LANES), jnp.float32)
        for g in range(n_pat):
            bucket = bucket_ref[g, 1]
            no_prev = bucket_ref[g, 0] < 0
            hits = [bucket == b for b in range(REL_BUCKETS)]
            for half in range(2):
                acc = jnp.full(bucket.shape, MASKED, jnp.float32)
                for b in range(REL_BUCKETS):
                    acc = jnp.where(hits[b], LOG2E * rb_ref[b, 2 * pair + half], acc)
                bias_ref[g, 1, half * BLK:(half + 1) * BLK, :] = acc
                bias_ref[g, 0, half * BLK:(half + 1) * BLK, :] = jnp.where(no_prev, MASKED, acc)

    for r in range(DEINT):
        q4_ref[r] = q_ref[pl.ds(r, PLANE, stride=DEINT), :]
        k4_ref[r, PLANE:2 * PLANE, :] = k_ref[pl.ds(r, PLANE, stride=DEINT), :]
        v4_ref[r, PLANE:2 * PLANE, :] = v_ref[pl.ds(r, PLANE, stride=DEINT), :]

    def d1_qk(n):
        a0 = pl.multiple_of(n * QROWS, QROWS)
        q = jnp.concatenate([q4_ref[r, pl.ds(a0, QROWS), :] for r in range(DEINT)], axis=0)
        k = jnp.concatenate([k4_ref[r, pl.ds(PLANE - QROWS + a0, 2 * QROWS), :] for r in range(DEINT)], axis=0)
        return q, k, (t > 0) | (n > 0)

    def d1_v(n):
        a0 = pl.multiple_of(n * QROWS, QROWS)
        return jnp.concatenate([v4_ref[r, pl.ds(PLANE - QROWS + a0, 2 * QROWS), :] for r in range(DEINT)], axis=0)

    stats_refs = (o4_ref, m4_ref, l4_ref)

    def d1_store(n, stats):
        a0 = pl.multiple_of(n * QROWS, QROWS)
        for ref, val in zip(stats_refs, stats):
            for r in range(DEINT):
                ref[0, r, pl.ds(a0, QROWS), :] = val[r * QROWS:(r + 1) * QROWS]

    def d4_qk(b):
        r, n = b // DEINT, b % DEINT
        a0 = pl.multiple_of(n * BLK, BLK)
        return (q4_ref[r, pl.ds(a0, BLK), :], k4_ref[r, pl.ds(PLANE - BLK + a0, 2 * BLK), :],
                (t > 0) | (n > 0))

    def d4_v(b):
        r, n = b // DEINT, b % DEINT
        return v4_ref[r, pl.ds(PLANE - BLK + pl.multiple_of(n * BLK, BLK), 2 * BLK), :]

    def d4_store(b, stats):
        r, n = b // DEINT, b % DEINT
        a0 = pl.multiple_of(n * BLK, BLK)
        for ref, val in zip(stats_refs, stats):
            ref[1, r, pl.ds(a0, BLK), :] = val

    def d16_qk(b):
        lo, hi = b // DEINT, b % DEINT
        return (q4_ref[lo, pl.ds(hi, BLK, stride=DEINT), :],
                k4_ref[lo, pl.ds(hi, 2 * BLK, stride=DEINT), :], t > 0)

    def d16_v(b):
        lo, hi = b // DEINT, b % DEINT
        return v4_ref[lo, pl.ds(hi, 2 * BLK, stride=DEINT), :]

    def d16_store(b, stats):
        lo, hi = b // DEINT, b % DEINT
        for ref, val in zip(stats_refs, stats):
            ref[2, lo, pl.ds(hi, BLK, stride=DEINT), :] = val

    for g, (qk, vload, store) in enumerate(((d1_qk, d1_v, d1_store), (d4_qk, d4_v, d4_store),
                                            (d16_qk, d16_v, d16_store))):
        def unit(u, carry, g=g, qk=qk, vload=vload, store=store):
            q, k, has_prev = qk(u)
            store(u, _attend(q, k, vload(u), bias_ref[g, jnp.where(has_prev, 1, 0)], low))
            return carry

        lax.fori_loop(0, ATT_BLOCKS, unit, 0, unroll=ATT_UNROLL)

    for r in range(DEINT):
        def mix(c, carry, r=r):
            a0 = pl.multiple_of(c * MIX_ROWS, MIX_ROWS)
            nat = pl.ds(DEINT * a0 + r, MIX_ROWS, stride=DEINT)
            rows = pl.ds(a0, MIX_ROWS)
            ms = [m4_ref[g, r, rows, :] for g in range(n_pat)]
            m = functools.reduce(jnp.maximum, ms)
            e = [jnp.exp2(x - m) for x in ms]
            num = sum(e[g] * o4_ref[g, r, rows, :] for g in range(n_pat))
            den = sum(e[g] * l4_ref[g, r, rows, :] for g in range(n_pat))
            ynat_ref[nat, :] = num / den * _silu(g_ref[nat, :])
            return carry

        lax.fori_loop(0, PLANE // MIX_ROWS, mix, 0)

    y_ref[...] = ynat_ref[...].astype(y_ref.dtype)
    k4_ref[:, 0:PLANE, :] = k4_ref[:, PLANE:2 * PLANE, :]
    v4_ref[:, 0:PLANE, :] = v4_ref[:, PLANE:2 * PLANE, :]


def _attn(proj, rel_bias, ada_args=None):
    s = proj.shape[0]
    assert s % ATT_TILE == 0
    lanes_per_col = BR // LANES
    n_tiles = s // ATT_TILE
    col = lambda c: pl.BlockSpec((ATT_TILE, LANES), lambda p, t, c=c: (t, c * lanes_per_col + p))
    n_pat = len(DILATIONS)
    in_specs = [
        pl.BlockSpec(memory_space=pltpu.SMEM),
        pl.BlockSpec((n_pat, 2, BLK, 2 * BLK), lambda p, t: (0, 0, 0, 0)),
        col(COL_Q), col(COL_K), col(COL_V), col(COL_B_G),
    ]
    out_specs = [pl.BlockSpec((ATT_TILE, LANES), lambda p, t: (t, p))]
    out_shape = [jax.ShapeDtypeStruct((s, BR), jnp.bfloat16)]
    operands = [rel_bias, _t5_bucket_tables(), proj, proj, proj, proj]
    next_layer = None
    if ada_args is not None:
        c_col, w_ada, b_ada, next_layer, w_out = ada_args
        d = c_col.shape[0]
        depth, _, n = w_ada.shape
        n_steps = N_PAIRS * n_tiles
        tn = n // n_steps
        w_out_rows = w_out.reshape(-1, w_out.shape[-1])
        tr = w_out_rows.shape[0] // n_steps
        assert tn % LANES == 0 and tr % (2 * SUBLANES) == 0
        in_specs += [pl.BlockSpec((d, 1), lambda p, t: (0, 0)),
                     pl.BlockSpec((None, d, tn), lambda p, t: (next_layer, 0, p * n_tiles + t)),
                     pl.BlockSpec((depth, tn), lambda p, t: (0, p * n_tiles + t)),
                     pl.BlockSpec((tr, w_out_rows.shape[1]), lambda p, t: (p * n_tiles + t, 0))]
        out_specs += [pl.BlockSpec((None, 1, tn), lambda p, t: (0, 0, p * n_tiles + t)),
                      pl.BlockSpec((tr, w_out_rows.shape[1]), lambda p, t: (p * n_tiles + t, 0))]
        out_shape += [jax.ShapeDtypeStruct((1, 1, n), jnp.float32),
                      jax.ShapeDtypeStruct(w_out_rows.shape, jnp.bfloat16)]
        operands += [c_col, w_ada, b_ada, w_out_rows]
    outs = pl.pallas_call(
        functools.partial(_attn_kernel, next_layer=next_layer),
        grid=(N_PAIRS, n_tiles),
        in_specs=in_specs,
        out_specs=out_specs,
        out_shape=out_shape,
        scratch_shapes=[pltpu.VMEM((DEINT, PLANE, LANES), jnp.float32),
                        pltpu.VMEM((DEINT, 2 * PLANE, LANES), jnp.float32),
                        pltpu.VMEM((DEINT, 2 * PLANE, LANES), jnp.float32),
                        pltpu.VMEM((n_pat, DEINT, PLANE, LANES), jnp.float32),
                        pltpu.VMEM((n_pat, DEINT, PLANE, LANES), jnp.float32),
                        pltpu.VMEM((n_pat, DEINT, PLANE, LANES), jnp.float32),
                        pltpu.VMEM((ATT_TILE, LANES), jnp.float32),
                        pltpu.VMEM((n_pat, 2, 2 * BLK, 2 * BLK), jnp.float32)],
        compiler_params=_params("arbitrary", "arbitrary"),
        name="dilated_attn",
    )(*operands)
    if ada_args is None:
        return outs[0]
    return outs[0], outs[1], outs[2].reshape(ada_args[4].shape)


def _rglru_tile(cx, cg, cw_ref, cb_ref, w_ref, ba_ref, bx_ref, lam_ref, hist_ref, h_ref):
    tm = cx.shape[0]
    d1, d2, d3 = _delayed(cx, hist_ref, 3)
    xc = d3 * cw_ref[0:1, :]
    xc = xc + d2 * cw_ref[1:2, :]
    xc = xc + d1 * cw_ref[2:3, :]
    xc = xc + cx * cw_ref[3:4, :]
    xc = xc + cb_ref[...]

    z = jnp.dot(xc.astype(jnp.bfloat16), w_ref[...], preferred_element_type=jnp.float32)
    r = _sigmoid(z[:, :BR] + ba_ref[...])
    ig = _sigmoid(z[:, BR:] + bx_ref[...])
    neg_lam = -lam_ref[...]
    softplus = jnp.maximum(neg_lam, 0.0) + _log1p(jnp.exp(-jnp.abs(neg_lam)))
    log_a = -LRU_C * r * softplus
    a = jnp.exp(log_a)
    b = jnp.sqrt(jnp.tanh(-log_a) * (a * a + 1.0)) * ig * xc

    d = 1
    while d < SUBLANES:
        b = b + a * _shift_rows_fill(b, d, 0.0, period=SUBLANES)
        a = a * _shift_rows_fill(a, d, 1.0, period=SUBLANES)
        d *= 2
    carry = h_ref[...]
    groups = []
    for g in range(tm // SUBLANES):
        rows = slice(g * SUBLANES, (g + 1) * SUBLANES)
        hg = b[rows] + a[rows] * carry
        carry = hg[SUBLANES - 1:]
        groups.append(hg)
    h_ref[...] = carry
    h = jnp.concatenate(groups, axis=0)
    return (h * _silu(cg)).astype(jnp.bfloat16)


N_STATE = S5_GROUPS * S5_STATE
S5_TM = 1024
S5_CHUNKS = BR // LANES
S5_CHUNK_STATES = N_STATE // S5_CHUNKS
SLABS_PER_CHUNK = S5_CHUNK_STATES // LANES
N_SLABS = N_STATE // LANES
SLAB_PITCH = S5_TM + SUBLANES // 2
S5_UNROLL = 8


def _s5_kernel(u_ref, g_ref, bmat_ref, ar_ref, ai_ref, cmat_ref, d_ref, wglu_ref, bglu_ref,
               y_ref, xs_ref, state_ref, *, layer):
    d_ref = d_ref.at[pl.ds(layer, 1)]
    bglu_ref = bglu_ref.at[pl.ds(layer, 1)]

    @pl.when(pl.program_id(0) == 0)
    def _():
        state_ref[...] = jnp.zeros_like(state_ref)

    u = u_ref[...]
    tm = u.shape[0]
    ub = u.astype(jnp.bfloat16)
    for j in range(S5_CHUNKS):
        bu = jnp.dot(ub[:, j * LANES:(j + 1) * LANES], bmat_ref[j], preferred_element_type=jnp.float32)
        for part in range(2):
            for cc in range(SLABS_PER_CHUNK):
                slab = part * N_SLABS + j * SLABS_PER_CHUNK + cc
                col = part * S5_CHUNK_STATES + cc * LANES
                xs_ref[pl.ds(slab * SLAB_PITCH, tm), :] = bu[:, col:col + LANES]

    n_vreg = N_SLABS // SUBLANES
    a_re = [ar_ref[v * SUBLANES:(v + 1) * SUBLANES, :] for v in range(n_vreg)]
    a_im = [ai_ref[v * SUBLANES:(v + 1) * SUBLANES, :] for v in range(n_vreg)]

    def rows(part, v, t):
        return pl.ds((part * N_SLABS + v * SUBLANES) * SLAB_PITCH + t, SUBLANES, stride=SLAB_PITCH)

    def step(t, x):
        new = []
        for v in range(n_vreg):
            xr, xi = x[2 * v], x[2 * v + 1]
            nr = (a_re[v] * xr + xs_ref[rows(0, v, t), :]) - a_im[v] * xi
            ni = (a_re[v] * xi + xs_ref[rows(1, v, t), :]) + a_im[v] * xr
            xs_ref[rows(0, v, t), :] = nr
            xs_ref[rows(1, v, t), :] = ni
            new += [nr, ni]
        return tuple(new)

    x0 = tuple(state_ref[i] for i in range(2 * n_vreg))
    x1 = lax.fori_loop(0, tm, step, x0, unroll=S5_UNROLL)
    for i in range(2 * n_vreg):
        state_ref[i] = x1[i]

    ys = []
    for j in range(S5_CHUNKS):
        xcat = jnp.concatenate(
            [xs_ref[pl.ds((part * N_SLABS + j * SLABS_PER_CHUNK + cc) * SLAB_PITCH, tm), :]
             for part in range(2) for cc in range(SLABS_PER_CHUNK)], axis=1)
        ys.append(jnp.dot(xcat.astype(jnp.bfloat16), cmat_ref[j], preferred_element_type=jnp.float32))
    y = jnp.concatenate(ys, axis=1) + d_ref[...] * u
    y = 0.5 * y * (1.0 + jnp.tanh(math.sqrt(2.0 / math.pi) * (y + 0.044715 * (y * y * y))))
    gate = jnp.dot(y.astype(jnp.bfloat16), wglu_ref[...], preferred_element_type=jnp.float32) + bglu_ref[...]
    y = y * _sigmoid(gate)
    y_ref[...] = (y * _silu(g_ref[...])).astype(y_ref.dtype)


def _s5_discretize(lam_re, lam_im, log_dt, b_re, b_im, c_re, c_im):
    depth = lam_re.shape[0]
    per = S5_GROUPS // S5_CHUNKS
    dt = jnp.exp(log_dt)[..., None]
    mag = jnp.exp(lam_re * dt)
    ab_re = mag * jnp.cos(lam_im * dt)
    ab_im = mag * jnp.sin(lam_im * dt)
    den = lam_re * lam_re + lam_im * lam_im
    f_re = ((ab_re - 1.0) * lam_re + ab_im * lam_im) / den
    f_im = (ab_im * lam_re - (ab_re - 1.0) * lam_im) / den
    bb_re = f_re[..., None] * b_re - f_im[..., None] * b_im
    bb_im = f_re[..., None] * b_im + f_im[..., None] * b_re
    eye = jnp.eye(per, dtype=jnp.float32)
    bb = jnp.stack([bb_re, bb_im], axis=1).reshape(depth, 2, S5_CHUNKS, per, S5_STATE, S5_CH)
    bmat = jnp.einsum('lkjgpc,gh->ljgckhp', bb, eye).reshape(depth, S5_CHUNKS, LANES, 2 * S5_CHUNK_STATES)
    cc = jnp.stack([c_re, -c_im], axis=1).reshape(depth, 2, S5_CHUNKS, per, S5_CH, S5_STATE)
    cmat = jnp.einsum('lkjgcp,gh->ljkhpgc', cc, eye).reshape(depth, S5_CHUNKS, 2 * S5_CHUNK_STATES, LANES)
    return (bmat.astype(jnp.bfloat16), cmat.astype(jnp.bfloat16),
            ab_re.reshape(depth, N_SLABS, LANES), ab_im.reshape(depth, N_SLABS, LANES))


def _s5(proj, bmat, a_re, a_im, cmat, s5_d, w_glu_bf16, b_glu, layer):
    s = proj.shape[0]
    tm = S5_TM
    full = lambda a: pl.BlockSpec(a.shape, lambda i, nd=a.ndim: (0,) * nd)
    vec = lambda a: pl.BlockSpec((None,) + a.shape[1:], lambda i, nd=a.ndim: (layer,) + (0,) * (nd - 1))
    return pl.pallas_call(
        functools.partial(_s5_kernel, layer=layer),
        grid=(s // tm,),
        in_specs=[pl.BlockSpec((tm, BR), lambda i: (i, COL_D_U)),
                  pl.BlockSpec((tm, BR), lambda i: (i, COL_D_G)),
                  vec(bmat), vec(a_re), vec(a_im), vec(cmat),
                  full(s5_d), vec(w_glu_bf16), full(b_glu)],
        out_specs=pl.BlockSpec((tm, BR), lambda i: (i, 0)),
        out_shape=jax.ShapeDtypeStruct((s, BR), jnp.bfloat16),
        scratch_shapes=[pltpu.VMEM((2 * N_SLABS * SLAB_PITCH, LANES), jnp.float32),
                        pltpu.VMEM((2 * N_SLABS // SUBLANES, SUBLANES, LANES), jnp.float32)],
        compiler_params=_params("arbitrary"),
        name="s5",
    )(proj, proj, bmat, a_re, a_im, cmat, s5_d, w_glu_bf16, b_glu)


OUT_TM = 512
OUT_SUB = 256


def _gated_conv_tile(ab, ac, ax, ag, cw_ref, hist_ref):
    u = ac * ax
    d1, d2 = _delayed(u, hist_ref, 2)
    conv = d2 * cw_ref[0:1, :]
    conv = conv + d1 * cw_ref[1:2, :]
    conv = conv + u * cw_ref[2:3, :]
    return (ab * conv * _silu(ag)).astype(jnp.bfloat16)


def _out_kernel(x_ref, ab_ref, ac_ref, ax_ref, ag_ref, caw_ref,
                cx_ref, cg_ref, ccw_ref, ccb_ref, lw_ref, lba_ref, lbx_ref, lam_ref,
                yb_ref, yd_ref, w_ref, gate_ref, g_ref, b_ref, o_ref,
                hist_a_ref, hist_c_ref, h_ref, *, layer):
    ccb_ref, lba_ref, lbx_ref, lam_ref, g_ref, b_ref = (
        r.at[pl.ds(layer, 1)] for r in (ccb_ref, lba_ref, lbx_ref, lam_ref, g_ref, b_ref))

    @pl.when(pl.program_id(0) == 0)
    def _():
        hist_a_ref[...] = jnp.zeros_like(hist_a_ref)
        hist_c_ref[...] = jnp.zeros_like(hist_c_ref)
        h_ref[...] = jnp.zeros_like(h_ref)

    for sub in range(x_ref.shape[0] // OUT_SUB):
        rows = slice(sub * OUT_SUB, (sub + 1) * OUT_SUB)
        ya = _gated_conv_tile(ab_ref[rows, :], ac_ref[rows, :], ax_ref[rows, :], ag_ref[rows, :],
                              caw_ref, hist_a_ref)
        yc = _rglru_tile(cx_ref[rows, :], cg_ref[rows, :], ccw_ref, ccb_ref, lw_ref, lba_ref, lbx_ref,
                         lam_ref, hist_c_ref, h_ref)
        y = jnp.dot(ya, w_ref[0 * BR:1 * BR, :], preferred_element_type=jnp.float32)
        y = y + jnp.dot(yb_ref[rows, :], w_ref[1 * BR:2 * BR, :], preferred_element_type=jnp.float32)
        y = y + jnp.dot(yc, w_ref[2 * BR:3 * BR, :], preferred_element_type=jnp.float32)
        y = y + jnp.dot(yd_ref[rows, :], w_ref[3 * BR:4 * BR, :], preferred_element_type=jnp.float32)
        z = ALPHA * x_ref[rows, :] + (1.0 + gate_ref[...]) * y
        mu = jnp.mean(z, axis=-1, keepdims=True)
        zc = z - mu
        var = jnp.mean(zc * zc, axis=-1, keepdims=True)
        o_ref[rows, :] = zc * lax.rsqrt(var + LN_EPS) * g_ref[...] + b_ref[...]


def _out(x, proj, conv_a, conv_c, conv_c_b, lru_w, lru_ba, lru_bx, lru_lambda, yb, yd, w_out_bf16, ada,
         ln_g, ln_b, layer):
    s, d = x.shape
    tm = OUT_TM
    col = lambda c: pl.BlockSpec((tm, BR), lambda i, c=c: (i, c))
    branch = pl.BlockSpec((tm, BR), lambda i: (i, 0))
    full = lambda a: pl.BlockSpec(a.shape, lambda i, nd=a.ndim: (0,) * nd)
    vec = lambda a: pl.BlockSpec((None,) + a.shape[1:], lambda i, nd=a.ndim: (layer,) + (0,) * (nd - 1))
    return pl.pallas_call(
        functools.partial(_out_kernel, layer=layer),
        grid=(s // tm,),
        in_specs=[pl.BlockSpec((tm, d), lambda i: (i, 0)),
                  col(COL_A_B), col(COL_A_C), col(COL_A_X), col(COL_A_G), vec(conv_a),
                  col(COL_C_X), col(COL_C_G), vec(conv_c), full(conv_c_b), vec(lru_w),
                  full(lru_ba), full(lru_bx), full(lru_lambda),
                  branch, branch,
                  pl.BlockSpec((None, 4 * BR, d), lambda i: (layer, 0, 0), pipeline_mode=pl.Buffered(1)),
                  pl.BlockSpec((None, 1, d), lambda i: (0, 0, 2)),
                  full(ln_g), full(ln_b)],
        out_specs=pl.BlockSpec((tm, d), lambda i: (i, 0)),
        out_shape=jax.ShapeDtypeStruct((s, d), jnp.float32),
        scratch_shapes=[pltpu.VMEM((SUBLANES, BR), jnp.float32), pltpu.VMEM((SUBLANES, BR), jnp.float32),
                        pltpu.VMEM((1, BR), jnp.float32)],
        compiler_params=_params("arbitrary"),
        name="out_proj_ln",
    )(x, proj, proj, proj, proj, conv_a, proj, proj, conv_c, conv_c_b, lru_w, lru_ba, lru_bx, lru_lambda,
      yb, yd, w_out_bf16, ada, ln_g, ln_b)


def kernel(x, c, rel_bias, w_ada, b_ada, w_in, conv_a, conv_c, conv_c_b, lru_wa, lru_ba, lru_wx, lru_bx, lru_lambda, s5_lam_re, s5_lam_im, s5_log_dt, s5_b_re, s5_b_im, s5_c_re, s5_c_im, s5_d, s5_w_glu, s5_b_glu, w_out, ln_g, ln_b):
    bsz, s, d = x.shape
    assert bsz == 1 and w_in.shape == (DEPTH, d, N_IN)
    xs = x.reshape(s, d)
    w_out_bf16 = w_out.astype(jnp.bfloat16) if DEPTH == 1 else None
    w_glu_bf16 = s5_w_glu.astype(jnp.bfloat16)
    eye = jnp.eye(LRU_HEADS, dtype=jnp.float32)
    lru_w = jnp.einsum('lkhij,hg->lhikgj', jnp.stack([lru_wa, lru_wx], axis=1), eye)
    lru_w = lru_w.reshape(DEPTH, BR, 2 * BR).astype(jnp.bfloat16)
    bmat, cmat, a_re, a_im = _s5_discretize(s5_lam_re, s5_lam_im, s5_log_dt, s5_b_re, s5_b_im, s5_c_re, s5_c_im)
    c_col = c.reshape(d, 1)
    ada = _ada(c_col, w_ada, b_ada)

    for l in range(DEPTH):
        proj = _proj(xs, ada, w_in, l)
        if l + 1 < DEPTH:
            yb, ada_next, w_out_cast = _attn(proj, rel_bias, (c_col, w_ada, b_ada, l + 1, w_out))
            w_out_bf16 = w_out_cast if l == 0 else w_out_bf16
        else:
            yb, ada_next = _attn(proj, rel_bias), None
        yd = _s5(proj, bmat, a_re, a_im, cmat, s5_d, w_glu_bf16, s5_b_glu, l)
        xs = _out(xs, proj, conv_a, conv_c, conv_c_b, lru_w, lru_ba, lru_bx, lru_lambda, yb, yd, w_out_bf16, ada,
                  ln_g, ln_b, l)
        ada = ada_next
    return xs.reshape(bsz, s, d)
```
